```python
import jax
import jax.numpy as jnp
from jax import lax
import numpy as np

D_MODEL = 2048
BATCH = 1
SEQ = 8192
DEPTH = 4
DEC_BATCH = 8
DEC_SEQ = 64
PAST_LEN = 1024

CHUNK = 64
N_MIXERS = 2
N_ATTN_LAYERS = (DEPTH + 1) // 2
N_REC_LAYERS = DEPTH // 2
ATTN_HEADS = 16
ATTN_KV_HEADS = 4
ATTN_HEAD_DIM = D_MODEL // ATTN_HEADS
ATTN_GROUP = ATTN_HEADS // ATTN_KV_HEADS
IDX_HEADS = 16
IDX_HEAD_DIM = 64
TOPK_MAX = 256
Q_BLOCK = 128
HGRN_HEAD_DIM = 128
HGRN_HEADS = D_MODEL // HGRN_HEAD_DIM
D_FF = 4 * D_MODEL
RMS_EPS = 1e-6
ATTN_Q_DIM = ATTN_HEADS * ATTN_HEAD_DIM
ATTN_KV_DIM = ATTN_KV_HEADS * ATTN_HEAD_DIM
IDX_Q_DIM = IDX_HEADS * IDX_HEAD_DIM
ATTN_IN_DIM = ATTN_Q_DIM + 2 * ATTN_KV_DIM + IDX_Q_DIM + IDX_HEAD_DIM + IDX_HEADS
REC_IN_DIM = 4 * D_MODEL

kernel_name = 'hybrid_dsa_hgrn2_stream_step'


def rms_norm(x, g):
    xf = x.astype(jnp.float32)
    y = xf * lax.rsqrt(jnp.mean(xf * xf, axis=-1, keepdims=True) + RMS_EPS)
    return (y * g.astype(jnp.float32)).astype(x.dtype)


def sq_relu_mlp(h, w_up, w_down):
    u = jax.nn.relu(jnp.einsum('bld,df->blf', h, w_up))
    return jnp.einsum('blf,fd->bld', u * u, w_down)


def attn_project(h, w_in):
    b, l, _ = h.shape
    p = jnp.einsum('bld,de->ble', h, w_in)
    o1 = ATTN_Q_DIM
    o2 = o1 + ATTN_KV_DIM
    o3 = o2 + ATTN_KV_DIM
    o4 = o3 + IDX_Q_DIM
    o5 = o4 + IDX_HEAD_DIM
    q = p[..., :o1].reshape(b, l, ATTN_HEADS, ATTN_HEAD_DIM)
    k = p[..., o1:o2].reshape(b, l, ATTN_KV_HEADS, ATTN_HEAD_DIM)
    v = p[..., o2:o3].reshape(b, l, ATTN_KV_HEADS, ATTN_HEAD_DIM)
    q_idx = p[..., o3:o4].reshape(b, l, IDX_HEADS, IDX_HEAD_DIM)
    k_idx = p[..., o4:o5]
    w_idx = p[..., o5:]
    return q, k, v, q_idx, k_idx, w_idx


def dsa_attend(q, q_idx, w_idx, q_pos, k, v, k_idx, k_top):
    b, tq = q.shape[:2]
    l = k.shape[1]
    q_chunk = q_pos // CHUNK
    admissible = (jnp.arange(l, dtype=jnp.int32)[None, :] // CHUNK) <= q_chunk[:, None]
    dots = jnp.einsum('bqhd,bsd->bqhs', q_idx, k_idx).astype(jnp.float32) * (IDX_HEAD_DIM ** -0.5)
    score = jnp.einsum('bqh,bqhs->bqs', w_idx.astype(jnp.float32) * (IDX_HEADS ** -0.5), jax.nn.relu(dots))
    score = jnp.where(admissible[None], score, -jnp.inf)
    _, sel = lax.top_k(score, k_top)
    sel_ok = (sel // CHUNK) <= q_chunk[None, :, None]
    k_sel = jax.vmap(lambda kb, sb: kb[sb])(k, sel)
    v_sel = jax.vmap(lambda vb, sb: vb[sb])(v, sel)
    qg = q.reshape(b, tq, ATTN_KV_HEADS, ATTN_GROUP, ATTN_HEAD_DIM)
    logits = jnp.einsum('bqkgd,bqjkd->bqkgj', qg, k_sel).astype(jnp.float32) * (ATTN_HEAD_DIM ** -0.5)
    logits = jnp.where(sel_ok[:, :, None, None, :], logits, -jnp.inf)
    prob = jax.nn.softmax(logits, axis=-1).astype(v.dtype)
    out = jnp.einsum('bqkgj,bqjkd->bqkgd', prob, v_sel)
    return out.reshape(b, tq, ATTN_Q_DIM)


def attn_mixer_prompt(h, w_in, w_out):
    b, l, _ = h.shape
    q, k, v, q_idx, k_idx, w_idx = attn_project(h, w_in)
    k_top = min(TOPK_MAX, l // 4)
    nb = l // Q_BLOCK

    def blocks(a):
        return jnp.moveaxis(a.reshape((b, nb, Q_BLOCK) + a.shape[2:]), 1, 0)

    pos = jnp.arange(l, dtype=jnp.int32).reshape(nb, Q_BLOCK)

    def one_block(args):
        qb, qib, wb, pb = args
        return dsa_attend(qb, qib, wb, pb, k, v, k_idx, k_top)

    o = lax.map(one_block, (blocks(q), blocks(q_idx), blocks(w_idx), pos))
    o = jnp.moveaxis(o, 0, 1).reshape(b, l, ATTN_Q_DIM)
    return jnp.einsum('ble,ed->bld', o, w_out), k, v, k_idx


def attn_mixer_sample(h, w_in, w_out, cache_k, cache_v, cache_kidx):
    b, t, _ = h.shape
    past = cache_k.shape[1]
    q, k, v, q_idx, k_idx, w_idx = attn_project(h, w_in)
    k_all = jnp.concatenate([cache_k.astype(k.dtype), k], axis=1)
    v_all = jnp.concatenate([cache_v.astype(v.dtype), v], axis=1)
    kidx_all = jnp.concatenate([cache_kidx.astype(k_idx.dtype), k_idx], axis=1)
    k_top = min(TOPK_MAX, (past + t) // 4)
    pos = past + jnp.arange(t, dtype=jnp.int32)
    o = dsa_attend(q, q_idx, w_idx, pos, k_all, v_all, kidx_all, k_top)
    return jnp.einsum('ble,ed->bld', o, w_out), k, v, k_idx


def hgrn_chunk_scan(q, k, v, log_f, s0):
    b, h, l, dk = q.shape
    c = min(CHUNK, l)
    n = l // c

    def blocks(a):
        return jnp.moveaxis(a.reshape(b, h, n, c, a.shape[-1]), 2, 0)

    causal = jnp.tril(jnp.ones((c, c), dtype=bool))

    def step(s, xs):
        qc, kc, vc, lfc = xs
        g = jnp.cumsum(lfc, axis=2)
        rel = g[:, :, :, None, :] - g[:, :, None, :, :]
        decay = jnp.exp(jnp.where(causal[:, :, None], rel, -jnp.inf))
        scores = jnp.einsum('bhtd,bhsd,bhtsd->bhts', qc, kc, decay)
        o_intra = jnp.einsum('bhts,bhsv->bhtv', scores, vc)
        o_inter = jnp.einsum('bhtd,bhdv->bhtv', qc * jnp.exp(g), s)
        g_last = g[:, :, -1:, :]
        s_new = jnp.exp(g_last[:, :, 0, :])[..., None] * s + jnp.einsum('bhsd,bhsv->bhdv', kc * jnp.exp(g_last - g), vc)
        return s_new, o_intra + o_inter

    s_fin, o = lax.scan(step, s0, (blocks(q), blocks(k), blocks(v), blocks(log_f)))
    o = jnp.moveaxis(o, 0, 2).reshape(b, h, l, v.shape[-1])
    return o, s_fin


def hgrn_mixer(h, w_in, w_out, gnorm, lb, s0):
    b, l, _ = h.shape
    p = jnp.einsum('bld,de->ble', h, w_in).astype(jnp.float32)
    q, f_logit, i_in, gate = jnp.split(p, 4, axis=-1)
    lb32 = lb.astype(jnp.float32)
    f = lb32 + (1.0 - lb32) * jax.nn.sigmoid(f_logit)

    def heads(a):
        return a.reshape(b, l, HGRN_HEADS, HGRN_HEAD_DIM).transpose(0, 2, 1, 3)

    o, s_fin = hgrn_chunk_scan(heads(q), heads(1.0 - f), heads(i_in), heads(jnp.log(f)), s0.astype(jnp.float32))
    o = o.transpose(0, 2, 1, 3)
    o = rms_norm(o, gnorm.reshape(HGRN_HEADS, HGRN_HEAD_DIM)).reshape(b, l, D_MODEL)
    o = (o * jax.nn.sigmoid(gate)).astype(h.dtype)
    return jnp.einsum('ble,ed->bld', o, w_out), s_fin


def setup_inputs(seed: int = 0) -> dict:
    key = jax.random.key(seed)
    ks = jax.random.split(key, 20)
    f32 = jnp.float32
    d = D_MODEL

    def nrm(k, shape, scale):
        return jax.random.normal(k, shape, f32) * scale

    return {
        'x_prompt': nrm(ks[0], (BATCH, SEQ, d), 1.0),
        'x_sample': nrm(ks[1], (DEC_BATCH, DEC_SEQ, d), 1.0),
        'cache_k': nrm(ks[2], (N_ATTN_LAYERS, DEC_BATCH, PAST_LEN, ATTN_KV_HEADS, ATTN_HEAD_DIM), 1.0),
        'cache_v': nrm(ks[3], (N_ATTN_LAYERS, DEC_BATCH, PAST_LEN, ATTN_KV_HEADS, ATTN_HEAD_DIM), 1.0),
        'cache_kidx': nrm(ks[4], (N_ATTN_LAYERS, DEC_BATCH, PAST_LEN, IDX_HEAD_DIM), 1.0),
        'state_s': nrm(ks[5], (N_REC_LAYERS, DEC_BATCH, HGRN_HEADS, HGRN_HEAD_DIM, HGRN_HEAD_DIM), 0.5),
        'norm_mix': 1.0 + nrm(ks[6], (DEPTH, d), 0.1),
        'norm_mlp': 1.0 + nrm(ks[7], (DEPTH, d), 0.1),
        'norm_final': 1.0 + nrm(ks[8], (d,), 0.1),
        'attn_w_in': nrm(ks[9], (N_ATTN_LAYERS, d, ATTN_IN_DIM), d ** -0.5),
        'attn_w_out': nrm(ks[10], (N_ATTN_LAYERS, ATTN_Q_DIM, d), ATTN_Q_DIM ** -0.5),
        'rec_w_in': nrm(ks[11], (N_REC_LAYERS, d, REC_IN_DIM), d ** -0.5),
        'rec_w_out': nrm(ks[12], (N_REC_LAYERS, d, d), d ** -0.5),
        'rec_gnorm': 1.0 + nrm(ks[13], (N_REC_LAYERS, d), 0.1),
        'rec_lb': nrm(ks[14], (N_REC_LAYERS, d), 0.5),
        'mlp_w_up': nrm(ks[15], (DEPTH, d, D_FF), d ** -0.5),
        'mlp_w_down': nrm(ks[16], (DEPTH, D_FF, d), D_FF ** -0.5),
    }


def reference(x_prompt, x_sample, cache_k, cache_v, cache_kidx, state_s, norm_mix, norm_mlp, norm_final,
              attn_w_in, attn_w_out, rec_w_in, rec_w_out, rec_gnorm, rec_lb, mlp_w_up, mlp_w_down):
    p_lb = jax.nn.softmax(rec_lb.astype(jnp.float32), axis=0)
    lower_bounds = jnp.cumsum(p_lb, axis=0) - p_lb[0:1]
    xp, xs = x_prompt, x_sample
    kp, vp, kip, sp = [], [], [], []
    ksl, vsl, kisl, ssl = [], [], [], []
    for layer in range(DEPTH):
        hp = rms_norm(xp, norm_mix[layer])
        hs = rms_norm(xs, norm_mix[layer])
        j = layer // N_MIXERS
        if layer % N_MIXERS == 0:
            mp, k_p, v_p, ki_p = attn_mixer_prompt(hp, attn_w_in[j], attn_w_out[j])
            ms, k_s, v_s, ki_s = attn_mixer_sample(hs, attn_w_in[j], attn_w_out[j], cache_k[j], cache_v[j], cache_kidx[j])
            kp.append(k_p)
            vp.append(v_p)
            kip.append(ki_p)
            ksl.append(k_s)
            vsl.append(v_s)
            kisl.append(ki_s)
        else:
            s0 = jnp.zeros((xp.shape[0], HGRN_HEADS, HGRN_HEAD_DIM, HGRN_HEAD_DIM), jnp.float32)
            mp, s_p = hgrn_mixer(hp, rec_w_in[j], rec_w_out[j], rec_gnorm[j], lower_bounds[j], s0)
            ms, s_s = hgrn_mixer(hs, rec_w_in[j], rec_w_out[j], rec_gnorm[j], lower_bounds[j], state_s[j])
            sp.append(s_p)
            ssl.append(s_s)
        xp = xp + mp
        xs = xs + ms
        xp = xp + sq_relu_mlp(rms_norm(xp, norm_mlp[layer]), mlp_w_up[layer], mlp_w_down[layer])
        xs = xs + sq_relu_mlp(rms_norm(xs, norm_mlp[layer]), mlp_w_up[layer], mlp_w_down[layer])
    y_prompt = rms_norm(xp, norm_final)
    y_sample = rms_norm(xs, norm_final)
    return (y_prompt, y_sample, jnp.stack(kp), jnp.stack(vp), jnp.stack(kip), jnp.stack(sp),
            jnp.stack(ksl), jnp.stack(vsl), jnp.stack(kisl), jnp.stack(ssl))
```

```python
import functools

import numpy as np
import jax
import jax.numpy as jnp
from jax import lax
from jax.experimental import pallas as pl
from jax.experimental.pallas import tpu as pltpu

D_MODEL = 2048
DEPTH = 4
CHUNK = 64
N_MIXERS = 2
ATTN_HEADS = 16
ATTN_KV_HEADS = 4
ATTN_HEAD_DIM = D_MODEL // ATTN_HEADS
ATTN_GROUP = ATTN_HEADS // ATTN_KV_HEADS
IDX_HEADS = 16
IDX_HEAD_DIM = 64
TOPK_MAX = 256
HGRN_HEAD_DIM = 128
HGRN_HEADS = D_MODEL // HGRN_HEAD_DIM
D_FF = 4 * D_MODEL
RMS_EPS = 1e-6
ATTN_Q_DIM = ATTN_HEADS * ATTN_HEAD_DIM
ATTN_KV_DIM = ATTN_KV_HEADS * ATTN_HEAD_DIM
IDX_Q_DIM = IDX_HEADS * IDX_HEAD_DIM

LANES = 128
VMEM_LIMIT = 56 * 1024 * 1024

F32 = jnp.float32
BF16 = jnp.bfloat16
INT_MIN = np.int32(-(2 ** 31))
NEG_BIG = -1e30

_NT = (((1,), (1,)), ((), ()))


def _params(*sem):
    return pltpu.CompilerParams(dimension_semantics=sem, vmem_limit_bytes=VMEM_LIMIT)


def _rms(x, g):
    ms = jnp.mean(x * x, axis=-1, keepdims=True)
    return (x * lax.rsqrt(ms + RMS_EPS)) * g


def _norm_matmul_kernel(x_ref, g_ref, w_ref, o_ref, h_ref):
    @pl.when(pl.program_id(1) == 0)
    def _():
        h_ref[...] = _rms(x_ref[...], g_ref[...]).astype(BF16)

    o_ref[...] = jnp.dot(h_ref[...], w_ref[...], preferred_element_type=F32).astype(o_ref.dtype)


def norm_matmul(x, g, w, out_dtype, tm=512, tn=512):
    m, d = x.shape
    n = w.shape[1]
    tn = min(tn, n)
    assert m % tm == 0 and n % tn == 0
    return pl.pallas_call(
        _norm_matmul_kernel,
        grid=(m // tm, n // tn),
        in_specs=[
            pl.BlockSpec((tm, d), lambda i, j: (i, 0)),
            pl.BlockSpec((1, d), lambda i, j: (0, 0)),
            pl.BlockSpec((d, tn), lambda i, j: (0, j)),
        ],
        out_specs=pl.BlockSpec((tm, tn), lambda i, j: (i, j)),
        out_shape=jax.ShapeDtypeStruct((m, n), out_dtype),
        scratch_shapes=[pltpu.VMEM((tm, d), BF16)],
        compiler_params=_params("parallel", "arbitrary"),
    )(x, g.reshape(1, d), w)


def _matmul_res_kernel(a_ref, w_ref, r_ref, o_ref):
    o_ref[...] = r_ref[...] + jnp.dot(a_ref[...], w_ref[...], preferred_element_type=F32)


def matmul_residual(a, w, res, tm=512, tn=512):
    m, k = a.shape
    n = w.shape[1]
    assert m % tm == 0 and n % tn == 0
    return pl.pallas_call(
        _matmul_res_kernel,
        grid=(m // tm, n // tn),
        in_specs=[
            pl.BlockSpec((tm, k), lambda i, j: (i, 0)),
            pl.BlockSpec((k, tn), lambda i, j: (0, j)),
            pl.BlockSpec((tm, tn), lambda i, j: (i, j)),
        ],
        out_specs=pl.BlockSpec((tm, tn), lambda i, j: (i, j)),
        out_shape=jax.ShapeDtypeStruct((m, n), F32),
        compiler_params=_params("parallel", "parallel"),
    )(a, w, res)


def _mlp_kernel(x_ref, g_ref, wu_ref, wd_ref, o_ref, h_ref):
    @pl.when(pl.program_id(1) == 0)
    def _():
        x = x_ref[...]
        h_ref[...] = _rms(x, g_ref[...]).astype(BF16)
        o_ref[...] = x

    u = jnp.maximum(jnp.dot(h_ref[...], wu_ref[...], preferred_element_type=F32), 0.0)
    o_ref[...] += jnp.dot((u * u).astype(BF16), wd_ref[...], preferred_element_type=F32)


def mlp_residual(x, g, w_up, w_down, tm=512, tf=512):
    m, d = x.shape
    f = w_up.shape[1]
    assert m % tm == 0 and f % tf == 0
    return pl.pallas_call(
        _mlp_kernel,
        grid=(m // tm, f // tf),
        in_specs=[
            pl.BlockSpec((tm, d), lambda i, j: (i, 0)),
            pl.BlockSpec((1, d), lambda i, j: (0, 0)),
            pl.BlockSpec((d, tf), lambda i, j: (0, j)),
            pl.BlockSpec((tf, d), lambda i, j: (j, 0)),
        ],
        out_specs=pl.BlockSpec((tm, d), lambda i, j: (i, 0)),
        out_shape=jax.ShapeDtypeStruct((m, d), F32),
        scratch_shapes=[pltpu.VMEM((tm, d), BF16)],
        compiler_params=_params("parallel", "arbitrary"),
    )(x, g.reshape(1, d), w_up, w_down)


def _final_norm_kernel(x_ref, g_ref, o_ref):
    o_ref[...] = _rms(x_ref[...], g_ref[...])


def final_norm(x, g, tm=512):
    m, d = x.shape
    return pl.pallas_call(
        _final_norm_kernel,
        grid=(m // tm,),
        in_specs=[pl.BlockSpec((tm, d), lambda i: (i, 0)), pl.BlockSpec((1, d), lambda i: (0, 0))],
        out_specs=pl.BlockSpec((tm, d), lambda i: (i, 0)),
        out_shape=jax.ShapeDtypeStruct((m, d), F32),
        compiler_params=_params("parallel"),
    )(x, g.reshape(1, d))


def _attn_kernel(q_ref, qi_ref, kx_ref, k_ref, v_ref, ki_ref, o_ref,
                 key_ref, wb_ref, m_ref, l_ref, acc_ref, *, tq, tk, q_pos0, k_top, n_kb_max):
    q_start = q_pos0 + pl.program_id(1) * tq
    n_adm = ((q_start + tq - 1) // CHUNK + 1) * CHUNK
    n_kb = jnp.minimum((n_adm + tk - 1) // tk, n_kb_max)

    q_chunk = (q_start + lax.broadcasted_iota(jnp.int32, (tq, LANES), 0)) // CHUNK
    lane = lax.broadcasted_iota(jnp.int32, (tq, LANES), 1)

    w = kx_ref[0][:, IDX_HEAD_DIM:IDX_HEAD_DIM + IDX_HEADS] * (IDX_HEADS ** -0.5 * IDX_HEAD_DIM ** -0.5)
    for h in range(IDX_HEADS):
        wb_ref[h] = jnp.broadcast_to(w[:, h:h + 1], (tq, LANES))

    qi = qi_ref[0].reshape(IDX_HEADS * tq, IDX_HEAD_DIM)

    def score_body(j, carry):
        kib = ki_ref[0, pl.ds(pl.multiple_of(j * tk, tk), tk), :]
        d = lax.dot_general(qi, kib, _NT, preferred_element_type=F32)
        for c in range(tk // LANES):
            sc = jnp.zeros((tq, LANES), F32)
            for h in range(IDX_HEADS):
                sc = sc + wb_ref[h] * jnp.maximum(d[h * tq:(h + 1) * tq, c * LANES:(c + 1) * LANES], 0.0)
            bits = pltpu.bitcast(sc, jnp.int32)
            key = bits ^ ((bits >> 31) & jnp.int32(0x7FFFFFFF))
            k_pos = j * tk + c * LANES + lane
            key = jnp.where((k_pos // CHUNK) <= q_chunk, key, INT_MIN)
            key_ref[j, :, c * LANES:(c + 1) * LANES] = key
        return carry

    lax.fori_loop(0, n_kb, score_body, 0)

    def count_ge(cand):
        def body(j, cnt):
            blk = key_ref[j]
            for c in range(tk // LANES):
                cnt = cnt + jnp.where(blk[:, c * LANES:(c + 1) * LANES] >= cand, 1.0, 0.0)
            return cnt
        cnt = lax.fori_loop(0, n_kb, body, jnp.zeros((tq, LANES), F32))
        return jnp.sum(cnt, axis=-1, keepdims=True)

    def bit_body(p, thr):
        cand = thr + jnp.left_shift(jnp.int32(1), 31 - p)
        return jnp.where(count_ge(cand) >= float(k_top), cand, thr)

    thr = lax.fori_loop(0, 32, bit_body, jnp.full((tq, 1), INT_MIN, jnp.int32))
    thr = jnp.maximum(thr, INT_MIN + 1)

    m_ref[...] = jnp.full(m_ref.shape, NEG_BIG, F32)
    l_ref[...] = jnp.zeros(l_ref.shape, F32)
    acc_ref[...] = jnp.zeros(acc_ref.shape, F32)
    scale = ATTN_HEAD_DIM ** -0.5
    rows = ATTN_GROUP * tq

    def attn_body(j, carry):
        bias = jnp.where(key_ref[j] >= thr, 0.0, NEG_BIG)
        k0 = pl.multiple_of(j * tk, tk)
        for g in range(ATTN_KV_HEADS):
            kb = k_ref[0, pl.ds(k0, tk), g * ATTN_HEAD_DIM:(g + 1) * ATTN_HEAD_DIM]
            vb = v_ref[0, pl.ds(k0, tk), g * ATTN_HEAD_DIM:(g + 1) * ATTN_HEAD_DIM]
            qg = q_ref[0, g * ATTN_GROUP:(g + 1) * ATTN_GROUP].reshape(rows, ATTN_HEAD_DIM)
            s = lax.dot_general(qg, kb, _NT, preferred_element_type=F32) * scale
            s = (s.reshape(ATTN_GROUP, tq, tk) + bias[None]).reshape(rows, tk)
            m_old = m_ref[g]
            m_new = jnp.maximum(m_old, jnp.max(s, axis=-1, keepdims=True))
            alpha = jnp.exp(m_old - m_new)
            p = jnp.exp(s - m_new)
            l_ref[g] = alpha * l_ref[g] + jnp.sum(p, axis=-1, keepdims=True)
            acc_ref[g] = alpha * acc_ref[g] + jnp.dot(p.astype(BF16), vb, preferred_element_type=F32)
            m_ref[g] = m_new
        return carry

    lax.fori_loop(0, n_kb, attn_body, 0)

    for g in range(ATTN_KV_HEADS):
        out = acc_ref[g] / l_ref[g]
        for hh in range(ATTN_GROUP):
            h = g * ATTN_GROUP + hh
            o_ref[0, :, h * ATTN_HEAD_DIM:(h + 1) * ATTN_HEAD_DIM] = out[hh * tq:(hh + 1) * tq].astype(o_ref.dtype)


def dsa_attention(q, qi, kx, k, v, ki, *, tq, tk, q_pos0, k_top):
    b, _, t, _ = q.shape
    s = k.shape[1]
    assert t % tq == 0 and s % tk == 0
    n_kb_max = s // tk
    rows = ATTN_GROUP * tq
    kern = functools.partial(_attn_kernel, tq=tq, tk=tk, q_pos0=q_pos0, k_top=k_top, n_kb_max=n_kb_max)
    return pl.pallas_call(
        kern,
        grid=(b, t // tq),
        in_specs=[
            pl.BlockSpec((1, ATTN_HEADS, tq, ATTN_HEAD_DIM), lambda bi, i: (bi, 0, i, 0)),
            pl.BlockSpec((1, IDX_HEADS, tq, IDX_HEAD_DIM), lambda bi, i: (bi, 0, i, 0)),
            pl.BlockSpec((1, tq, LANES), lambda bi, i: (bi, i, 0)),
            pl.BlockSpec((1, s, ATTN_KV_DIM), lambda bi, i: (bi, 0, 0)),
            pl.BlockSpec((1, s, ATTN_KV_DIM), lambda bi, i: (bi, 0, 0)),
            pl.BlockSpec((1, s, IDX_HEAD_DIM), lambda bi, i: (bi, 0, 0)),
        ],
        out_specs=pl.BlockSpec((1, tq, ATTN_Q_DIM), lambda bi, i: (bi, i, 0)),
        out_shape=jax.ShapeDtypeStruct((b, t, ATTN_Q_DIM), BF16),
        scratch_shapes=[
            pltpu.VMEM((n_kb_max, tq, tk), jnp.int32),
            pltpu.VMEM((IDX_HEADS, tq, LANES), F32),
            pltpu.VMEM((ATTN_KV_HEADS, rows, 1), F32),
            pltpu.VMEM((ATTN_KV_HEADS, rows, 1), F32),
            pltpu.VMEM((ATTN_KV_HEADS, rows, ATTN_HEAD_DIM), F32),
        ],
        compiler_params=_params("parallel", "parallel"),
    )(q, qi, kx, k, v, ki)


_LEVELS = (32, 16, 8, 4, 2, 1)


def _hgrn_tables():
    c = CHUNK
    t = np.arange(c)[:, None]
    u = np.arange(c)[None, :]
    mats = [(u <= t), (u > t)]
    masks = []
    for w in _LEVELS:
        r = (t // (2 * w)) * (2 * w) + w - 1
        upper = (t % (2 * w)) >= w
        mats.append(upper & (u > r) & (u <= t))
        mats.append((~upper) & (u > t) & (u <= r))
        s = np.arange(c)[None, :]
        masks.append(((t // (2 * w)) == (s // (2 * w))) & upper & ((s % (2 * w)) < w))
    masks.append(t == np.arange(c)[None, :])
    table = np.concatenate(mats, axis=0).astype(np.float32)
    return np.concatenate([table] * 3, axis=1), np.stack(masks).astype(np.float32)


def _hgrn_kernel(q_ref, f_ref, i_ref, gt_ref, lb_ref, gn_ref, tab_ref, msk_ref, s0_ref,
                 o_ref, sfin_ref, st_ref, *, layer):
    c = CHUNK
    dh = HGRN_HEAD_DIM
    step = pl.program_id(1)

    @pl.when(step == 0)
    def _():
        for h in range(HGRN_HEADS):
            st_ref[h] = s0_ref[0, h].T

    lb_all = lb_ref[...]
    e = jnp.exp(lb_all - jnp.max(lb_all, axis=0, keepdims=True))
    p_lb = e / jnp.sum(e, axis=0, keepdims=True)
    lower = jnp.sum(p_lb[:layer + 1], axis=0, keepdims=True) - p_lb[0:1]

    f = lower + (1.0 - lower) * jax.nn.sigmoid(f_ref[0])
    kk = 1.0 - f
    lf = jnp.log(f)
    lf_hi = lf.astype(BF16)
    r1 = lf - lf_hi.astype(F32)
    lf_mid = r1.astype(BF16)
    lf_lo = (r1 - lf_mid.astype(F32)).astype(BF16)
    lf3 = jnp.concatenate([lf_hi, lf_mid, lf_lo], axis=0)
    ex = jnp.exp(jnp.dot(tab_ref[...], lf3, preferred_element_type=F32))

    qq = q_ref[0]
    vv = i_ref[0]
    for h in range(HGRN_HEADS):
        sl = slice(h * dh, (h + 1) * dh)
        qh = qq[:, sl]
        kh = kk[:, sl]
        vh = vv[:, sl]
        vh16 = vh.astype(BF16)
        scores = msk_ref[len(_LEVELS)] * lax.dot_general(
            qh.astype(BF16), kh.astype(BF16), _NT, preferred_element_type=F32)
        for li in range(len(_LEVELS)):
            eq = ex[(2 + 2 * li) * c:(3 + 2 * li) * c, sl]
            ek = ex[(3 + 2 * li) * c:(4 + 2 * li) * c, sl]
            scores = scores + msk_ref[li] * lax.dot_general(
                (qh * eq).astype(BF16), (kh * ek).astype(BF16), _NT, preferred_element_type=F32)
        o_h = jnp.dot(scores.astype(BF16), vh16, preferred_element_type=F32)
        st = st_ref[h]
        o_h = o_h + lax.dot_general((qh * ex[0:c, sl]).astype(BF16), st.astype(BF16), _NT,
                                    preferred_element_type=F32)
        kd = (kh * ex[c:2 * c, sl]).astype(BF16)
        st_ref[h] = st * ex[c - 1:c, sl] + jnp.dot(vh.T.astype(BF16), kd, preferred_element_type=F32)
        o_n = _rms(o_h, gn_ref[:, sl]) * jax.nn.sigmoid(gt_ref[0][:, sl])
        o_ref[0, :, sl] = o_n.astype(o_ref.dtype)

    @pl.when(step == pl.num_programs(1) - 1)
    def _():
        for h in range(HGRN_HEADS):
            sfin_ref[0, h] = st_ref[h].T


def hgrn_scan(p, lb, gnorm, s0, *, layer):
    b, l, _ = p.shape
    d = D_MODEL
    c = CHUNK
    tab, msk = _hgrn_tables()
    kern = functools.partial(_hgrn_kernel, layer=layer)
    col = lambda j: pl.BlockSpec((1, c, d), lambda bi, t, j=j: (bi, t, j))
    state_spec = pl.BlockSpec((1, HGRN_HEADS, HGRN_HEAD_DIM, HGRN_HEAD_DIM), lambda bi, t: (bi, 0, 0, 0))
    return pl.pallas_call(
        kern,
        grid=(b, l // c),
        in_specs=[
            col(0), col(1), col(2), col(3),
            pl.BlockSpec(lb.shape, lambda bi, t: (0, 0)),
            pl.BlockSpec((1, d), lambda bi, t: (0, 0)),
            pl.BlockSpec(tab.shape, lambda bi, t: (0, 0)),
            pl.BlockSpec(msk.shape, lambda bi, t: (0, 0, 0)),
            state_spec,
        ],
        out_specs=[pl.BlockSpec((1, c, d), lambda bi, t: (bi, t, 0)), state_spec],
        out_shape=[jax.ShapeDtypeStruct((b, l, d), BF16), jax.ShapeDtypeStruct(s0.shape, F32)],
        scratch_shapes=[pltpu.VMEM((HGRN_HEADS, HGRN_HEAD_DIM, HGRN_HEAD_DIM), F32)],
        compiler_params=_params("parallel", "arbitrary"),
    )(p, p, p, p, lb, gnorm.reshape(1, d), jnp.asarray(tab, BF16), jnp.asarray(msk), s0)


def _head_major(a, b, t, heads, dim):
    return a.reshape(b, t, heads, dim).transpose(0, 2, 1, 3)


def _attn_layer(x, g, w_in, w_out, cache_k, cache_v, cache_kidx, n_p, b_s, t_s):
    o1 = ATTN_Q_DIM
    o2 = o1 + 2 * ATTN_KV_DIM
    o3 = o2 + IDX_Q_DIM
    w_in16 = w_in.astype(BF16)
    w_kx = jnp.pad(w_in16[:, o3:], ((0, 0), (0, LANES - IDX_HEAD_DIM - IDX_HEADS)))
    q = norm_matmul(x, g, w_in16[:, :o1], BF16)
    kv = norm_matmul(x, g, w_in16[:, o1:o2], F32)
    qi = norm_matmul(x, g, w_in16[:, o2:o3], BF16)
    kx = norm_matmul(x, g, w_kx, F32)

    k_new, v_new, ki_new = kv[:, :ATTN_KV_DIM], kv[:, ATTN_KV_DIM:], kx[:, :IDX_HEAD_DIM]
    past = cache_k.shape[1]

    o_p = dsa_attention(
        _head_major(q[:n_p], 1, n_p, ATTN_HEADS, ATTN_HEAD_DIM),
        _head_major(qi[:n_p], 1, n_p, IDX_HEADS, IDX_HEAD_DIM),
        kx[:n_p].reshape(1, n_p, LANES),
        k_new[:n_p].astype(BF16).reshape(1, n_p, ATTN_KV_DIM),
        v_new[:n_p].astype(BF16).reshape(1, n_p, ATTN_KV_DIM),
        ki_new[:n_p].astype(BF16).reshape(1, n_p, IDX_HEAD_DIM),
        tq=128, tk=256, q_pos0=0, k_top=min(TOPK_MAX, n_p // 4))

    tk_s = 256
    s_all = past + t_s
    s_pad = -(-s_all // tk_s) * tk_s

    def with_cache(cache, new, width):
        full = jnp.concatenate([cache.reshape(b_s, past, width).astype(BF16),
                                new[n_p:].astype(BF16).reshape(b_s, t_s, width)], axis=1)
        return jnp.pad(full, ((0, 0), (0, s_pad - s_all), (0, 0)))

    o_s = dsa_attention(
        _head_major(q[n_p:], b_s, t_s, ATTN_HEADS, ATTN_HEAD_DIM),
        _head_major(qi[n_p:], b_s, t_s, IDX_HEADS, IDX_HEAD_DIM),
        kx[n_p:].reshape(b_s, t_s, LANES),
        with_cache(cache_k, k_new, ATTN_KV_DIM),
        with_cache(cache_v, v_new, ATTN_KV_DIM),
        with_cache(cache_kidx, ki_new, IDX_HEAD_DIM),
        tq=t_s, tk=tk_s, q_pos0=past, k_top=min(TOPK_MAX, s_all // 4))

    o = jnp.concatenate([o_p.reshape(n_p, ATTN_Q_DIM), o_s.reshape(b_s * t_s, ATTN_Q_DIM)], axis=0)
    x = matmul_residual(o, w_out.astype(BF16), x)
    return x, k_new, v_new, ki_new


def _rec_layer(x, g, w_in, w_out, gnorm, rec_lb, state, layer, n_p, b_s, t_s):
    p = norm_matmul(x, g, w_in.astype(BF16), F32, tn=1024)
    s0_p = jnp.zeros((1,) + state.shape[1:], F32)
    o_p, s_p = hgrn_scan(p[:n_p].reshape(1, n_p, -1), rec_lb, gnorm, s0_p, layer=layer)
    o_s, s_s = hgrn_scan(p[n_p:].reshape(b_s, t_s, -1), rec_lb, gnorm, state, layer=layer)
    o = jnp.concatenate([o_p.reshape(n_p, D_MODEL), o_s.reshape(b_s * t_s, D_MODEL)], axis=0)
    x = matmul_residual(o, w_out.astype(BF16), x)
    return x, s_p, s_s


def kernel(x_prompt, x_sample, cache_k, cache_v, cache_kidx, state_s, norm_mix, norm_mlp, norm_final,
           attn_w_in, attn_w_out, rec_w_in, rec_w_out, rec_gnorm, rec_lb, mlp_w_up, mlp_w_down):
    b_p, l_p, d = x_prompt.shape
    b_s, t_s, _ = x_sample.shape
    assert b_p == 1
    n_p = b_p * l_p
    x = jnp.concatenate([x_prompt.reshape(n_p, d), x_sample.reshape(b_s * t_s, d)], axis=0)

    kp, vp, kip, sp = [], [], [], []
    ks, vs, kis, ss = [], [], [], []
    for layer in range(DEPTH):
        j = layer // N_MIXERS
        if layer % N_MIXERS == 0:
            x, k_new, v_new, ki_new = _attn_layer(
                x, norm_mix[layer], attn_w_in[j], attn_w_out[j],
                cache_k[j], cache_v[j], cache_kidx[j], n_p, b_s, t_s)
            kp.append(k_new[:n_p].reshape(b_p, l_p, ATTN_KV_HEADS, ATTN_HEAD_DIM))
            vp.append(v_new[:n_p].reshape(b_p, l_p, ATTN_KV_HEADS, ATTN_HEAD_DIM))
            kip.append(ki_new[:n_p].reshape(b_p, l_p, IDX_HEAD_DIM))
            ks.append(k_new[n_p:].reshape(b_s, t_s, ATTN_KV_HEADS, ATTN_HEAD_DIM))
            vs.append(v_new[n_p:].reshape(b_s, t_s, ATTN_KV_HEADS, ATTN_HEAD_DIM))
            kis.append(ki_new[n_p:].reshape(b_s, t_s, IDX_HEAD_DIM))
        else:
            x, s_p, s_s = _rec_layer(
                x, norm_mix[layer], rec_w_in[j], rec_w_out[j], rec_gnorm[j], rec_lb, state_s[j],
                j, n_p, b_s, t_s)
            sp.append(s_p)
            ss.append(s_s)
        x = mlp_residual(x, norm_mlp[layer], mlp_w_up[layer].astype(BF16), mlp_w_down[layer].astype(BF16))

    y = final_norm(x, norm_final)
    return (y[:n_p].reshape(b_p, l_p, d), y[n_p:].reshape(b_s, t_s, d),
            jnp.stack(kp), jnp.stack(vp), jnp.stack(kip), jnp.stack(sp),
            jnp.stack(ks), jnp.stack(vs), jnp.stack(kis), jnp.stack(ss))
```

```python
import functools

import numpy as np
import jax
import jax.numpy as jnp
from jax import lax
from jax.experimental import pallas as pl
from jax.experimental.pallas import tpu as pltpu

D_MODEL = 2048
DEPTH = 4
CHUNK = 64
N_MIXERS = 2
ATTN_HEADS = 16
ATTN_KV_HEADS = 4
ATTN_HEAD_DIM = D_MODEL // ATTN_HEADS
ATTN_GROUP = ATTN_HEADS // ATTN_KV_HEADS
IDX_HEADS = 16
IDX_HEAD_DIM = 64
TOPK_MAX = 256
HGRN_HEAD_DIM = 128
HGRN_HEADS = D_MODEL // HGRN_HEAD_DIM
D_FF = 4 * D_MODEL
RMS_EPS = 1e-6
ATTN_Q_DIM = ATTN_HEADS * ATTN_HEAD_DIM
ATTN_KV_DIM = ATTN_KV_HEADS * ATTN_HEAD_DIM
IDX_Q_DIM = IDX_HEADS * IDX_HEAD_DIM

LANES = 128
SUBLANES = 8
VMEM_LIMIT = 56 * 1024 * 1024

F32 = jnp.float32
BF16 = jnp.bfloat16
INT_MIN = np.int32(-(2 ** 31))
NEG_BIG = -1e30
LOG2_E = 1.4426950408889634

_NT = (((1,), (1,)), ((), ()))


def _params(*sem):
    return pltpu.CompilerParams(dimension_semantics=sem, vmem_limit_bytes=VMEM_LIMIT)


def _rms(x, g):
    ms = jnp.mean(x * x, axis=-1, keepdims=True)
    return (x * lax.rsqrt(ms + RMS_EPS)) * g


def _norm_matmul_kernel(x_ref, g_ref, w_ref, o_ref, h_ref, *, head_dim):
    @pl.when(pl.program_id(1) == 0)
    def _():
        h_ref[...] = _rms(x_ref[...], g_ref[...]).astype(BF16)

    res = jnp.dot(h_ref[...], w_ref[...], preferred_element_type=F32)
    if head_dim is None:
        o_ref[...] = res.astype(o_ref.dtype)
    else:
        for hh in range(o_ref.shape[0]):
            o_ref[hh] = res[:, hh * head_dim:(hh + 1) * head_dim].astype(o_ref.dtype)


def norm_matmul(x, g, w, out_dtype, tm=512, tn=512, head_dim=None):
    m, d = x.shape
    n = w.shape[1]
    tn = min(tn, n)
    assert m % tm == 0 and n % tn == 0
    if head_dim is None:
        out_spec = pl.BlockSpec((tm, tn), lambda i, j: (i, j))
        out_shape = jax.ShapeDtypeStruct((m, n), out_dtype)
    else:
        assert tn % head_dim == 0
        out_spec = pl.BlockSpec((tn // head_dim, tm, head_dim), lambda i, j: (j, i, 0))
        out_shape = jax.ShapeDtypeStruct((n // head_dim, m, head_dim), out_dtype)
    return pl.pallas_call(
        functools.partial(_norm_matmul_kernel, head_dim=head_dim),
        grid=(m // tm, n // tn),
        in_specs=[
            pl.BlockSpec((tm, d), lambda i, j: (i, 0)),
            pl.BlockSpec((1, d), lambda i, j: (0, 0)),
            pl.BlockSpec((d, tn), lambda i, j: (0, j)),
        ],
        out_specs=out_spec,
        out_shape=out_shape,
        scratch_shapes=[pltpu.VMEM((tm, d), BF16)],
        compiler_params=_params("parallel", "arbitrary"),
    )(x, g.reshape(1, d), w)


def _matmul_res_kernel(a_ref, w_ref, r_ref, o_ref):
    o_ref[...] = r_ref[...] + jnp.dot(a_ref[...], w_ref[...], preferred_element_type=F32)


def matmul_residual(a, w, res, tm=512, tn=512):
    m, k = a.shape
    n = w.shape[1]
    assert m % tm == 0 and n % tn == 0
    return pl.pallas_call(
        _matmul_res_kernel,
        grid=(m // tm, n // tn),
        in_specs=[
            pl.BlockSpec((tm, k), lambda i, j: (i, 0)),
            pl.BlockSpec((k, tn), lambda i, j: (0, j)),
            pl.BlockSpec((tm, tn), lambda i, j: (i, j)),
        ],
        out_specs=pl.BlockSpec((tm, tn), lambda i, j: (i, j)),
        out_shape=jax.ShapeDtypeStruct((m, n), F32),
        compiler_params=_params("parallel", "parallel"),
    )(a, w, res)


def _mlp_kernel(x_ref, g_ref, wu_ref, wd_ref, o_ref, h_ref):
    @pl.when(pl.program_id(1) == 0)
    def _():
        x = x_ref[...]
        h_ref[...] = _rms(x, g_ref[...]).astype(BF16)
        o_ref[...] = x

    u = jnp.maximum(jnp.dot(h_ref[...], wu_ref[...], preferred_element_type=F32), 0.0)
    o_ref[...] += jnp.dot((u * u).astype(BF16), wd_ref[...], preferred_element_type=F32)


def mlp_residual(x, g, w_up, w_down, tm=512, tf=512):
    m, d = x.shape
    f = w_up.shape[1]
    assert m % tm == 0 and f % tf == 0
    return pl.pallas_call(
        _mlp_kernel,
        grid=(m // tm, f // tf),
        in_specs=[
            pl.BlockSpec((tm, d), lambda i, j: (i, 0)),
            pl.BlockSpec((1, d), lambda i, j: (0, 0)),
            pl.BlockSpec((d, tf), lambda i, j: (0, j)),
            pl.BlockSpec((tf, d), lambda i, j: (j, 0)),
        ],
        out_specs=pl.BlockSpec((tm, d), lambda i, j: (i, 0)),
        out_shape=jax.ShapeDtypeStruct((m, d), F32),
        scratch_shapes=[pltpu.VMEM((tm, d), BF16)],
        compiler_params=_params("parallel", "arbitrary"),
    )(x, g.reshape(1, d), w_up, w_down)


def _final_norm_kernel(x_ref, g_ref, o_ref):
    o_ref[...] = _rms(x_ref[...], g_ref[...])


def final_norm(x, g, tm=512):
    m, d = x.shape
    return pl.pallas_call(
        _final_norm_kernel,
        grid=(m // tm,),
        in_specs=[pl.BlockSpec((tm, d), lambda i: (i, 0)), pl.BlockSpec((1, d), lambda i: (0, 0))],
        out_specs=pl.BlockSpec((tm, d), lambda i: (i, 0)),
        out_shape=jax.ShapeDtypeStruct((m, d), F32),
        compiler_params=_params("parallel"),
    )(x, g.reshape(1, d))


def _attn_kernel(q_ref, qi_ref, kx_ref, k_ref, vt_ref, ki_ref, o_ref,
                 key_ref, m_ref, l_ref, acc_ref, *, tq, tk, q_pos0, k_top, n_kb_max):
    q_start = q_pos0 + pl.program_id(1) * tq
    n_adm = ((q_start + tq - 1) // CHUNK + 1) * CHUNK
    n_kb = jnp.minimum((n_adm + tk - 1) // tk, n_kb_max)

    k_row = lax.broadcasted_iota(jnp.int32, (tk, tq), 0)
    k_lim = ((q_start + lax.broadcasted_iota(jnp.int32, (tk, tq), 1)) // CHUNK + 1) * CHUNK

    w_t = kx_ref[0].T[IDX_HEAD_DIM:IDX_HEAD_DIM + IDX_HEADS, :] * (IDX_HEADS ** -0.5 * IDX_HEAD_DIM ** -0.5)
    qi = qi_ref[0].reshape(IDX_HEADS * tq, IDX_HEAD_DIM)

    def score_body(j, carry):
        kib = ki_ref[0, pl.ds(pl.multiple_of(j * tk, tk), tk), :]
        d = lax.dot_general(kib, qi, _NT, preferred_element_type=F32)
        sc = jnp.zeros((tk, tq), F32)
        for h in range(IDX_HEADS):
            sc = sc + w_t[h:h + 1, :] * jnp.maximum(d[:, h * tq:(h + 1) * tq], 0.0)
        bits = pltpu.bitcast(sc, jnp.int32)
        key = bits ^ ((bits >> 31) & jnp.int32(0x7FFFFFFF))
        key_ref[j] = jnp.where(k_row < k_lim - j * tk, key, INT_MIN)
        return carry

    lax.fori_loop(0, n_kb, score_body, 0)

    n_acc = 8

    def count_where(pred):
        def body(j, cnt):
            hit = jnp.where(pred(key_ref[j], j), 1.0, 0.0)
            return cnt + jnp.sum(hit.reshape(tk // (n_acc * SUBLANES), n_acc * SUBLANES, tq), axis=0)
        cnt = lax.fori_loop(0, n_kb, body, jnp.zeros((n_acc * SUBLANES, tq), F32))
        return jnp.sum(cnt, axis=0, keepdims=True)

    def bit_body(p, state):
        thr, cnt_thr = state
        cand = thr + jnp.left_shift(jnp.int32(1), 31 - p)
        cnt = count_where(lambda blk, j: blk >= cand)
        keep = cnt >= float(k_top)
        return jnp.where(keep, cand, thr), jnp.where(keep, cnt, cnt_thr)

    thr0 = jnp.full((1, tq), INT_MIN, jnp.int32)
    thr_raw, cnt_thr = lax.fori_loop(0, 32, bit_body, (thr0, jnp.full((1, tq), 2.0 ** 30, F32)))
    thr = jnp.maximum(thr_raw, INT_MIN + 1)

    tied = jnp.logical_and(cnt_thr > float(k_top), thr_raw > INT_MIN)
    n_tied = jnp.max(jnp.where(tied, 1.0, 0.0))

    @pl.when(n_tied > 0.0)
    def _():
        need = float(k_top) - count_where(lambda blk, j: blk > thr)
        n_bits = max(1, int(n_kb_max * tk - 1).bit_length())

        def pos_body(b, last):
            step = jnp.left_shift(jnp.int32(1), n_bits - 1 - b)
            cand = last + step - 1
            got = count_where(lambda blk, j: jnp.logical_and(blk == thr, k_row + j * tk <= cand))
            return jnp.where(got < need, last + step, last)

        last = lax.fori_loop(0, n_bits, pos_body, jnp.zeros((1, tq), jnp.int32))

        def demote_body(j, carry):
            blk = key_ref[j]
            drop = jnp.logical_and(blk == thr, k_row + j * tk > last)
            key_ref[j] = jnp.where(drop, thr - 1, blk)
            return carry

        lax.fori_loop(0, n_kb, demote_body, 0)

    m_ref[...] = jnp.full(m_ref.shape, NEG_BIG, F32)
    l_ref[...] = jnp.zeros(l_ref.shape, F32)
    acc_ref[...] = jnp.zeros(acc_ref.shape, F32)
    c_exp = ATTN_HEAD_DIM ** -0.5 * LOG2_E
    cols = ATTN_GROUP * tq

    def attn_body(j, carry):
        bias = jnp.where(key_ref[j] >= thr, 0.0, NEG_BIG)
        bias = jnp.concatenate([bias] * ATTN_GROUP, axis=1)
        k0 = pl.multiple_of(j * tk, tk)
        for g in range(ATTN_KV_HEADS):
            kb = k_ref[0, pl.ds(k0, tk), g * ATTN_HEAD_DIM:(g + 1) * ATTN_HEAD_DIM]
            vtb = vt_ref[0, j, g * ATTN_HEAD_DIM:(g + 1) * ATTN_HEAD_DIM, :]
            qg = q_ref[0, g * ATTN_GROUP:(g + 1) * ATTN_GROUP].reshape(cols, ATTN_HEAD_DIM)
            s = lax.dot_general(kb, qg, _NT, preferred_element_type=F32) + bias
            m_old = m_ref[g]
            m_new = jnp.maximum(m_old, jnp.max(s, axis=0, keepdims=True))
            alpha = jnp.exp2((m_old - m_new) * c_exp)
            p = jnp.exp2((s - m_new) * c_exp)
            l_ref[g] = alpha * l_ref[g] + jnp.sum(p, axis=0, keepdims=True)
            acc_ref[g] = alpha * acc_ref[g] + jnp.dot(vtb, p.astype(BF16), preferred_element_type=F32)
            m_ref[g] = m_new
        return carry

    lax.fori_loop(0, n_kb, attn_body, 0)

    for g in range(ATTN_KV_HEADS):
        out_t = acc_ref[g] / l_ref[g]
        for hh in range(ATTN_GROUP):
            h = g * ATTN_GROUP + hh
            o_ref[0, :, h * ATTN_HEAD_DIM:(h + 1) * ATTN_HEAD_DIM] = (
                out_t[:, hh * tq:(hh + 1) * tq].T.astype(o_ref.dtype))


def dsa_attention(q, qi, kx, k, v, ki, *, n_q, tq, tk, q_pos0, k_top):
    b = q.shape[0]
    s = k.shape[1]
    assert n_q % tq == 0 and s % tk == 0 and tq == LANES
    n_kb_max = s // tk
    cols = ATTN_GROUP * tq
    vt = v.reshape(b, n_kb_max, tk, ATTN_KV_DIM).transpose(0, 1, 3, 2)
    kern = functools.partial(_attn_kernel, tq=tq, tk=tk, q_pos0=q_pos0, k_top=k_top, n_kb_max=n_kb_max)
    resident = dict(pipeline_mode=pl.Buffered(1))
    return pl.pallas_call(
        kern,
        grid=(b, n_q // tq),
        in_specs=[
            pl.BlockSpec((1, ATTN_HEADS, tq, ATTN_HEAD_DIM), lambda bi, i: (bi, 0, i, 0)),
            pl.BlockSpec((1, IDX_HEADS, tq, IDX_HEAD_DIM), lambda bi, i: (bi, 0, i, 0)),
            pl.BlockSpec((1, tq, LANES), lambda bi, i: (bi, i, 0)),
            pl.BlockSpec((1, s, ATTN_KV_DIM), lambda bi, i: (bi, 0, 0), **resident),
            pl.BlockSpec((1, n_kb_max, ATTN_KV_DIM, tk), lambda bi, i: (bi, 0, 0, 0), **resident),
            pl.BlockSpec((1, s, IDX_HEAD_DIM), lambda bi, i: (bi, 0, 0), **resident),
        ],
        out_specs=pl.BlockSpec((1, tq, ATTN_Q_DIM), lambda bi, i: (bi, i, 0)),
        out_shape=jax.ShapeDtypeStruct((b, n_q, ATTN_Q_DIM), BF16),
        scratch_shapes=[
            pltpu.VMEM((n_kb_max, tk, tq), jnp.int32),
            pltpu.VMEM((ATTN_KV_HEADS, 1, cols), F32),
            pltpu.VMEM((ATTN_KV_HEADS, 1, cols), F32),
            pltpu.VMEM((ATTN_KV_HEADS, ATTN_HEAD_DIM, cols), F32),
        ],
        compiler_params=_params("parallel", "parallel"),
    )(q, qi, kx, k, vt, ki)


_LEVELS = (32, 16, 8, 4, 2, 1)


def _hgrn_tables():
    c = CHUNK
    t = np.arange(c)[:, None]
    u = np.arange(c)[None, :]
    mats = [(u <= t), (u > t)]
    masks = []
    for w in _LEVELS:
        r = (t // (2 * w)) * (2 * w) + w - 1
        upper = (t % (2 * w)) >= w
        mats.append(upper & (u > r) & (u <= t))
        mats.append((~upper) & (u > t) & (u <= r))
        s = np.arange(c)[None, :]
        masks.append(((t // (2 * w)) == (s // (2 * w))) & upper & ((s % (2 * w)) < w))
    masks.append(t == np.arange(c)[None, :])
    table = np.concatenate(mats, axis=0).astype(np.float32)
    return np.concatenate([table] * 3, axis=1), np.stack(masks).astype(np.float32)


def _hgrn_kernel(q_ref, f_ref, i_ref, gt_ref, lb_ref, gn_ref, tab_ref, msk_ref, s0_ref,
                 o_ref, sfin_ref, st_ref, *, layer):
    c = CHUNK
    dh = HGRN_HEAD_DIM
    step = pl.program_id(1)

    @pl.when(step == 0)
    def _():
        for h in range(HGRN_HEADS):
            st_ref[h] = s0_ref[0, h].T

    lb_all = lb_ref[...]
    e = jnp.exp(lb_all - jnp.max(lb_all, axis=0, keepdims=True))
    p_lb = e / jnp.sum(e, axis=0, keepdims=True)
    lower = jnp.sum(p_lb[:layer + 1], axis=0, keepdims=True) - p_lb[0:1]

    f = lower + (1.0 - lower) * jax.nn.sigmoid(f_ref[0])
    kk = 1.0 - f
    lf = jnp.log(f)
    lf_hi = lf.astype(BF16)
    r1 = lf - lf_hi.astype(F32)
    lf_mid = r1.astype(BF16)
    lf_lo = (r1 - lf_mid.astype(F32)).astype(BF16)
    lf3 = jnp.concatenate([lf_hi, lf_mid, lf_lo], axis=0)
    ex = jnp.exp(jnp.dot(tab_ref[...], lf3, preferred_element_type=F32))

    qq = q_ref[0]
    vv = i_ref[0]
    for h in range(HGRN_HEADS):
        sl = slice(h * dh, (h + 1) * dh)
        qh = qq[:, sl]
        kh = kk[:, sl]
        vh = vv[:, sl]
        vh16 = vh.astype(BF16)
        scores = msk_ref[len(_LEVELS)] * lax.dot_general(
            qh.astype(BF16), kh.astype(BF16), _NT, preferred_element_type=F32)
        for li in range(len(_LEVELS)):
            eq = ex[(2 + 2 * li) * c:(3 + 2 * li) * c, sl]
            ek = ex[(3 + 2 * li) * c:(4 + 2 * li) * c, sl]
            scores = scores + msk_ref[li] * lax.dot_general(
                (qh * eq).astype(BF16), (kh * ek).astype(BF16), _NT, preferred_element_type=F32)
        o_h = jnp.dot(scores.astype(BF16), vh16, preferred_element_type=F32)
        st = st_ref[h]
        o_h = o_h + lax.dot_general((qh * ex[0:c, sl]).astype(BF16), st.astype(BF16), _NT,
                                    preferred_element_type=F32)
        kd = (kh * ex[c:2 * c, sl]).astype(BF16)
        st_ref[h] = st * ex[c - 1:c, sl] + jnp.dot(vh.T.astype(BF16), kd, preferred_element_type=F32)
        o_n = _rms(o_h, gn_ref[:, sl]) * jax.nn.sigmoid(gt_ref[0][:, sl])
        o_ref[0, :, sl] = o_n.astype(o_ref.dtype)

    @pl.when(step == pl.num_programs(1) - 1)
    def _():
        for h in range(HGRN_HEADS):
            sfin_ref[0, h] = st_ref[h].T


def hgrn_scan(p, lb, gnorm, s0, *, layer):
    b, l, _ = p.shape
    d = D_MODEL
    c = CHUNK
    tab, msk = _hgrn_tables()
    kern = functools.partial(_hgrn_kernel, layer=layer)
    col = lambda j: pl.BlockSpec((1, c, d), lambda bi, t, j=j: (bi, t, j))
    state_spec = pl.BlockSpec((1, HGRN_HEADS, HGRN_HEAD_DIM, HGRN_HEAD_DIM), lambda bi, t: (bi, 0, 0, 0))
    return pl.pallas_call(
        kern,
        grid=(b, l // c),
        in_specs=[
            col(0), col(1), col(2), col(3),
            pl.BlockSpec(lb.shape, lambda bi, t: (0, 0)),
            pl.BlockSpec((1, d), lambda bi, t: (0, 0)),
            pl.BlockSpec(tab.shape, lambda bi, t: (0, 0)),
            pl.BlockSpec(msk.shape, lambda bi, t: (0, 0, 0)),
            state_spec,
        ],
        out_specs=[pl.BlockSpec((1, c, d), lambda bi, t: (bi, t, 0)), state_spec],
        out_shape=[jax.ShapeDtypeStruct((b, l, d), BF16), jax.ShapeDtypeStruct(s0.shape, F32)],
        scratch_shapes=[pltpu.VMEM((HGRN_HEADS, HGRN_HEAD_DIM, HGRN_HEAD_DIM), F32)],
        compiler_params=_params("parallel", "arbitrary"),
    )(p, p, p, p, lb, gnorm.reshape(1, d), jnp.asarray(tab, BF16), jnp.asarray(msk), s0)


def _attn_layer(x, g, w_in, w_out, cache_k, cache_v, cache_kidx, n_p, b_s, t_s):
    o1 = ATTN_Q_DIM
    o2 = o1 + 2 * ATTN_KV_DIM
    o3 = o2 + IDX_Q_DIM
    m = x.shape[0]
    w_in16 = w_in.astype(BF16)
    w_kx = jnp.pad(w_in16[:, o3:], ((0, 0), (0, LANES - IDX_HEAD_DIM - IDX_HEADS)))
    q = norm_matmul(x, g, w_in16[:, :o1], BF16, head_dim=ATTN_HEAD_DIM)
    kv = norm_matmul(x, g, w_in16[:, o1:o2], F32)
    qi = norm_matmul(x, g, w_in16[:, o2:o3], BF16, head_dim=IDX_HEAD_DIM)
    kx = norm_matmul(x, g, w_kx, F32)

    k_new, v_new, ki_new = kv[:, :ATTN_KV_DIM], kv[:, ATTN_KV_DIM:], kx[:, :IDX_HEAD_DIM]
    past = cache_k.shape[1]
    tq = LANES

    o_p = dsa_attention(
        q[None], qi[None], kx[None],
        k_new[:n_p].astype(BF16)[None], v_new[:n_p].astype(BF16)[None], ki_new[:n_p].astype(BF16)[None],
        n_q=n_p, tq=tq, tk=512, q_pos0=0, k_top=min(TOPK_MAX, n_p // 4))

    tk_s = 256
    s_all = past + t_s
    n_adm_pad = ((past + tq - 1) // CHUNK + 1) * CHUNK
    s_pad = -(-max(s_all, n_adm_pad) // tk_s) * tk_s

    def with_cache(cache, new, width):
        full = jnp.concatenate([cache.reshape(b_s, past, width).astype(BF16),
                                new[n_p:].astype(BF16).reshape(b_s, t_s, width)], axis=1)
        return jnp.pad(full, ((0, 0), (0, s_pad - s_all), (0, 0)))

    def sample_heads(a, heads, dim):
        a = a[:, n_p:].reshape(heads, b_s, t_s, dim).transpose(1, 0, 2, 3)
        return jnp.pad(a, ((0, 0), (0, 0), (0, tq - t_s), (0, 0)))

    kx_s = jnp.pad(kx[n_p:].reshape(b_s, t_s, LANES), ((0, 0), (0, tq - t_s), (0, 0)))
    o_s = dsa_attention(
        sample_heads(q, ATTN_HEADS, ATTN_HEAD_DIM), sample_heads(qi, IDX_HEADS, IDX_HEAD_DIM), kx_s,
        with_cache(cache_k, k_new, ATTN_KV_DIM), with_cache(cache_v, v_new, ATTN_KV_DIM),
        with_cache(cache_kidx, ki_new, IDX_HEAD_DIM),
        n_q=tq, tq=tq, tk=tk_s, q_pos0=past, k_top=min(TOPK_MAX, s_all // 4))

    o = jnp.concatenate([o_p.reshape(n_p, ATTN_Q_DIM), o_s[:, :t_s].reshape(b_s * t_s, ATTN_Q_DIM)], axis=0)
    x = matmul_residual(o, w_out.astype(BF16), x)
    return x, k_new, v_new, ki_new


def _rec_layer(x, g, w_in, w_out, gnorm, rec_lb, state, layer, n_p, b_s, t_s):
    p = norm_matmul(x, g, w_in.astype(BF16), F32, tn=1024)
    s0_p = jnp.zeros((1,) + state.shape[1:], F32)
    o_p, s_p = hgrn_scan(p[:n_p].reshape(1, n_p, -1), rec_lb, gnorm, s0_p, layer=layer)
    o_s, s_s = hgrn_scan(p[n_p:].reshape(b_s, t_s, -1), rec_lb, gnorm, state, layer=layer)
    o = jnp.concatenate([o_p.reshape(n_p, D_MODEL), o_s.reshape(b_s * t_s, D_MODEL)], axis=0)
    x = matmul_residual(o, w_out.astype(BF16), x)
    return x, s_p, s_s


def kernel(x_prompt, x_sample, cache_k, cache_v, cache_kidx, state_s, norm_mix, norm_mlp, norm_final,
           attn_w_in, attn_w_out, rec_w_in, rec_w_out, rec_gnorm, rec_lb, mlp_w_up, mlp_w_down):
    b_p, l_p, d = x_prompt.shape
    b_s, t_s, _ = x_sample.shape
    assert b_p == 1
    n_p = b_p * l_p
    x = jnp.concatenate([x_prompt.reshape(n_p, d), x_sample.reshape(b_s * t_s, d)], axis=0)

    kp, vp, kip, sp = [], [], [], []
    ks, vs, kis, ss = [], [], [], []
    for layer in range(DEPTH):
        j = layer // N_MIXERS
        if layer % N_MIXERS == 0:
            x, k_new, v_new, ki_new = _attn_layer(
                x, norm_mix[layer], attn_w_in[j], attn_w_out[j],
                cache_k[j], cache_v[j], cache_kidx[j], n_p, b_s, t_s)
            kp.append(k_new[:n_p].reshape(b_p, l_p, ATTN_KV_HEADS, ATTN_HEAD_DIM))
            vp.append(v_new[:n_p].reshape(b_p, l_p, ATTN_KV_HEADS, ATTN_HEAD_DIM))
            kip.append(ki_new[:n_p].reshape(b_p, l_p, IDX_HEAD_DIM))
            ks.append(k_new[n_p:].reshape(b_s, t_s, ATTN_KV_HEADS, ATTN_HEAD_DIM))
            vs.append(v_new[n_p:].reshape(b_s, t_s, ATTN_KV_HEADS, ATTN_HEAD_DIM))
            kis.append(ki_new[n_p:].reshape(b_s, t_s, IDX_HEAD_DIM))
        else:
            x, s_p, s_s = _rec_layer(
                x, norm_mix[layer], rec_w_in[j], rec_w_out[j], rec_gnorm[j], rec_lb, state_s[j],
                j, n_p, b_s, t_s)
            sp.append(s_p)
            ss.append(s_s)
        x = mlp_residual(x, norm_mlp[layer], mlp_w_up[layer].astype(BF16), mlp_w_down[layer].astype(BF16))

    y = final_norm(x, norm_final)
    return (y[:n_p].reshape(b_p, l_p, d), y[n_p:].reshape(b_s, t_s, d),
            jnp.stack(kp), jnp.stack(vp), jnp.stack(kip), jnp.stack(sp),
            jnp.stack(ks), jnp.stack(vs), jnp.stack(kis), jnp.stack(ss))
```

```python
import functools

import numpy as np
import jax
import jax.numpy as jnp
from jax import lax
from jax.experimental import pallas as pl
from jax.experimental.pallas import tpu as pltpu

D_MODEL = 2048
DEPTH = 4
CHUNK = 64
N_MIXERS = 2
ATTN_HEADS = 16
ATTN_KV_HEADS = 4
ATTN_HEAD_DIM = D_MODEL // ATTN_HEADS
ATTN_GROUP = ATTN_HEADS // ATTN_KV_HEADS
IDX_HEADS = 16
IDX_HEAD_DIM = 64
TOPK_MAX = 256
HGRN_HEAD_DIM = 128
HGRN_HEADS = D_MODEL // HGRN_HEAD_DIM
D_FF = 4 * D_MODEL
RMS_EPS = 1e-6
ATTN_Q_DIM = ATTN_HEADS * ATTN_HEAD_DIM
ATTN_KV_DIM = ATTN_KV_HEADS * ATTN_HEAD_DIM
IDX_Q_DIM = IDX_HEADS * IDX_HEAD_DIM

LANES = 128
SUBLANES = 8
VMEM_LIMIT = 56 * 1024 * 1024

F32 = jnp.float32
BF16 = jnp.bfloat16
INT_MIN = np.int32(-(2 ** 31))
NEG_BIG = -1e30
LOG2_E = 1.4426950408889634

_NT = (((1,), (1,)), ((), ()))


def _params(*sem):
    return pltpu.CompilerParams(dimension_semantics=sem, vmem_limit_bytes=VMEM_LIMIT)


def _rms(x, g):
    ms = jnp.mean(x * x, axis=-1, keepdims=True)
    return (x * lax.rsqrt(ms + RMS_EPS)) * g


def _norm_matmul_kernel(x_ref, g_ref, w_ref, o_ref, h_ref, *, head_dim):
    @pl.when(pl.program_id(1) == 0)
    def _():
        h_ref[...] = _rms(x_ref[...], g_ref[...]).astype(BF16)

    res = jnp.dot(h_ref[...], w_ref[...].astype(BF16), preferred_element_type=F32)
    if head_dim is None:
        o_ref[...] = res.astype(o_ref.dtype)
    else:
        for hh in range(o_ref.shape[0]):
            o_ref[hh] = res[:, hh * head_dim:(hh + 1) * head_dim].astype(o_ref.dtype)


def _row_tile(m, cap=1088):
    return max(t for t in range(16, cap + 1, 16) if m % t == 0)


_ONE_BUFFER = dict(pipeline_mode=pl.Buffered(1))


def norm_matmul(x, g, w, out_dtype, *, layer=0, col0=0, n=None, tn=512, head_dim=None):
    m, d = x.shape
    n = w.shape[2] - col0 if n is None else n
    tm = _row_tile(m)
    tn = min(tn, n)
    assert n % tn == 0 and col0 % tn == 0
    jb0 = col0 // tn
    if head_dim is None:
        out_spec = pl.BlockSpec((tm, tn), lambda i, j: (i, j))
        out_shape = jax.ShapeDtypeStruct((m, n), out_dtype)
    else:
        assert tn % head_dim == 0
        out_spec = pl.BlockSpec((tn // head_dim, tm, head_dim), lambda i, j: (j, i, 0))
        out_shape = jax.ShapeDtypeStruct((n // head_dim, m, head_dim), out_dtype)
    return pl.pallas_call(
        functools.partial(_norm_matmul_kernel, head_dim=head_dim),
        grid=(m // tm, n // tn),
        in_specs=[
            pl.BlockSpec((tm, d), lambda i, j: (i, 0), **_ONE_BUFFER),
            pl.BlockSpec((1, d), lambda i, j: (0, 0)),
            pl.BlockSpec((None, d, tn), lambda i, j: (layer, 0, jb0 + j)),
        ],
        out_specs=out_spec,
        out_shape=out_shape,
        scratch_shapes=[pltpu.VMEM((tm, d), BF16)],
        compiler_params=_params("parallel", "arbitrary"),
    )(x, g.reshape(1, d), w)


def _matmul_res_kernel(a_ref, w_ref, r_ref, o_ref):
    o_ref[...] = r_ref[...] + jnp.dot(a_ref[...], w_ref[...].astype(BF16), preferred_element_type=F32)


def matmul_residual(a, w, res, *, layer, tn=512):
    m, k = a.shape
    n = w.shape[2]
    tm = _row_tile(m)
    assert n % tn == 0
    return pl.pallas_call(
        _matmul_res_kernel,
        grid=(m // tm, n // tn),
        in_specs=[
            pl.BlockSpec((tm, k), lambda i, j: (i, 0)),
            pl.BlockSpec((None, k, tn), lambda i, j: (layer, 0, j)),
            pl.BlockSpec((tm, tn), lambda i, j: (i, j)),
        ],
        out_specs=pl.BlockSpec((tm, tn), lambda i, j: (i, j)),
        out_shape=jax.ShapeDtypeStruct((m, n), F32),
        compiler_params=_params("parallel", "parallel"),
    )(a, w, res)


def _mlp_kernel(x_ref, g_ref, wu_ref, wd_ref, o_ref, h_ref):
    @pl.when(pl.program_id(1) == 0)
    def _():
        x = x_ref[...]
        h_ref[...] = _rms(x, g_ref[...]).astype(BF16)
        o_ref[...] = x

    u = jnp.maximum(jnp.dot(h_ref[...], wu_ref[...].astype(BF16), preferred_element_type=F32), 0.0)
    o_ref[...] += jnp.dot((u * u).astype(BF16), wd_ref[...].astype(BF16), preferred_element_type=F32)


def mlp_residual(x, g, w_up, w_down, *, layer, tf=512):
    m, d = x.shape
    f = w_up.shape[2]
    tm = _row_tile(m)
    assert f % tf == 0
    return pl.pallas_call(
        _mlp_kernel,
        grid=(m // tm, f // tf),
        in_specs=[
            pl.BlockSpec((tm, d), lambda i, j: (i, 0), **_ONE_BUFFER),
            pl.BlockSpec((1, d), lambda i, j: (0, 0)),
            pl.BlockSpec((None, d, tf), lambda i, j: (layer, 0, j)),
            pl.BlockSpec((None, tf, d), lambda i, j: (layer, j, 0)),
        ],
        out_specs=pl.BlockSpec((tm, d), lambda i, j: (i, 0), **_ONE_BUFFER),
        out_shape=jax.ShapeDtypeStruct((m, d), F32),
        scratch_shapes=[pltpu.VMEM((tm, d), BF16)],
        compiler_params=_params("parallel", "arbitrary"),
    )(x, g.reshape(1, d), w_up, w_down)


def _final_norm_kernel(x_ref, g_ref, o_ref):
    o_ref[...] = _rms(x_ref[...], g_ref[...])


def final_norm(x, g, tm=512):
    m, d = x.shape
    return pl.pallas_call(
        _final_norm_kernel,
        grid=(m // tm,),
        in_specs=[pl.BlockSpec((tm, d), lambda i: (i, 0)), pl.BlockSpec((1, d), lambda i: (0, 0))],
        out_specs=pl.BlockSpec((tm, d), lambda i: (i, 0)),
        out_shape=jax.ShapeDtypeStruct((m, d), F32),
        compiler_params=_params("parallel"),
    )(x, g.reshape(1, d))


def _attn_kernel(q_ref, qi_ref, kx_ref, k_ref, vt_ref, ki_ref, o_ref,
                 key_ref, m_ref, l_ref, acc_ref, kn_ref, *, tq, tk, q_pos0, k_top, n_kb_max):
    @pl.when(pl.program_id(1) == 0)
    def _():
        def kn_body(j, best):
            kf = k_ref[0, pl.ds(pl.multiple_of(j * tk, tk), tk), :].astype(F32)
            sq = kf * kf
            for g in range(ATTN_KV_HEADS):
                n2 = jnp.sum(sq[:, g * ATTN_HEAD_DIM:(g + 1) * ATTN_HEAD_DIM], axis=1, keepdims=True)
                best = jnp.maximum(best, jnp.max(n2, axis=0, keepdims=True))
            return best

        kn_ref[...] = jnp.broadcast_to(lax.fori_loop(0, n_kb_max, kn_body, jnp.zeros((1, 1), F32)), kn_ref.shape)

    q_start = q_pos0 + pl.program_id(1) * tq
    n_adm = ((q_start + tq - 1) // CHUNK + 1) * CHUNK
    n_kb = jnp.minimum((n_adm + tk - 1) // tk, n_kb_max)

    k_row = lax.broadcasted_iota(jnp.int32, (tk, tq), 0)
    k_lim = ((q_start + lax.broadcasted_iota(jnp.int32, (tk, tq), 1)) // CHUNK + 1) * CHUNK

    w_t = kx_ref[0].T[IDX_HEAD_DIM:IDX_HEAD_DIM + IDX_HEADS, :] * (IDX_HEADS ** -0.5 * IDX_HEAD_DIM ** -0.5)
    qi = qi_ref[0].reshape(IDX_HEADS * tq, IDX_HEAD_DIM)

    def score_body(j, carry):
        kib = ki_ref[0, pl.ds(pl.multiple_of(j * tk, tk), tk), :]
        d = lax.dot_general(kib, qi, _NT, preferred_element_type=F32)
        sc = jnp.zeros((tk, tq), F32)
        for h in range(IDX_HEADS):
            sc = sc + w_t[h:h + 1, :] * jnp.maximum(d[:, h * tq:(h + 1) * tq], 0.0)
        bits = pltpu.bitcast(sc, jnp.int32)
        key = bits ^ ((bits >> 31) & jnp.int32(0x7FFFFFFF))
        key_ref[j] = jnp.where(k_row < k_lim - j * tk, key, INT_MIN)
        return carry

    lax.fori_loop(0, n_kb, score_body, 0)

    n_acc = 8

    def count_where(pred):
        def body(j, cnt):
            hit = jnp.where(pred(key_ref[j], j), 1.0, 0.0)
            return cnt + jnp.sum(hit.reshape(tk // (n_acc * SUBLANES), n_acc * SUBLANES, tq), axis=0)
        cnt = lax.fori_loop(0, n_kb, body, jnp.zeros((n_acc * SUBLANES, tq), F32))
        return jnp.sum(cnt, axis=0, keepdims=True)

    def bit_body(p, state):
        thr, cnt_thr = state
        cand = thr + jnp.left_shift(jnp.int32(1), 31 - p)
        cnt = count_where(lambda blk, j: blk >= cand)
        keep = cnt >= float(k_top)
        return jnp.where(keep, cand, thr), jnp.where(keep, cnt, cnt_thr)

    thr0 = jnp.full((1, tq), INT_MIN, jnp.int32)
    thr_raw, cnt_thr = lax.fori_loop(0, 32, bit_body, (thr0, jnp.full((1, tq), 2.0 ** 30, F32)))
    thr = jnp.maximum(thr_raw, INT_MIN + 1)

    tied = jnp.logical_and(cnt_thr > float(k_top), thr_raw > INT_MIN)
    n_tied = jnp.max(jnp.where(tied, 1.0, 0.0))

    @pl.when(n_tied > 0.0)
    def _():
        need = float(k_top) - count_where(lambda blk, j: blk > thr)
        n_bits = max(1, int(n_kb_max * tk - 1).bit_length())

        def pos_body(b, last):
            step = jnp.left_shift(jnp.int32(1), n_bits - 1 - b)
            cand = last + step - 1
            got = count_where(lambda blk, j: jnp.logical_and(blk == thr, k_row + j * tk <= cand))
            return jnp.where(got < need, last + step, last)

        last = lax.fori_loop(0, n_bits, pos_body, jnp.zeros((1, tq), jnp.int32))

        def demote_body(j, carry):
            blk = key_ref[j]
            drop = jnp.logical_and(blk == thr, k_row + j * tk > last)
            key_ref[j] = jnp.where(drop, thr - 1, blk)
            return carry

        lax.fori_loop(0, n_kb, demote_body, 0)

    m_ref[...] = jnp.full(m_ref.shape, NEG_BIG, F32)
    l_ref[...] = jnp.zeros(l_ref.shape, F32)
    acc_ref[...] = jnp.zeros(acc_ref.shape, F32)
    c_exp = ATTN_HEAD_DIM ** -0.5 * LOG2_E
    cols = ATTN_GROUP * tq

    def logits_t(j, g):
        k0 = pl.multiple_of(j * tk, tk)
        kb = k_ref[0, pl.ds(k0, tk), g * ATTN_HEAD_DIM:(g + 1) * ATTN_HEAD_DIM]
        vtb = vt_ref[0, j, g * ATTN_HEAD_DIM:(g + 1) * ATTN_HEAD_DIM, :]
        qg = q_ref[0, g * ATTN_GROUP:(g + 1) * ATTN_GROUP].reshape(cols, ATTN_HEAD_DIM)
        return lax.dot_general(kb, qg, _NT, preferred_element_type=F32), vtb

    qf = q_ref[0].reshape(ATTN_HEADS * tq, ATTN_HEAD_DIM).astype(F32)
    qn2 = lax.dot_general(jnp.ones((SUBLANES, ATTN_HEAD_DIM), BF16), (qf * qf).astype(BF16), _NT,
                          preferred_element_type=F32)
    qn2_max = qn2[0:1, 0:tq]
    for h in range(1, ATTN_HEADS):
        qn2_max = jnp.maximum(qn2_max, qn2[0:1, h * tq:(h + 1) * tq])
    m_bound = jnp.sqrt(qn2_max * kn_ref[...]) * 1.02
    fixed_shift_ok = jnp.max(m_bound) * (2.0 * c_exp) < 100.0

    @pl.when(fixed_shift_ok)
    def _():
        shift = -c_exp * m_bound
        ones = jnp.ones((2 * SUBLANES, tk), BF16)

        def body(j, carry):
            bias = jnp.where(key_ref[j] >= thr, shift, NEG_BIG)
            bias = jnp.concatenate([bias] * ATTN_GROUP, axis=1)
            for g in range(ATTN_KV_HEADS):
                s, vtb = logits_t(j, g)
                p = jnp.exp2(s * c_exp + bias).astype(BF16)
                acc_ref[g] += jnp.dot(vtb, p, preferred_element_type=F32)
                l_ref[g] += jnp.dot(ones, p, preferred_element_type=F32)[0:1]
            return carry

        lax.fori_loop(0, n_kb, body, 0)

    @pl.when(jnp.logical_not(fixed_shift_ok))
    def _():
        def body(j, carry):
            bias = jnp.where(key_ref[j] >= thr, 0.0, NEG_BIG)
            bias = jnp.concatenate([bias] * ATTN_GROUP, axis=1)
            for g in range(ATTN_KV_HEADS):
                s, vtb = logits_t(j, g)
                s = s + bias
                m_old = m_ref[g]
                m_new = jnp.maximum(m_old, jnp.max(s, axis=0, keepdims=True))
                alpha = jnp.exp2((m_old - m_new) * c_exp)
                p = jnp.exp2((s - m_new) * c_exp)
                l_ref[g] = alpha * l_ref[g] + jnp.sum(p, axis=0, keepdims=True)
                acc_ref[g] = alpha * acc_ref[g] + jnp.dot(vtb, p.astype(BF16), preferred_element_type=F32)
                m_ref[g] = m_new
            return carry

        lax.fori_loop(0, n_kb, body, 0)

    for g in range(ATTN_KV_HEADS):
        out_t = acc_ref[g] / l_ref[g]
        for hh in range(ATTN_GROUP):
            h = g * ATTN_GROUP + hh
            o_ref[0, :, h * ATTN_HEAD_DIM:(h + 1) * ATTN_HEAD_DIM] = (
                out_t[:, hh * tq:(hh + 1) * tq].T.astype(o_ref.dtype))


def dsa_attention(q, qi, kx, k, v, ki, *, n_q, tq, tk, q_pos0, k_top):
    b = q.shape[0]
    s = k.shape[1]
    assert n_q % tq == 0 and s % tk == 0 and tq == LANES
    n_kb_max = s // tk
    cols = ATTN_GROUP * tq
    vt = v.reshape(b, n_kb_max, tk, ATTN_KV_DIM).transpose(0, 1, 3, 2)
    kern = functools.partial(_attn_kernel, tq=tq, tk=tk, q_pos0=q_pos0, k_top=k_top, n_kb_max=n_kb_max)
    resident = dict(pipeline_mode=pl.Buffered(1))
    return pl.pallas_call(
        kern,
        grid=(b, n_q // tq),
        in_specs=[
            pl.BlockSpec((1, ATTN_HEADS, tq, ATTN_HEAD_DIM), lambda bi, i: (bi, 0, i, 0)),
            pl.BlockSpec((1, IDX_HEADS, tq, IDX_HEAD_DIM), lambda bi, i: (bi, 0, i, 0)),
            pl.BlockSpec((1, tq, LANES), lambda bi, i: (bi, i, 0)),
            pl.BlockSpec((1, s, ATTN_KV_DIM), lambda bi, i: (bi, 0, 0), **resident),
            pl.BlockSpec((1, n_kb_max, ATTN_KV_DIM, tk), lambda bi, i: (bi, 0, 0, 0), **resident),
            pl.BlockSpec((1, s, IDX_HEAD_DIM), lambda bi, i: (bi, 0, 0), **resident),
        ],
        out_specs=pl.BlockSpec((1, tq, ATTN_Q_DIM), lambda bi, i: (bi, i, 0)),
        out_shape=jax.ShapeDtypeStruct((b, n_q, ATTN_Q_DIM), BF16),
        scratch_shapes=[
            pltpu.VMEM((n_kb_max, tk, tq), jnp.int32),
            pltpu.VMEM((ATTN_KV_HEADS, 1, cols), F32),
            pltpu.VMEM((ATTN_KV_HEADS, 1, cols), F32),
            pltpu.VMEM((ATTN_KV_HEADS, ATTN_HEAD_DIM, cols), F32),
            pltpu.VMEM((1, tq), F32),
        ],
        compiler_params=_params("parallel", "arbitrary"),
    )(q, qi, kx, k, vt, ki)


_LEVELS = (32, 16, 8, 4, 2, 1)


def _hgrn_tables():
    c = CHUNK
    t = np.arange(c)[:, None]
    u = np.arange(c)[None, :]
    mats = [(u <= t), (u > t)]
    masks = []
    for w in _LEVELS:
        r = (t // (2 * w)) * (2 * w) + w - 1
        upper = (t % (2 * w)) >= w
        mats.append(upper & (u > r) & (u <= t))
        mats.append((~upper) & (u > t) & (u <= r))
        s = np.arange(c)[None, :]
        masks.append(((t // (2 * w)) == (s // (2 * w))) & upper & ((s % (2 * w)) < w))
    masks.append(t == np.arange(c)[None, :])
    table = np.concatenate(mats, axis=0).astype(np.float32)
    return np.concatenate([table] * 3, axis=1), np.stack(masks).astype(np.float32)


def _hgrn_kernel(q_ref, f_ref, i_ref, gt_ref, lb_ref, gn_ref, tab_ref, msk_ref, s0_ref,
                 o_ref, sfin_ref, st_ref, *, layer):
    c = CHUNK
    dh = HGRN_HEAD_DIM
    step = pl.program_id(1)

    @pl.when(step == 0)
    def _():
        for h in range(HGRN_HEADS):
            st_ref[h] = s0_ref[0, h].T

    lb_all = lb_ref[...]
    e = jnp.exp(lb_all - jnp.max(lb_all, axis=0, keepdims=True))
    p_lb = e / jnp.sum(e, axis=0, keepdims=True)
    lower = jnp.sum(p_lb[:layer + 1], axis=0, keepdims=True) - p_lb[0:1]

    f = lower + (1.0 - lower) * jax.nn.sigmoid(f_ref[0])
    kk = 1.0 - f
    lf = jnp.log(f)
    lf_hi = lf.astype(BF16)
    r1 = lf - lf_hi.astype(F32)
    lf_mid = r1.astype(BF16)
    lf_lo = (r1 - lf_mid.astype(F32)).astype(BF16)
    lf3 = jnp.concatenate([lf_hi, lf_mid, lf_lo], axis=0)
    ex = jnp.exp(jnp.dot(tab_ref[...], lf3, preferred_element_type=F32))

    qq = q_ref[0]
    vv = i_ref[0]
    for h in range(HGRN_HEADS):
        sl = slice(h * dh, (h + 1) * dh)
        qh = qq[:, sl]
        kh = kk[:, sl]
        vh = vv[:, sl]
        vh16 = vh.astype(BF16)
        scores = msk_ref[len(_LEVELS)] * lax.dot_general(
            qh.astype(BF16), kh.astype(BF16), _NT, preferred_element_type=F32)
        for li in range(len(_LEVELS)):
            eq = ex[(2 + 2 * li) * c:(3 + 2 * li) * c, sl]
            ek = ex[(3 + 2 * li) * c:(4 + 2 * li) * c, sl]
            scores = scores + msk_ref[li] * lax.dot_general(
                (qh * eq).astype(BF16), (kh * ek).astype(BF16), _NT, preferred_element_type=F32)
        o_h = jnp.dot(scores.astype(BF16), vh16, preferred_element_type=F32)
        st = st_ref[h]
        o_h = o_h + lax.dot_general((qh * ex[0:c, sl]).astype(BF16), st.astype(BF16), _NT,
                                    preferred_element_type=F32)
        kd = (kh * ex[c:2 * c, sl]).astype(BF16)
        st_ref[h] = st * ex[c - 1:c, sl] + jnp.dot(vh.T.astype(BF16), kd, preferred_element_type=F32)
        o_n = _rms(o_h, gn_ref[:, sl]) * jax.nn.sigmoid(gt_ref[0][:, sl])
        o_ref[0, :, sl] = o_n.astype(o_ref.dtype)

    @pl.when(step == pl.num_programs(1) - 1)
    def _():
        for h in range(HGRN_HEADS):
            sfin_ref[0, h] = st_ref[h].T


def hgrn_scan(p, lb, gnorm, s0, *, layer):
    b, l, _ = p.shape
    d = D_MODEL
    c = CHUNK
    tab, msk = _hgrn_tables()
    kern = functools.partial(_hgrn_kernel, layer=layer)
    col = lambda j: pl.BlockSpec((1, c, d), lambda bi, t, j=j: (bi, t, j))
    state_spec = pl.BlockSpec((1, HGRN_HEADS, HGRN_HEAD_DIM, HGRN_HEAD_DIM), lambda bi, t: (bi, 0, 0, 0))
    return pl.pallas_call(
        kern,
        grid=(b, l // c),
        in_specs=[
            col(0), col(1), col(2), col(3),
            pl.BlockSpec(lb.shape, lambda bi, t: (0, 0)),
            pl.BlockSpec((1, d), lambda bi, t: (0, 0)),
            pl.BlockSpec(tab.shape, lambda bi, t: (0, 0)),
            pl.BlockSpec(msk.shape, lambda bi, t: (0, 0, 0)),
            state_spec,
        ],
        out_specs=[pl.BlockSpec((1, c, d), lambda bi, t: (bi, t, 0)), state_spec],
        out_shape=[jax.ShapeDtypeStruct((b, l, d), BF16), jax.ShapeDtypeStruct(s0.shape, F32)],
        scratch_shapes=[pltpu.VMEM((HGRN_HEADS, HGRN_HEAD_DIM, HGRN_HEAD_DIM), F32)],
        compiler_params=_params("parallel", "arbitrary"),
    )(p, p, p, p, lb, gnorm.reshape(1, d), jnp.asarray(tab, BF16), jnp.asarray(msk), s0)


def _attn_layer(x, g, w_in, w_out, layer, cache_k, cache_v, cache_kidx, n_p, b_s, t_s):
    o1 = ATTN_Q_DIM
    o2 = o1 + 2 * ATTN_KV_DIM
    o3 = o2 + IDX_Q_DIM
    w_kx = jnp.pad(w_in[layer:layer + 1, :, o3:], ((0, 0), (0, 0), (0, LANES - IDX_HEAD_DIM - IDX_HEADS)))
    q = norm_matmul(x, g, w_in, BF16, layer=layer, col0=0, n=o1, head_dim=ATTN_HEAD_DIM)
    kv = norm_matmul(x, g, w_in, F32, layer=layer, col0=o1, n=o2 - o1)
    qi = norm_matmul(x, g, w_in, BF16, layer=layer, col0=o2, n=o3 - o2, head_dim=IDX_HEAD_DIM)
    kx = norm_matmul(x, g, w_kx, F32)

    k_new, v_new, ki_new = kv[:, :ATTN_KV_DIM], kv[:, ATTN_KV_DIM:], kx[:, :IDX_HEAD_DIM]
    past = cache_k.shape[1]
    tq = LANES

    o_p = dsa_attention(
        q[None], qi[None], kx[None],
        k_new[:n_p].astype(BF16)[None], v_new[:n_p].astype(BF16)[None], ki_new[:n_p].astype(BF16)[None],
        n_q=n_p, tq=tq, tk=512, q_pos0=0, k_top=min(TOPK_MAX, n_p // 4))

    tk_s = 256
    s_all = past + t_s
    n_adm_pad = ((past + tq - 1) // CHUNK + 1) * CHUNK
    s_pad = -(-max(s_all, n_adm_pad) // tk_s) * tk_s

    def with_cache(cache, new, width):
        full = jnp.concatenate([cache.reshape(b_s, past, width).astype(BF16),
                                new[n_p:].astype(BF16).reshape(b_s, t_s, width)], axis=1)
        return jnp.pad(full, ((0, 0), (0, s_pad - s_all), (0, 0)))

    def sample_heads(a, heads, dim):
        a = a[:, n_p:].reshape(heads, b_s, t_s, dim).transpose(1, 0, 2, 3)
        return jnp.pad(a, ((0, 0), (0, 0), (0, tq - t_s), (0, 0)))

    kx_s = jnp.pad(kx[n_p:].reshape(b_s, t_s, LANES), ((0, 0), (0, tq - t_s), (0, 0)))
    o_s = dsa_attention(
        sample_heads(q, ATTN_HEADS, ATTN_HEAD_DIM), sample_heads(qi, IDX_HEADS, IDX_HEAD_DIM), kx_s,
        with_cache(cache_k, k_new, ATTN_KV_DIM), with_cache(cache_v, v_new, ATTN_KV_DIM),
        with_cache(cache_kidx, ki_new, IDX_HEAD_DIM),
        n_q=tq, tq=tq, tk=tk_s, q_pos0=past, k_top=min(TOPK_MAX, s_all // 4))

    o = jnp.concatenate([o_p.reshape(n_p, ATTN_Q_DIM), o_s[:, :t_s].reshape(b_s * t_s, ATTN_Q_DIM)], axis=0)
    x = matmul_residual(o, w_out, x, layer=layer)
    return x, k_new, v_new, ki_new


def _rec_layer(x, g, w_in, w_out, gnorm, rec_lb, state, layer, n_p, b_s, t_s):
    p = norm_matmul(x, g, w_in, F32, layer=layer)
    s0_p = jnp.zeros((1,) + state.shape[1:], F32)
    o_p, s_p = hgrn_scan(p[:n_p].reshape(1, n_p, -1), rec_lb, gnorm, s0_p, layer=layer)
    o_s, s_s = hgrn_scan(p[n_p:].reshape(b_s, t_s, -1), rec_lb, gnorm, state, layer=layer)
    o = jnp.concatenate([o_p.reshape(n_p, D_MODEL), o_s.reshape(b_s * t_s, D_MODEL)], axis=0)
    x = matmul_residual(o, w_out, x, layer=layer)
    return x, s_p, s_s


def kernel(x_prompt, x_sample, cache_k, cache_v, cache_kidx, state_s, norm_mix, norm_mlp, norm_final,
           attn_w_in, attn_w_out, rec_w_in, rec_w_out, rec_gnorm, rec_lb, mlp_w_up, mlp_w_down):
    b_p, l_p, d = x_prompt.shape
    b_s, t_s, _ = x_sample.shape
    assert b_p == 1
    n_p = b_p * l_p
    x = jnp.concatenate([x_prompt.reshape(n_p, d), x_sample.reshape(b_s * t_s, d)], axis=0)

    kp, vp, kip, sp = [], [], [], []
    ks, vs, kis, ss = [], [], [], []
    for layer in range(DEPTH):
        j = layer // N_MIXERS
        if layer % N_MIXERS == 0:
            x, k_new, v_new, ki_new = _attn_layer(
                x, norm_mix[layer], attn_w_in, attn_w_out, j,
                cache_k[j], cache_v[j], cache_kidx[j], n_p, b_s, t_s)
            kp.append(k_new[:n_p].reshape(b_p, l_p, ATTN_KV_HEADS, ATTN_HEAD_DIM))
            vp.append(v_new[:n_p].reshape(b_p, l_p, ATTN_KV_HEADS, ATTN_HEAD_DIM))
            kip.append(ki_new[:n_p].reshape(b_p, l_p, IDX_HEAD_DIM))
            ks.append(k_new[n_p:].reshape(b_s, t_s, ATTN_KV_HEADS, ATTN_HEAD_DIM))
            vs.append(v_new[n_p:].reshape(b_s, t_s, ATTN_KV_HEADS, ATTN_HEAD_DIM))
            kis.append(ki_new[n_p:].reshape(b_s, t_s, IDX_HEAD_DIM))
        else:
            x, s_p, s_s = _rec_layer(
                x, norm_mix[layer], rec_w_in, rec_w_out, rec_gnorm[j], rec_lb, state_s[j],
                j, n_p, b_s, t_s)
            sp.append(s_p)
            ss.append(s_s)
        x = mlp_residual(x, norm_mlp[layer], mlp_w_up, mlp_w_down, layer=layer)

    y = final_norm(x, norm_final)
    return (y[:n_p].reshape(b_p, l_p, d), y[n_p:].reshape(b_s, t_s, d),
            jnp.stack(kp), jnp.stack(vp), jnp.stack(kip), jnp.stack(sp),
            jnp.stack(ks), jnp.stack(vs), jnp.stack(kis), jnp.stack(ss))
```

```python
import functools

import numpy as np
import jax
import jax.numpy as jnp
from jax import lax
from jax.experimental import pallas as pl
from jax.experimental.pallas import tpu as pltpu

D_MODEL = 2048
DEPTH = 4
CHUNK = 64
N_MIXERS = 2
ATTN_HEADS = 16
ATTN_KV_HEADS = 4
ATTN_HEAD_DIM = D_MODEL // ATTN_HEADS
ATTN_GROUP = ATTN_HEADS // ATTN_KV_HEADS
IDX_HEADS = 16
IDX_HEAD_DIM = 64
TOPK_MAX = 256
HGRN_HEAD_DIM = 128
HGRN_HEADS = D_MODEL // HGRN_HEAD_DIM
D_FF = 4 * D_MODEL
RMS_EPS = 1e-6
ATTN_Q_DIM = ATTN_HEADS * ATTN_HEAD_DIM
ATTN_KV_DIM = ATTN_KV_HEADS * ATTN_HEAD_DIM
IDX_Q_DIM = IDX_HEADS * IDX_HEAD_DIM

LANES = 128
SUBLANES = 8
VMEM_LIMIT = 56 * 1024 * 1024

F32 = jnp.float32
BF16 = jnp.bfloat16
INT_MIN = np.int32(-(2 ** 31))
NEG_BIG = -1e30
LOG2_E = 1.4426950408889634

_NT = (((1,), (1,)), ((), ()))


def _params(*sem):
    return pltpu.CompilerParams(dimension_semantics=sem, vmem_limit_bytes=VMEM_LIMIT)


def _rms(x, g):
    ms = jnp.mean(x * x, axis=-1, keepdims=True)
    return (x * lax.rsqrt(ms + RMS_EPS)) * g


def _norm_matmul_kernel(x_ref, g_ref, w_ref, o_ref, h_ref):
    @pl.when(pl.program_id(1) == 0)
    def _():
        h_ref[...] = _rms(x_ref[...], g_ref[...]).astype(BF16)

    o_ref[...] = jnp.dot(h_ref[...], w_ref[...].astype(BF16), preferred_element_type=F32)


def _row_tile(m, cap=1088):
    return max(t for t in range(16, cap + 1, 16) if m % t == 0)


_ONE_BUFFER = dict(pipeline_mode=pl.Buffered(1))


def norm_matmul(x, g, w, *, layer, tn=512):
    m, d = x.shape
    n = w.shape[2]
    tm = _row_tile(m)
    assert n % tn == 0
    return pl.pallas_call(
        _norm_matmul_kernel,
        grid=(m // tm, n // tn),
        in_specs=[
            pl.BlockSpec((tm, d), lambda i, j: (i, 0), **_ONE_BUFFER),
            pl.BlockSpec((1, d), lambda i, j: (0, 0)),
            pl.BlockSpec((None, d, tn), lambda i, j: (layer, 0, j)),
        ],
        out_specs=pl.BlockSpec((tm, tn), lambda i, j: (i, j)),
        out_shape=jax.ShapeDtypeStruct((m, n), F32),
        scratch_shapes=[pltpu.VMEM((tm, d), BF16)],
        compiler_params=_params("parallel", "arbitrary"),
    )(x, g.reshape(1, d), w)


def _attn_in_proj_kernel(x_ref, g_ref, w_ref, wkx_ref, q_ref, kv_ref, qi_ref, kx_ref, h_ref, *, n_q, n_kv, n_qi):
    j = pl.program_id(1)

    @pl.when(j == 0)
    def _():
        h_ref[...] = _rms(x_ref[...], g_ref[...]).astype(BF16)

    @pl.when(j < n_q + n_kv + n_qi)
    def _():
        res = jnp.dot(h_ref[...], w_ref[...].astype(BF16), preferred_element_type=F32)

        @pl.when(j < n_q)
        def _():
            for hh in range(q_ref.shape[0]):
                q_ref[hh] = res[:, hh * ATTN_HEAD_DIM:(hh + 1) * ATTN_HEAD_DIM].astype(q_ref.dtype)

        @pl.when(jnp.logical_and(j >= n_q, j < n_q + n_kv))
        def _():
            kv_ref[...] = res

        @pl.when(j >= n_q + n_kv)
        def _():
            for hh in range(qi_ref.shape[0]):
                qi_ref[hh] = res[:, hh * IDX_HEAD_DIM:(hh + 1) * IDX_HEAD_DIM].astype(qi_ref.dtype)

    @pl.when(j == n_q + n_kv + n_qi)
    def _():
        kx_ref[...] = jnp.dot(h_ref[...], wkx_ref[...].astype(BF16), preferred_element_type=F32)


def attn_in_proj(x, g, w_in, *, layer, tn=512):
    m, d = x.shape
    tm = _row_tile(m)
    kv_dim = 2 * ATTN_KV_DIM
    n_q, n_kv, n_qi = ATTN_Q_DIM // tn, kv_dim // tn, IDX_Q_DIM // tn
    n_main = n_q + n_kv + n_qi
    tail = w_in.shape[2] - n_main * tn
    w_kx = jnp.pad(w_in[layer:layer + 1, :, n_main * tn:], ((0, 0), (0, 0), (0, LANES - tail)))
    clip = lambda v, hi: jnp.minimum(jnp.maximum(v, 0), hi)
    return pl.pallas_call(
        functools.partial(_attn_in_proj_kernel, n_q=n_q, n_kv=n_kv, n_qi=n_qi),
        grid=(m // tm, n_main + 1),
        in_specs=[
            pl.BlockSpec((tm, d), lambda i, j: (i, 0), **_ONE_BUFFER),
            pl.BlockSpec((1, d), lambda i, j: (0, 0)),
            pl.BlockSpec((None, d, tn), lambda i, j: (layer, 0, jnp.minimum(j, n_main - 1))),
            pl.BlockSpec((None, d, LANES), lambda i, j: (0, 0, 0)),
        ],
        out_specs=[
            pl.BlockSpec((tn // ATTN_HEAD_DIM, tm, ATTN_HEAD_DIM), lambda i, j: (clip(j, n_q - 1), i, 0)),
            pl.BlockSpec((tm, tn), lambda i, j: (i, clip(j - n_q, n_kv - 1))),
            pl.BlockSpec((tn // IDX_HEAD_DIM, tm, IDX_HEAD_DIM), lambda i, j: (clip(j - n_q - n_kv, n_qi - 1), i, 0)),
            pl.BlockSpec((tm, LANES), lambda i, j: (i, 0)),
        ],
        out_shape=[
            jax.ShapeDtypeStruct((ATTN_HEADS, m, ATTN_HEAD_DIM), BF16),
            jax.ShapeDtypeStruct((m, kv_dim), F32),
            jax.ShapeDtypeStruct((IDX_HEADS, m, IDX_HEAD_DIM), BF16),
            jax.ShapeDtypeStruct((m, LANES), F32),
        ],
        scratch_shapes=[pltpu.VMEM((tm, d), BF16)],
        compiler_params=_params("parallel", "arbitrary"),
    )(x, g.reshape(1, d), w_in, w_kx)


def _matmul_res_kernel(a_ref, w_ref, r_ref, o_ref):
    o_ref[...] = r_ref[...] + jnp.dot(a_ref[...], w_ref[...].astype(BF16), preferred_element_type=F32)


def matmul_residual(a, w, res, *, layer, tn=512):
    m, k = a.shape
    n = w.shape[2]
    tm = _row_tile(m)
    assert n % tn == 0
    return pl.pallas_call(
        _matmul_res_kernel,
        grid=(m // tm, n // tn),
        in_specs=[
            pl.BlockSpec((tm, k), lambda i, j: (i, 0)),
            pl.BlockSpec((None, k, tn), lambda i, j: (layer, 0, j)),
            pl.BlockSpec((tm, tn), lambda i, j: (i, j)),
        ],
        out_specs=pl.BlockSpec((tm, tn), lambda i, j: (i, j)),
        out_shape=jax.ShapeDtypeStruct((m, n), F32),
        compiler_params=_params("parallel", "parallel"),
    )(a, w, res)


def _mlp_kernel(x_ref, g_ref, wu_ref, wd_ref, o_ref, h_ref):
    @pl.when(pl.program_id(1) == 0)
    def _():
        x = x_ref[...]
        h_ref[...] = _rms(x, g_ref[...]).astype(BF16)
        o_ref[...] = x

    u = jnp.maximum(jnp.dot(h_ref[...], wu_ref[...].astype(BF16), preferred_element_type=F32), 0.0)
    o_ref[...] += jnp.dot((u * u).astype(BF16), wd_ref[...].astype(BF16), preferred_element_type=F32)


def mlp_residual(x, g, w_up, w_down, *, layer, tf=512):
    m, d = x.shape
    f = w_up.shape[2]
    tm = _row_tile(m)
    assert f % tf == 0
    return pl.pallas_call(
        _mlp_kernel,
        grid=(m // tm, f // tf),
        in_specs=[
            pl.BlockSpec((tm, d), lambda i, j: (i, 0), **_ONE_BUFFER),
            pl.BlockSpec((1, d), lambda i, j: (0, 0)),
            pl.BlockSpec((None, d, tf), lambda i, j: (layer, 0, j)),
            pl.BlockSpec((None, tf, d), lambda i, j: (layer, j, 0)),
        ],
        out_specs=pl.BlockSpec((tm, d), lambda i, j: (i, 0), **_ONE_BUFFER),
        out_shape=jax.ShapeDtypeStruct((m, d), F32),
        scratch_shapes=[pltpu.VMEM((tm, d), BF16)],
        compiler_params=_params("parallel", "arbitrary"),
    )(x, g.reshape(1, d), w_up, w_down)


def _final_norm_kernel(x_ref, g_ref, o_ref):
    o_ref[...] = _rms(x_ref[...], g_ref[...])


def final_norm(x, g, tm=512):
    m, d = x.shape
    return pl.pallas_call(
        _final_norm_kernel,
        grid=(m // tm,),
        in_specs=[pl.BlockSpec((tm, d), lambda i: (i, 0)), pl.BlockSpec((1, d), lambda i: (0, 0))],
        out_specs=pl.BlockSpec((tm, d), lambda i: (i, 0)),
        out_shape=jax.ShapeDtypeStruct((m, d), F32),
        compiler_params=_params("parallel"),
    )(x, g.reshape(1, d))


def _attn_kernel(q_ref, qi_ref, kx_ref, k_ref, vt_ref, ki_ref, o_ref,
                 key_ref, m_ref, l_ref, acc_ref, kn_ref, p_ref, khi_ref, klo_ref, *, tq, tk, q_pos0, k_top, n_kb_max):
    @pl.when(pl.program_id(1) == 0)
    def _():
        def kn_body(j, best):
            kf = k_ref[0, pl.ds(pl.multiple_of(j * tk, tk), tk), :].astype(F32)
            sq = kf * kf
            for g in range(ATTN_KV_HEADS):
                n2 = jnp.sum(sq[:, g * ATTN_HEAD_DIM:(g + 1) * ATTN_HEAD_DIM], axis=1, keepdims=True)
                best = jnp.maximum(best, jnp.max(n2, axis=0, keepdims=True))
            return best

        kn_ref[...] = jnp.broadcast_to(lax.fori_loop(0, n_kb_max, kn_body, jnp.zeros((1, 1), F32)), kn_ref.shape)

    q_start = q_pos0 + pl.program_id(1) * tq
    n_adm = ((q_start + tq - 1) // CHUNK + 1) * CHUNK
    n_kb = jnp.minimum((n_adm + tk - 1) // tk, n_kb_max)

    k_row = lax.broadcasted_iota(jnp.int32, (tk, tq), 0)
    k_lim = ((q_start + lax.broadcasted_iota(jnp.int32, (tk, tq), 1)) // CHUNK + 1) * CHUNK

    w_t = kx_ref[0].T[IDX_HEAD_DIM:IDX_HEAD_DIM + IDX_HEADS, :] * (IDX_HEADS ** -0.5 * IDX_HEAD_DIM ** -0.5)
    qi = qi_ref[0].reshape(IDX_HEADS * tq, IDX_HEAD_DIM)

    def score_body(j, carry):
        kib = ki_ref[0, pl.ds(pl.multiple_of(j * tk, tk), tk), :]
        d = lax.dot_general(kib, qi, _NT, preferred_element_type=F32)
        sc = jnp.zeros((tk, tq), F32)
        for h in range(IDX_HEADS):
            sc = sc + w_t[h:h + 1, :] * jnp.maximum(d[:, h * tq:(h + 1) * tq], 0.0)
        bits = pltpu.bitcast(sc, jnp.int32)
        key = bits ^ ((bits >> 31) & jnp.int32(0x7FFFFFFF))
        key = jnp.where(k_row < k_lim - j * tk, key, INT_MIN)
        key_ref[j] = key
        khi_ref[j] = (key >> 16).astype(jnp.int16)
        klo_ref[j] = ((key & 0xFFFF) - 0x8000).astype(jnp.int16)
        return carry

    lax.fori_loop(0, n_kb, score_body, 0)

    n_acc = 8
    rows32 = 8 * SUBLANES

    def count16(ref, pred):
        def body(j, cnt):
            hit = jnp.where(pred(ref[j]), jnp.ones((), jnp.int16), jnp.zeros((), jnp.int16))
            pairs = pltpu.bitcast(hit, jnp.int32)
            return cnt + jnp.sum(pairs.reshape(tk // (2 * rows32), rows32, tq), axis=0)
        cnt = lax.fori_loop(0, n_kb, body, jnp.zeros((rows32, tq), jnp.int32))
        both = (cnt & 0xFFFF) + (cnt >> 16)
        return jnp.sum(both.astype(F32), axis=0, keepdims=True)

    def search16(ref, need):
        def bit16_body(p, state):
            t, cnt_t = state
            cand = t + jnp.left_shift(jnp.int32(1), 15 - p)
            cand16 = cand.astype(jnp.int16)
            cnt = count16(ref, lambda blk: blk >= cand16)
            keep = cnt >= need
            return jnp.where(keep, cand, t), jnp.where(keep, cnt, cnt_t)
        t0 = jnp.full((1, tq), -0x8000, jnp.int32)
        return lax.fori_loop(0, 16, bit16_body, (t0, jnp.full((1, tq), 2.0 ** 30, F32)))

    def count_where(pred):
        def body(j, cnt):
            hit = jnp.where(pred(key_ref[j], j), 1.0, 0.0)
            return cnt + jnp.sum(hit.reshape(tk // (n_acc * SUBLANES), n_acc * SUBLANES, tq), axis=0)
        cnt = lax.fori_loop(0, n_kb, body, jnp.zeros((n_acc * SUBLANES, tq), F32))
        return jnp.sum(cnt, axis=0, keepdims=True)

    thr_hi, cnt_hi = search16(khi_ref, float(k_top))
    thr_hi16 = thr_hi.astype(jnp.int16)
    n_above = count16(khi_ref, lambda blk: blk > thr_hi16)

    def narrow_body(j, carry):
        klo_ref[j] = jnp.where(khi_ref[j] == thr_hi16, klo_ref[j], jnp.int16(-0x8000))
        return carry

    lax.fori_loop(0, n_kb, narrow_body, 0)
    thr_lo, cnt_lo = search16(klo_ref, float(k_top) - n_above)
    thr_raw = jnp.left_shift(thr_hi, 16) | ((thr_lo + 0x8000) & 0xFFFF)
    cnt_thr = jnp.where(thr_lo > -0x8000, n_above + cnt_lo, cnt_hi)
    thr = jnp.maximum(thr_raw, INT_MIN + 1)

    tied = jnp.logical_and(cnt_thr > float(k_top), thr_raw > INT_MIN)
    n_tied = jnp.max(jnp.where(tied, 1.0, 0.0))

    @pl.when(n_tied > 0.0)
    def _():
        need = float(k_top) - count_where(lambda blk, j: blk > thr)
        n_bits = max(1, int(n_kb_max * tk - 1).bit_length())

        def pos_body(b, last):
            step = jnp.left_shift(jnp.int32(1), n_bits - 1 - b)
            cand = last + step - 1
            got = count_where(lambda blk, j: jnp.logical_and(blk == thr, k_row + j * tk <= cand))
            return jnp.where(got < need, last + step, last)

        last = lax.fori_loop(0, n_bits, pos_body, jnp.zeros((1, tq), jnp.int32))

        def demote_body(j, carry):
            blk = key_ref[j]
            drop = jnp.logical_and(blk == thr, k_row + j * tk > last)
            key_ref[j] = jnp.where(drop, thr - 1, blk)
            return carry

        lax.fori_loop(0, n_kb, demote_body, 0)

    m_ref[...] = jnp.full(m_ref.shape, NEG_BIG, F32)
    l_ref[...] = jnp.zeros(l_ref.shape, F32)
    acc_ref[...] = jnp.zeros(acc_ref.shape, F32)
    c_exp = ATTN_HEAD_DIM ** -0.5 * LOG2_E
    cols = ATTN_GROUP * tq

    def logits_t(j, g):
        k0 = pl.multiple_of(j * tk, tk)
        kb = k_ref[0, pl.ds(k0, tk), g * ATTN_HEAD_DIM:(g + 1) * ATTN_HEAD_DIM]
        vtb = vt_ref[0, j, g * ATTN_HEAD_DIM:(g + 1) * ATTN_HEAD_DIM, :]
        qg = q_ref[0, g * ATTN_GROUP:(g + 1) * ATTN_GROUP].reshape(cols, ATTN_HEAD_DIM)
        return lax.dot_general(kb, qg, _NT, preferred_element_type=F32), vtb

    qf = q_ref[0].reshape(ATTN_HEADS * tq, ATTN_HEAD_DIM).astype(F32)
    qn2 = lax.dot_general(jnp.ones((SUBLANES, ATTN_HEAD_DIM), BF16), (qf * qf).astype(BF16), _NT,
                          preferred_element_type=F32)
    qn2_max = qn2[0:1, 0:tq]
    for h in range(1, ATTN_HEADS):
        qn2_max = jnp.maximum(qn2_max, qn2[0:1, h * tq:(h + 1) * tq])
    m_bound = jnp.sqrt(qn2_max * kn_ref[...]) * 1.02
    fixed_shift_ok = jnp.max(m_bound) * (2.0 * c_exp) < 100.0

    @pl.when(fixed_shift_ok)
    def _():
        shift = -c_exp * m_bound
        ones = jnp.ones((2 * SUBLANES, tk), BF16)

        def accumulate(j):
            for g in range(ATTN_KV_HEADS):
                vtb = vt_ref[0, j, g * ATTN_HEAD_DIM:(g + 1) * ATTN_HEAD_DIM, :]
                acc_ref[g] += jnp.dot(vtb, p_ref[g], preferred_element_type=F32)
                l_ref[g] += jnp.dot(ones, p_ref[g], preferred_element_type=F32)[0:1]

        def probabilities(j):
            bias = jnp.where(key_ref[j] >= thr, shift, NEG_BIG)
            bias = jnp.concatenate([bias] * ATTN_GROUP, axis=1)
            for g in range(ATTN_KV_HEADS):
                s, _ = logits_t(j, g)
                p_ref[g] = jnp.exp2(s * c_exp + bias).astype(BF16)

        probabilities(0)

        def body(j, carry):
            accumulate(j - 1)
            probabilities(j)
            return carry

        lax.fori_loop(1, n_kb, body, 0)
        accumulate(n_kb - 1)

    @pl.when(jnp.logical_not(fixed_shift_ok))
    def _():
        def body(j, carry):
            bias = jnp.where(key_ref[j] >= thr, 0.0, NEG_BIG)
            bias = jnp.concatenate([bias] * ATTN_GROUP, axis=1)
            for g in range(ATTN_KV_HEADS):
                s, vtb = logits_t(j, g)
                s = s + bias
                m_old = m_ref[g]
                m_new = jnp.maximum(m_old, jnp.max(s, axis=0, keepdims=True))
                alpha = jnp.exp2((m_old - m_new) * c_exp)
                p = jnp.exp2((s - m_new) * c_exp)
                l_ref[g] = alpha * l_ref[g] + jnp.sum(p, axis=0, keepdims=True)
                acc_ref[g] = alpha * acc_ref[g] + jnp.dot(vtb, p.astype(BF16), preferred_element_type=F32)
                m_ref[g] = m_new
            return carry

        lax.fori_loop(0, n_kb, body, 0)

    for g in range(ATTN_KV_HEADS):
        out_t = acc_ref[g] / l_ref[g]
        for hh in range(ATTN_GROUP):
            h = g * ATTN_GROUP + hh
            o_ref[0, :, h * ATTN_HEAD_DIM:(h + 1) * ATTN_HEAD_DIM] = (
                out_t[:, hh * tq:(hh + 1) * tq].T.astype(o_ref.dtype))


def dsa_attention(q, qi, kx, k, v, ki, *, n_q, tq, tk, q_pos0, k_top):
    b = q.shape[0]
    s = k.shape[1]
    assert n_q % tq == 0 and s % tk == 0 and tq == LANES
    n_kb_max = s // tk
    cols = ATTN_GROUP * tq
    vt = v.reshape(b, n_kb_max, tk, ATTN_KV_DIM).transpose(0, 1, 3, 2)
    kern = functools.partial(_attn_kernel, tq=tq, tk=tk, q_pos0=q_pos0, k_top=k_top, n_kb_max=n_kb_max)
    resident = dict(pipeline_mode=pl.Buffered(1))
    return pl.pallas_call(
        kern,
        grid=(b, n_q // tq),
        in_specs=[
            pl.BlockSpec((1, ATTN_HEADS, tq, ATTN_HEAD_DIM), lambda bi, i: (bi, 0, i, 0)),
            pl.BlockSpec((1, IDX_HEADS, tq, IDX_HEAD_DIM), lambda bi, i: (bi, 0, i, 0)),
            pl.BlockSpec((1, tq, LANES), lambda bi, i: (bi, i, 0)),
            pl.BlockSpec((1, s, ATTN_KV_DIM), lambda bi, i: (bi, 0, 0), **resident),
            pl.BlockSpec((1, n_kb_max, ATTN_KV_DIM, tk), lambda bi, i: (bi, 0, 0, 0), **resident),
            pl.BlockSpec((1, s, IDX_HEAD_DIM), lambda bi, i: (bi, 0, 0), **resident),
        ],
        out_specs=pl.BlockSpec((1, tq, ATTN_Q_DIM), lambda bi, i: (bi, i, 0)),
        out_shape=jax.ShapeDtypeStruct((b, n_q, ATTN_Q_DIM), BF16),
        scratch_shapes=[
            pltpu.VMEM((n_kb_max, tk, tq), jnp.int32),
            pltpu.VMEM((ATTN_KV_HEADS, 1, cols), F32),
            pltpu.VMEM((ATTN_KV_HEADS, 1, cols), F32),
            pltpu.VMEM((ATTN_KV_HEADS, ATTN_HEAD_DIM, cols), F32),
            pltpu.VMEM((1, tq), F32),
            pltpu.VMEM((ATTN_KV_HEADS, tk, cols), BF16),
            pltpu.VMEM((n_kb_max, tk, tq), jnp.int16),
            pltpu.VMEM((n_kb_max, tk, tq), jnp.int16),
        ],
        compiler_params=_params("parallel", "arbitrary"),
    )(q, qi, kx, k, vt, ki)


_LEVELS = (32, 16, 8, 4, 2, 1)


def _hgrn_tables():
    c = CHUNK
    t = np.arange(c)[:, None]
    u = np.arange(c)[None, :]
    mats = [(u <= t), (u > t)]
    masks = []
    for w in _LEVELS:
        r = (t // (2 * w)) * (2 * w) + w - 1
        upper = (t % (2 * w)) >= w
        mats.append((upper & (u > r) & (u <= t)) | ((~upper) & (u > t) & (u <= r)))
        s = np.arange(c)[None, :]
        masks.append(((t // (2 * w)) == (s // (2 * w))) & upper & ((s % (2 * w)) < w))
    masks.append(t == np.arange(c)[None, :])
    table = np.concatenate(mats, axis=0).astype(np.float32)
    return np.concatenate([table] * 3, axis=1), np.stack(masks).astype(np.float32)


def _hgrn_kernel(q_ref, f_ref, i_ref, gt_ref, lb_ref, gn_ref, tab_ref, msk_ref, s0_ref,
                 o_ref, sfin_ref, st_ref, *, layer):
    c = CHUNK
    dh = HGRN_HEAD_DIM
    step = pl.program_id(1)

    @pl.when(step == 0)
    def _():
        for h in range(HGRN_HEADS):
            st_ref[h] = s0_ref[0, h].T

    lb_all = lb_ref[...]
    e = jnp.exp(lb_all - jnp.max(lb_all, axis=0, keepdims=True))
    p_lb = e / jnp.sum(e, axis=0, keepdims=True)
    lower = jnp.sum(p_lb[:layer + 1], axis=0, keepdims=True) - p_lb[0:1]

    f = lower + (1.0 - lower) * jax.nn.sigmoid(f_ref[0])
    kk = 1.0 - f
    lf = jnp.log(f)
    lf_hi = lf.astype(BF16)
    r1 = lf - lf_hi.astype(F32)
    lf_mid = r1.astype(BF16)
    lf_lo = (r1 - lf_mid.astype(F32)).astype(BF16)
    lf3 = jnp.concatenate([lf_hi, lf_mid, lf_lo], axis=0)
    ex = jnp.exp(jnp.dot(tab_ref[...], lf3, preferred_element_type=F32))

    qq = q_ref[0]
    vv = i_ref[0]
    for h in range(HGRN_HEADS):
        sl = slice(h * dh, (h + 1) * dh)
        qh = qq[:, sl]
        kh = kk[:, sl]
        vh = vv[:, sl]
        vh16 = vh.astype(BF16)
        scores = msk_ref[len(_LEVELS)] * lax.dot_general(
            qh.astype(BF16), kh.astype(BF16), _NT, preferred_element_type=F32)
        for li in range(len(_LEVELS)):
            el = ex[(2 + li) * c:(3 + li) * c, sl]
            scores = scores + msk_ref[li] * lax.dot_general(
                (qh * el).astype(BF16), (kh * el).astype(BF16), _NT, preferred_element_type=F32)
        o_h = jnp.dot(scores.astype(BF16), vh16, preferred_element_type=F32)
        st = st_ref[h]
        o_h = o_h + lax.dot_general((qh * ex[0:c, sl]).astype(BF16), st.astype(BF16), _NT,
                                    preferred_element_type=F32)
        kd = (kh * ex[c:2 * c, sl]).astype(BF16)
        st_ref[h] = st * ex[c - 1:c, sl] + jnp.dot(vh.T.astype(BF16), kd, preferred_element_type=F32)
        o_n = _rms(o_h, gn_ref[:, sl]) * jax.nn.sigmoid(gt_ref[0][:, sl])
        o_ref[0, :, sl] = o_n.astype(o_ref.dtype)

    @pl.when(step == pl.num_programs(1) - 1)
    def _():
        for h in range(HGRN_HEADS):
            sfin_ref[0, h] = st_ref[h].T


def hgrn_scan(p, lb, gnorm, s0, *, layer, n_rows=None):
    b = p.shape[0]
    l = p.shape[1] if n_rows is None else n_rows
    d = D_MODEL
    c = CHUNK
    tab, msk = _hgrn_tables()
    kern = functools.partial(_hgrn_kernel, layer=layer)
    col = lambda j: pl.BlockSpec((1, c, d), lambda bi, t, j=j: (bi, t, j))
    state_spec = pl.BlockSpec((1, HGRN_HEADS, HGRN_HEAD_DIM, HGRN_HEAD_DIM), lambda bi, t: (bi, 0, 0, 0))
    return pl.pallas_call(
        kern,
        grid=(b, l // c),
        in_specs=[
            col(0), col(1), col(2), col(3),
            pl.BlockSpec(lb.shape, lambda bi, t: (0, 0)),
            pl.BlockSpec((1, d), lambda bi, t: (0, 0)),
            pl.BlockSpec(tab.shape, lambda bi, t: (0, 0)),
            pl.BlockSpec(msk.shape, lambda bi, t: (0, 0, 0)),
            state_spec,
        ],
        out_specs=[pl.BlockSpec((1, c, d), lambda bi, t: (bi, t, 0)), state_spec],
        out_shape=[jax.ShapeDtypeStruct((b, l, d), BF16), jax.ShapeDtypeStruct(s0.shape, F32)],
        scratch_shapes=[pltpu.VMEM((HGRN_HEADS, HGRN_HEAD_DIM, HGRN_HEAD_DIM), F32)],
        compiler_params=_params("parallel", "arbitrary"),
    )(p, p, p, p, lb, gnorm.reshape(1, d), jnp.asarray(tab, BF16), jnp.asarray(msk), s0)


def _attn_layer(x, g, w_in, w_out, layer, cache_k, cache_v, cache_kidx, n_p, b_s, t_s):
    q, kv, qi, kx = attn_in_proj(x, g, w_in, layer=layer)

    k_new, v_new, ki_new = kv[:, :ATTN_KV_DIM], kv[:, ATTN_KV_DIM:], kx[:, :IDX_HEAD_DIM]
    past = cache_k.shape[1]
    tq = LANES

    o_p = dsa_attention(
        q[None], qi[None], kx[None],
        k_new[:n_p].astype(BF16)[None], v_new[:n_p].astype(BF16)[None], ki_new[:n_p].astype(BF16)[None],
        n_q=n_p, tq=tq, tk=512, q_pos0=0, k_top=min(TOPK_MAX, n_p // 4))

    tk_s = 256
    s_all = past + t_s
    n_adm_pad = ((past + tq - 1) // CHUNK + 1) * CHUNK
    s_pad = -(-max(s_all, n_adm_pad) // tk_s) * tk_s

    def with_cache(cache, new, width):
        full = jnp.concatenate([cache.reshape(b_s, past, width).astype(BF16),
                                new[n_p:].astype(BF16).reshape(b_s, t_s, width)], axis=1)
        return jnp.pad(full, ((0, 0), (0, s_pad - s_all), (0, 0)))

    def sample_heads(a, heads, dim):
        a = a[:, n_p:].reshape(heads, b_s, t_s, dim).transpose(1, 0, 2, 3)
        return jnp.pad(a, ((0, 0), (0, 0), (0, tq - t_s), (0, 0)))

    kx_s = jnp.pad(kx[n_p:].reshape(b_s, t_s, LANES), ((0, 0), (0, tq - t_s), (0, 0)))
    o_s = dsa_attention(
        sample_heads(q, ATTN_HEADS, ATTN_HEAD_DIM), sample_heads(qi, IDX_HEADS, IDX_HEAD_DIM), kx_s,
        with_cache(cache_k, k_new, ATTN_KV_DIM), with_cache(cache_v, v_new, ATTN_KV_DIM),
        with_cache(cache_kidx, ki_new, IDX_HEAD_DIM),
        n_q=tq, tq=tq, tk=tk_s, q_pos0=past, k_top=min(TOPK_MAX, s_all // 4))

    o = jnp.concatenate([o_p.reshape(n_p, ATTN_Q_DIM), o_s[:, :t_s].reshape(b_s * t_s, ATTN_Q_DIM)], axis=0)
    x = matmul_residual(o, w_out, x, layer=layer)
    return x, k_new, v_new, ki_new


def _rec_layer(x, g, w_in, w_out, gnorm, rec_lb, state, layer, n_p, b_s, t_s):
    p = norm_matmul(x, g, w_in, layer=layer)
    s0_p = jnp.zeros((1,) + state.shape[1:], F32)
    o_p, s_p = hgrn_scan(p[None], rec_lb, gnorm, s0_p, layer=layer, n_rows=n_p)
    o_s, s_s = hgrn_scan(p[n_p:].reshape(b_s, t_s, -1), rec_lb, gnorm, state, layer=layer)
    o = jnp.concatenate([o_p.reshape(n_p, D_MODEL), o_s.reshape(b_s * t_s, D_MODEL)], axis=0)
    x = matmul_residual(o, w_out, x, layer=layer)
    return x, s_p, s_s


def kernel(x_prompt, x_sample, cache_k, cache_v, cache_kidx, state_s, norm_mix, norm_mlp, norm_final,
           attn_w_in, attn_w_out, rec_w_in, rec_w_out, rec_gnorm, rec_lb, mlp_w_up, mlp_w_down):
    b_p, l_p, d = x_prompt.shape
    b_s, t_s, _ = x_sample.shape
    assert b_p == 1
    n_p = b_p * l_p
    x = jnp.concatenate([x_prompt.reshape(n_p, d), x_sample.reshape(b_s * t_s, d)], axis=0)

    kp, vp, kip, sp = [], [], [], []
    ks, vs, kis, ss = [], [], [], []
    for layer in range(DEPTH):
        j = layer // N_MIXERS
        if layer % N_MIXERS == 0:
            x, k_new, v_new, ki_new = _attn_layer(
                x, norm_mix[layer], attn_w_in, attn_w_out, j,
                cache_k[j], cache_v[j], cache_kidx[j], n_p, b_s, t_s)
            kp.append(k_new[:n_p].reshape(b_p, l_p, ATTN_KV_HEADS, ATTN_HEAD_DIM))
            vp.append(v_new[:n_p].reshape(b_p, l_p, ATTN_KV_HEADS, ATTN_HEAD_DIM))
            kip.append(ki_new[:n_p].reshape(b_p, l_p, IDX_HEAD_DIM))
            ks.append(k_new[n_p:].reshape(b_s, t_s, ATTN_KV_HEADS, ATTN_HEAD_DIM))
            vs.append(v_new[n_p:].reshape(b_s, t_s, ATTN_KV_HEADS, ATTN_HEAD_DIM))
            kis.append(ki_new[n_p:].reshape(b_s, t_s, IDX_HEAD_DIM))
        else:
            x, s_p, s_s = _rec_layer(
                x, norm_mix[layer], rec_w_in, rec_w_out, rec_gnorm[j], rec_lb, state_s[j],
                j, n_p, b_s, t_s)
            sp.append(s_p)
            ss.append(s_s)
        x = mlp_residual(x, norm_mlp[layer], mlp_w_up, mlp_w_down, layer=layer)

    y = final_norm(x, norm_final)
    return (y[:n_p].reshape(b_p, l_p, d), y[n_p:].reshape(b_s, t_s, d),
            jnp.stack(kp), jnp.stack(vp), jnp.stack(kip), jnp.stack(sp),
            jnp.stack(ks), jnp.stack(vs), jnp.stack(kis), jnp.stack(ss))
```

```python
import functools

import numpy as np
import jax
import jax.numpy as jnp
from jax import lax
from jax.experimental import pallas as pl
from jax.experimental.pallas import tpu as pltpu

D_MODEL = 2048
DEPTH = 4
CHUNK = 64
N_MIXERS = 2
ATTN_HEADS = 16
ATTN_KV_HEADS = 4
ATTN_HEAD_DIM = D_MODEL // ATTN_HEADS
ATTN_GROUP = ATTN_HEADS // ATTN_KV_HEADS
IDX_HEADS = 16
IDX_HEAD_DIM = 64
TOPK_MAX = 256
HGRN_HEAD_DIM = 128
HGRN_HEADS = D_MODEL // HGRN_HEAD_DIM
D_FF = 4 * D_MODEL
RMS_EPS = 1e-6
ATTN_Q_DIM = ATTN_HEADS * ATTN_HEAD_DIM
ATTN_KV_DIM = ATTN_KV_HEADS * ATTN_HEAD_DIM
IDX_Q_DIM = IDX_HEADS * IDX_HEAD_DIM

LANES = 128
SUBLANES = 8
VMEM_LIMIT = 56 * 1024 * 1024

F32 = jnp.float32
BF16 = jnp.bfloat16
INT_MIN = np.int32(-(2 ** 31))
NEG_BIG = -1e30
LOG2_E = 1.4426950408889634

_NT = (((1,), (1,)), ((), ()))


def _params(*sem):
    return pltpu.CompilerParams(dimension_semantics=sem, vmem_limit_bytes=VMEM_LIMIT)


def _rms(x, g):
    ms = jnp.mean(x * x, axis=-1, keepdims=True)
    return (x * lax.rsqrt(ms + RMS_EPS)) * g


def _norm_matmul_kernel(x_ref, g_ref, w_ref, o_ref, h_ref):
    @pl.when(pl.program_id(1) == 0)
    def _():
        h_ref[...] = _rms(x_ref[...], g_ref[...]).astype(BF16)

    o_ref[...] = jnp.dot(h_ref[...], w_ref[...].astype(BF16), preferred_element_type=F32)


def _row_tile(m, cap=1088):
    return max(t for t in range(16, cap + 1, 16) if m % t == 0)


_ONE_BUFFER = dict(pipeline_mode=pl.Buffered(1))


def norm_matmul(x, g, w, *, layer, tn=512):
    m, d = x.shape
    n = w.shape[2]
    tm = _row_tile(m)
    assert n % tn == 0
    return pl.pallas_call(
        _norm_matmul_kernel,
        grid=(m // tm, n // tn),
        in_specs=[
            pl.BlockSpec((tm, d), lambda i, j: (i, 0), **_ONE_BUFFER),
            pl.BlockSpec((1, d), lambda i, j: (0, 0)),
            pl.BlockSpec((None, d, tn), lambda i, j: (layer, 0, j)),
        ],
        out_specs=pl.BlockSpec((tm, tn), lambda i, j: (i, j)),
        out_shape=jax.ShapeDtypeStruct((m, n), F32),
        scratch_shapes=[pltpu.VMEM((tm, d), BF16)],
        compiler_params=_params("parallel", "arbitrary"),
    )(x, g.reshape(1, d), w)


def _attn_in_proj_kernel(x_ref, g_ref, w_ref, wkx_ref, q_ref, kv_ref, qi_ref, kx_ref, h_ref, *, n_q, n_kv, n_qi):
    j = pl.program_id(1)

    @pl.when(j == 0)
    def _():
        h_ref[...] = _rms(x_ref[...], g_ref[...]).astype(BF16)

    @pl.when(j < n_q + n_kv + n_qi)
    def _():
        res = jnp.dot(h_ref[...], w_ref[...].astype(BF16), preferred_element_type=F32)

        @pl.when(j < n_q)
        def _():
            for hh in range(q_ref.shape[0]):
                q_ref[hh] = res[:, hh * ATTN_HEAD_DIM:(hh + 1) * ATTN_HEAD_DIM].astype(q_ref.dtype)

        @pl.when(jnp.logical_and(j >= n_q, j < n_q + n_kv))
        def _():
            kv_ref[...] = res

        @pl.when(j >= n_q + n_kv)
        def _():
            for hh in range(qi_ref.shape[0]):
                qi_ref[hh] = res[:, hh * IDX_HEAD_DIM:(hh + 1) * IDX_HEAD_DIM].astype(qi_ref.dtype)

    @pl.when(j == n_q + n_kv + n_qi)
    def _():
        kx_ref[...] = jnp.dot(h_ref[...], wkx_ref[...].astype(BF16), preferred_element_type=F32)


def attn_in_proj(x, g, w_in, *, layer, tn=512):
    m, d = x.shape
    tm = _row_tile(m)
    kv_dim = 2 * ATTN_KV_DIM
    n_q, n_kv, n_qi = ATTN_Q_DIM // tn, kv_dim // tn, IDX_Q_DIM // tn
    n_main = n_q + n_kv + n_qi
    tail = w_in.shape[2] - n_main * tn
    w_kx = jnp.pad(w_in[layer:layer + 1, :, n_main * tn:], ((0, 0), (0, 0), (0, LANES - tail)))
    clip = lambda v, hi: jnp.minimum(jnp.maximum(v, 0), hi)
    return pl.pallas_call(
        functools.partial(_attn_in_proj_kernel, n_q=n_q, n_kv=n_kv, n_qi=n_qi),
        grid=(m // tm, n_main + 1),
        in_specs=[
            pl.BlockSpec((tm, d), lambda i, j: (i, 0), **_ONE_BUFFER),
            pl.BlockSpec((1, d), lambda i, j: (0, 0)),
            pl.BlockSpec((None, d, tn), lambda i, j: (layer, 0, jnp.minimum(j, n_main - 1))),
            pl.BlockSpec((None, d, LANES), lambda i, j: (0, 0, 0)),
        ],
        out_specs=[
            pl.BlockSpec((tn // ATTN_HEAD_DIM, tm, ATTN_HEAD_DIM), lambda i, j: (clip(j, n_q - 1), i, 0)),
            pl.BlockSpec((tm, tn), lambda i, j: (i, clip(j - n_q, n_kv - 1))),
            pl.BlockSpec((tn // IDX_HEAD_DIM, tm, IDX_HEAD_DIM), lambda i, j: (clip(j - n_q - n_kv, n_qi - 1), i, 0)),
            pl.BlockSpec((tm, LANES), lambda i, j: (i, 0)),
        ],
        out_shape=[
            jax.ShapeDtypeStruct((ATTN_HEADS, m, ATTN_HEAD_DIM), BF16),
            jax.ShapeDtypeStruct((m, kv_dim), F32),
            jax.ShapeDtypeStruct((IDX_HEADS, m, IDX_HEAD_DIM), BF16),
            jax.ShapeDtypeStruct((m, LANES), F32),
        ],
        scratch_shapes=[pltpu.VMEM((tm, d), BF16)],
        compiler_params=_params("parallel", "arbitrary"),
    )(x, g.reshape(1, d), w_in, w_kx)


def _matmul_res_kernel(a_ref, w_ref, r_ref, o_ref):
    o_ref[...] = r_ref[...] + jnp.dot(a_ref[...], w_ref[...].astype(BF16), preferred_element_type=F32)


def matmul_residual(a, w, res, *, layer, tn=512):
    m, k = a.shape
    n = w.shape[2]
    tm = _row_tile(m)
    assert n % tn == 0
    return pl.pallas_call(
        _matmul_res_kernel,
        grid=(m // tm, n // tn),
        in_specs=[
            pl.BlockSpec((tm, k), lambda i, j: (i, 0)),
            pl.BlockSpec((None, k, tn), lambda i, j: (layer, 0, j)),
            pl.BlockSpec((tm, tn), lambda i, j: (i, j)),
        ],
        out_specs=pl.BlockSpec((tm, tn), lambda i, j: (i, j)),
        out_shape=jax.ShapeDtypeStruct((m, n), F32),
        compiler_params=_params("parallel", "parallel"),
    )(a, w, res)


def _mlp_kernel(x_ref, g_ref, wu_ref, wd_ref, gf_ref, o_ref, h_ref, *, final_norm):
    @pl.when(pl.program_id(1) == 0)
    def _():
        x = x_ref[...]
        h_ref[...] = _rms(x, g_ref[...]).astype(BF16)
        o_ref[...] = x

    u = jnp.maximum(jnp.dot(h_ref[...], wu_ref[...].astype(BF16), preferred_element_type=F32), 0.0)
    o_ref[...] += jnp.dot((u * u).astype(BF16), wd_ref[...].astype(BF16), preferred_element_type=F32)

    if final_norm:
        @pl.when(pl.program_id(1) == pl.num_programs(1) - 1)
        def _():
            o_ref[...] = _rms(o_ref[...], gf_ref[...])


def mlp_residual(x, g, w_up, w_down, *, layer, g_final=None, tf=512):
    m, d = x.shape
    f = w_up.shape[2]
    tm = _row_tile(m)
    assert f % tf == 0
    final_norm = g_final is not None
    gf = (g_final if final_norm else g).reshape(1, d)
    return pl.pallas_call(
        functools.partial(_mlp_kernel, final_norm=final_norm),
        grid=(m // tm, f // tf),
        in_specs=[
            pl.BlockSpec((tm, d), lambda i, j: (i, 0), **_ONE_BUFFER),
            pl.BlockSpec((1, d), lambda i, j: (0, 0)),
            pl.BlockSpec((None, d, tf), lambda i, j: (layer, 0, j)),
            pl.BlockSpec((None, tf, d), lambda i, j: (layer, j, 0)),
            pl.BlockSpec((1, d), lambda i, j: (0, 0)),
        ],
        out_specs=pl.BlockSpec((tm, d), lambda i, j: (i, 0), **_ONE_BUFFER),
        out_shape=jax.ShapeDtypeStruct((m, d), F32),
        scratch_shapes=[pltpu.VMEM((tm, d), BF16)],
        compiler_params=_params("parallel", "arbitrary"),
    )(x, g.reshape(1, d), w_up, w_down, gf)


def _attn_kernel(q_ref, qi_ref, kx_ref, k_ref, vt_ref, ki_ref, *rest, tq, tk, q_pos0, k_top, n_kb_max):
    o_ref, key_ref, m_ref, l_ref, acc_ref, kn_ref, p_ref = rest[-7:]
    @pl.when(pl.program_id(1) == 0)
    def _():
        def kn_body(j, best):
            kf = k_ref[0, pl.ds(pl.multiple_of(j * tk, tk), tk), :].astype(F32)
            sq = kf * kf
            for g in range(ATTN_KV_HEADS):
                n2 = jnp.sum(sq[:, g * ATTN_HEAD_DIM:(g + 1) * ATTN_HEAD_DIM], axis=1, keepdims=True)
                best = jnp.maximum(best, jnp.max(n2, axis=0, keepdims=True))
            return best

        kn_ref[...] = jnp.broadcast_to(lax.fori_loop(0, n_kb_max, kn_body, jnp.zeros((1, 1), F32)), kn_ref.shape)

    q_start = q_pos0 + pl.program_id(1) * tq
    n_adm = ((q_start + tq - 1) // CHUNK + 1) * CHUNK
    n_kb = jnp.minimum((n_adm + tk - 1) // tk, n_kb_max)

    k_row = lax.broadcasted_iota(jnp.int32, (tk, tq), 0)
    k_lim = ((q_start + lax.broadcasted_iota(jnp.int32, (tk, tq), 1)) // CHUNK + 1) * CHUNK

    w_t = kx_ref[0].T[IDX_HEAD_DIM:IDX_HEAD_DIM + IDX_HEADS, :] * (IDX_HEADS ** -0.5 * IDX_HEAD_DIM ** -0.5)
    qi = qi_ref[0].reshape(IDX_HEADS * tq, IDX_HEAD_DIM)

    def score_body(j, carry):
        kib = ki_ref[0, pl.ds(pl.multiple_of(j * tk, tk), tk), :]
        d = lax.dot_general(kib, qi, _NT, preferred_element_type=F32)
        sc = jnp.zeros((tk, tq), F32)
        for h in range(IDX_HEADS):
            sc = sc + w_t[h:h + 1, :] * jnp.maximum(d[:, h * tq:(h + 1) * tq], 0.0)
        bits = pltpu.bitcast(sc, jnp.int32)
        key = bits ^ ((bits >> 31) & jnp.int32(0x7FFFFFFF))
        key_ref[j] = jnp.where(k_row < k_lim - j * tk, key, INT_MIN)
        return carry

    lax.fori_loop(0, n_kb, score_body, 0)

    rows_acc = 8 * SUBLANES

    def count_where(pred):
        def body(j, cnt):
            blk = key_ref[j]
            hit = pred(blk, j)
            for r in range(tk // rows_acc):
                cnt = jnp.where(hit[r * rows_acc:(r + 1) * rows_acc], cnt + 1.0, cnt)
            return cnt
        cnt = lax.fori_loop(0, n_kb, body, jnp.zeros((rows_acc, tq), F32))
        return jnp.sum(cnt, axis=0, keepdims=True)

    def bit_body(p, state):
        thr, cnt_thr = state
        cand = thr + jnp.left_shift(jnp.int32(1), 31 - p)
        cnt = count_where(lambda blk, j: blk >= cand)
        keep = cnt >= float(k_top)
        return jnp.where(keep, cand, thr), jnp.where(keep, cnt, cnt_thr)

    thr0 = jnp.full((1, tq), INT_MIN, jnp.int32)
    thr_raw, cnt_thr = lax.fori_loop(0, 32, bit_body, (thr0, jnp.full((1, tq), 2.0 ** 30, F32)))
    thr = jnp.maximum(thr_raw, INT_MIN + 1)

    tied = jnp.logical_and(cnt_thr > float(k_top), thr_raw > INT_MIN)
    n_tied = jnp.max(jnp.where(tied, 1.0, 0.0))

    @pl.when(n_tied > 0.0)
    def _():
        need = float(k_top) - count_where(lambda blk, j: blk > thr)
        n_bits = max(1, int(n_kb_max * tk - 1).bit_length())

        def pos_body(b, last):
            step = jnp.left_shift(jnp.int32(1), n_bits - 1 - b)
            cand = last + step - 1
            got = count_where(lambda blk, j: jnp.logical_and(blk == thr, k_row + j * tk <= cand))
            return jnp.where(got < need, last + step, last)

        last = lax.fori_loop(0, n_bits, pos_body, jnp.zeros((1, tq), jnp.int32))

        def demote_body(j, carry):
            blk = key_ref[j]
            drop = jnp.logical_and(blk == thr, k_row + j * tk > last)
            key_ref[j] = jnp.where(drop, thr - 1, blk)
            return carry

        lax.fori_loop(0, n_kb, demote_body, 0)

    m_ref[...] = jnp.full(m_ref.shape, NEG_BIG, F32)
    l_ref[...] = jnp.zeros(l_ref.shape, F32)
    acc_ref[...] = jnp.zeros(acc_ref.shape, F32)
    c_exp = ATTN_HEAD_DIM ** -0.5 * LOG2_E
    cols = ATTN_GROUP * tq

    def logits_t(j, g):
        k0 = pl.multiple_of(j * tk, tk)
        kb = k_ref[0, pl.ds(k0, tk), g * ATTN_HEAD_DIM:(g + 1) * ATTN_HEAD_DIM]
        vtb = vt_ref[0, j, g * ATTN_HEAD_DIM:(g + 1) * ATTN_HEAD_DIM, :]
        qg = q_ref[0, g * ATTN_GROUP:(g + 1) * ATTN_GROUP].reshape(cols, ATTN_HEAD_DIM)
        return lax.dot_general(kb, qg, _NT, preferred_element_type=F32), vtb

    qf = q_ref[0].reshape(ATTN_HEADS * tq, ATTN_HEAD_DIM).astype(F32)
    qn2 = lax.dot_general(jnp.ones((SUBLANES, ATTN_HEAD_DIM), BF16), (qf * qf).astype(BF16), _NT,
                          preferred_element_type=F32)
    qn2_max = qn2[0:1, 0:tq]
    for h in range(1, ATTN_HEADS):
        qn2_max = jnp.maximum(qn2_max, qn2[0:1, h * tq:(h + 1) * tq])
    m_bound = jnp.sqrt(qn2_max * kn_ref[...]) * 1.02
    fixed_shift_ok = jnp.max(m_bound) * (2.0 * c_exp) < 100.0

    @pl.when(fixed_shift_ok)
    def _():
        shift = -c_exp * m_bound
        ones = jnp.ones((2 * SUBLANES, tk), BF16)

        def accumulate(j):
            for g in range(ATTN_KV_HEADS):
                vtb = vt_ref[0, j, g * ATTN_HEAD_DIM:(g + 1) * ATTN_HEAD_DIM, :]
                acc_ref[g] += jnp.dot(vtb, p_ref[g], preferred_element_type=F32)
                l_ref[g] += jnp.dot(ones, p_ref[g], preferred_element_type=F32)[0:1]

        def probabilities(j):
            bias = jnp.where(key_ref[j] >= thr, shift, NEG_BIG)
            bias = jnp.concatenate([bias] * ATTN_GROUP, axis=1)
            for g in range(ATTN_KV_HEADS):
                s, _ = logits_t(j, g)
                p_ref[g] = jnp.exp2(s * c_exp + bias).astype(BF16)

        probabilities(0)

        def body(j, carry):
            accumulate(j - 1)
            probabilities(j)
            return carry

        lax.fori_loop(1, n_kb, body, 0)
        accumulate(n_kb - 1)

    @pl.when(jnp.logical_not(fixed_shift_ok))
    def _():
        def body(j, carry):
            bias = jnp.where(key_ref[j] >= thr, 0.0, NEG_BIG)
            bias = jnp.concatenate([bias] * ATTN_GROUP, axis=1)
            for g in range(ATTN_KV_HEADS):
                s, vtb = logits_t(j, g)
                s = s + bias
                m_old = m_ref[g]
                m_new = jnp.maximum(m_old, jnp.max(s, axis=0, keepdims=True))
                alpha = jnp.exp2((m_old - m_new) * c_exp)
                p = jnp.exp2((s - m_new) * c_exp)
                l_ref[g] = alpha * l_ref[g] + jnp.sum(p, axis=0, keepdims=True)
                acc_ref[g] = alpha * acc_ref[g] + jnp.dot(vtb, p.astype(BF16), preferred_element_type=F32)
                m_ref[g] = m_new
            return carry

        lax.fori_loop(0, n_kb, body, 0)

    n_out = o_ref.shape[1]
    for g in range(ATTN_KV_HEADS):
        out_t = acc_ref[g] / l_ref[g]
        for hh in range(ATTN_GROUP):
            h = g * ATTN_GROUP + hh
            o_ref[0, :, h * ATTN_HEAD_DIM:(h + 1) * ATTN_HEAD_DIM] = (
                out_t[:, hh * tq:(hh + 1) * tq].T[:n_out].astype(o_ref.dtype))


def _rows_out_spec(width, n_steps, out_rows, into, dtype):
    b, rows = out_rows[0], out_rows[1]
    if into is None:
        spec = pl.BlockSpec((1, rows, width), lambda bi, i: (bi, i, 0))
        return spec, jax.ShapeDtypeStruct((b, out_rows[2], width), dtype), [], []
    buf, first_row, rows = into
    assert first_row % rows == 0 and buf.shape[0] == 1 and buf.dtype == dtype
    blk0 = first_row // rows
    spec = pl.BlockSpec((1, rows, width), lambda bi, i: (0, blk0 + bi * n_steps + i, 0))
    return spec, jax.ShapeDtypeStruct(buf.shape, dtype), [pl.BlockSpec(memory_space=pl.ANY)], [buf]


def dsa_attention(q, qi, kx, k, v, ki, *, n_q, tq, tk, q_pos0, k_top, out_rows=None, into=None):
    b = q.shape[0]
    s = k.shape[1]
    assert n_q % tq == 0 and s % tk == 0 and tq == LANES
    n_kb_max = s // tk
    cols = ATTN_GROUP * tq
    vt = v.reshape(b, n_kb_max, tk, ATTN_KV_DIM).transpose(0, 1, 3, 2)
    kern = functools.partial(_attn_kernel, tq=tq, tk=tk, q_pos0=q_pos0, k_top=k_top, n_kb_max=n_kb_max)
    resident = dict(pipeline_mode=pl.Buffered(1))
    out_spec, out_shape, buf_specs, bufs = _rows_out_spec(
        ATTN_Q_DIM, n_q // tq, (b, tq, out_rows or n_q), into, BF16)
    return pl.pallas_call(
        kern,
        grid=(b, n_q // tq),
        in_specs=[
            pl.BlockSpec((1, ATTN_HEADS, tq, ATTN_HEAD_DIM), lambda bi, i: (bi, 0, i, 0)),
            pl.BlockSpec((1, IDX_HEADS, tq, IDX_HEAD_DIM), lambda bi, i: (bi, 0, i, 0)),
            pl.BlockSpec((1, tq, LANES), lambda bi, i: (bi, i, 0)),
            pl.BlockSpec((1, s, ATTN_KV_DIM), lambda bi, i: (bi, 0, 0), **resident),
            pl.BlockSpec((1, n_kb_max, ATTN_KV_DIM, tk), lambda bi, i: (bi, 0, 0, 0), **resident),
            pl.BlockSpec((1, s, IDX_HEAD_DIM), lambda bi, i: (bi, 0, 0), **resident),
        ] + buf_specs,
        out_specs=out_spec,
        out_shape=out_shape,
        input_output_aliases={6: 0} if bufs else {},
        scratch_shapes=[
            pltpu.VMEM((n_kb_max, tk, tq), jnp.int32),
            pltpu.VMEM((ATTN_KV_HEADS, 1, cols), F32),
            pltpu.VMEM((ATTN_KV_HEADS, 1, cols), F32),
            pltpu.VMEM((ATTN_KV_HEADS, ATTN_HEAD_DIM, cols), F32),
            pltpu.VMEM((1, tq), F32),
            pltpu.VMEM((ATTN_KV_HEADS, tk, cols), BF16),
        ],
        compiler_params=_params("parallel", "arbitrary"),
    )(q, qi, kx, k, vt, ki, *bufs)


_LEVELS = (32, 16, 8, 4, 2, 1)


def _hgrn_tables():
    c = CHUNK
    t = np.arange(c)[:, None]
    u = np.arange(c)[None, :]
    mats = [(u <= t), (u > t)]
    masks = []
    for w in _LEVELS:
        r = (t // (2 * w)) * (2 * w) + w - 1
        upper = (t % (2 * w)) >= w
        mats.append((upper & (u > r) & (u <= t)) | ((~upper) & (u > t) & (u <= r)))
        s = np.arange(c)[None, :]
        masks.append(((t // (2 * w)) == (s // (2 * w))) & upper & ((s % (2 * w)) < w))
    masks.append(t == np.arange(c)[None, :])
    table = np.concatenate(mats, axis=0).astype(np.float32)
    return np.concatenate([table] * 3, axis=1), np.stack(masks).astype(np.float32)


def _hgrn_kernel(q_ref, f_ref, i_ref, gt_ref, lb_ref, gn_ref, tab_ref, msk_ref, s0_ref, *rest, layer):
    o_ref, sfin_ref, st_ref = rest[-3:]
    c = CHUNK
    dh = HGRN_HEAD_DIM
    step = pl.program_id(1)

    @pl.when(step == 0)
    def _():
        for h in range(HGRN_HEADS):
            st_ref[h] = s0_ref[0, h].T

    lb_all = lb_ref[...]
    e = jnp.exp(lb_all - jnp.max(lb_all, axis=0, keepdims=True))
    p_lb = e / jnp.sum(e, axis=0, keepdims=True)
    lower = jnp.sum(p_lb[:layer + 1], axis=0, keepdims=True) - p_lb[0:1]

    f = lower + (1.0 - lower) * jax.nn.sigmoid(f_ref[0])
    kk = 1.0 - f
    lf = jnp.log(f)
    lf_hi = lf.astype(BF16)
    r1 = lf - lf_hi.astype(F32)
    lf_mid = r1.astype(BF16)
    lf_lo = (r1 - lf_mid.astype(F32)).astype(BF16)
    lf3 = jnp.concatenate([lf_hi, lf_mid, lf_lo], axis=0)
    ex = jnp.exp(jnp.dot(tab_ref[...], lf3, preferred_element_type=F32))

    qq = q_ref[0]
    vv = i_ref[0]
    q16 = qq.astype(BF16)
    k16 = kk.astype(BF16)
    v16 = vv.astype(BF16)
    q_lv = [(qq * ex[(2 + li) * c:(3 + li) * c]).astype(BF16) for li in range(len(_LEVELS))]
    k_lv = [(kk * ex[(2 + li) * c:(3 + li) * c]).astype(BF16) for li in range(len(_LEVELS))]
    q_in = (qq * ex[0:c]).astype(BF16)
    k_out = (kk * ex[c:2 * c]).astype(BF16)
    gate = jax.nn.sigmoid(gt_ref[0])

    scores = []
    for h in range(HGRN_HEADS):
        sl = slice(h * dh, (h + 1) * dh)
        s_h = msk_ref[len(_LEVELS)] * lax.dot_general(q16[:, sl], k16[:, sl], _NT, preferred_element_type=F32)
        for li in range(len(_LEVELS)):
            s_h = s_h + msk_ref[li] * lax.dot_general(
                q_lv[li][:, sl], k_lv[li][:, sl], _NT, preferred_element_type=F32)
        scores.append(s_h.astype(BF16))

    for h in range(HGRN_HEADS):
        sl = slice(h * dh, (h + 1) * dh)
        st = st_ref[h]
        o_h = jnp.dot(scores[h], v16[:, sl], preferred_element_type=F32)
        o_h = o_h + lax.dot_general(q_in[:, sl], st.astype(BF16), _NT, preferred_element_type=F32)
        st_ref[h] = st * ex[c - 1:c, sl] + jnp.dot(vv[:, sl].T.astype(BF16), k_out[:, sl],
                                                   preferred_element_type=F32)
        o_ref[0, :, sl] = (_rms(o_h, gn_ref[:, sl]) * gate[:, sl]).astype(o_ref.dtype)

    @pl.when(step == pl.num_programs(1) - 1)
    def _():
        for h in range(HGRN_HEADS):
            sfin_ref[0, h] = st_ref[h].T


def hgrn_scan(p, lb, gnorm, s0, *, layer, n_rows=None, out_rows=None, into=None):
    b = p.shape[0]
    l = p.shape[1] if n_rows is None else n_rows
    d = D_MODEL
    c = CHUNK
    tab, msk = _hgrn_tables()
    kern = functools.partial(_hgrn_kernel, layer=layer)
    col = lambda j: pl.BlockSpec((1, c, d), lambda bi, t, j=j: (bi, t, j))
    state_spec = pl.BlockSpec((1, HGRN_HEADS, HGRN_HEAD_DIM, HGRN_HEAD_DIM), lambda bi, t: (bi, 0, 0, 0))
    out_spec, out_shape, buf_specs, bufs = _rows_out_spec(d, l // c, (b, c, out_rows or l), into, BF16)
    return pl.pallas_call(
        kern,
        grid=(b, l // c),
        in_specs=[
            col(0), col(1), col(2), col(3),
            pl.BlockSpec(lb.shape, lambda bi, t: (0, 0)),
            pl.BlockSpec((1, d), lambda bi, t: (0, 0)),
            pl.BlockSpec(tab.shape, lambda bi, t: (0, 0)),
            pl.BlockSpec(msk.shape, lambda bi, t: (0, 0, 0)),
            state_spec,
        ] + buf_specs,
        out_specs=[out_spec, state_spec],
        out_shape=[out_shape, jax.ShapeDtypeStruct(s0.shape, F32)],
        input_output_aliases={9: 0} if bufs else {},
        scratch_shapes=[pltpu.VMEM((HGRN_HEADS, HGRN_HEAD_DIM, HGRN_HEAD_DIM), F32)],
        compiler_params=_params("parallel", "arbitrary"),
    )(p, p, p, p, lb, gnorm.reshape(1, d), jnp.asarray(tab, BF16), jnp.asarray(msk), s0, *bufs)


def _attn_layer(x, g, w_in, w_out, layer, cache_k, cache_v, cache_kidx, n_p, b_s, t_s):
    q, kv, qi, kx = attn_in_proj(x, g, w_in, layer=layer)

    k_new, v_new, ki_new = kv[:, :ATTN_KV_DIM], kv[:, ATTN_KV_DIM:], kx[:, :IDX_HEAD_DIM]
    past = cache_k.shape[1]
    tq = LANES

    o_p = dsa_attention(
        q[None], qi[None], kx[None],
        k_new[:n_p].astype(BF16)[None], v_new[:n_p].astype(BF16)[None], ki_new[:n_p].astype(BF16)[None],
        n_q=n_p, tq=tq, tk=512, q_pos0=0, k_top=min(TOPK_MAX, n_p // 4),
        out_rows=x.shape[0])

    tk_s = 256
    s_all = past + t_s
    n_adm_pad = ((past + tq - 1) // CHUNK + 1) * CHUNK
    s_pad = -(-max(s_all, n_adm_pad) // tk_s) * tk_s

    def with_cache(cache, new, width):
        full = jnp.zeros((b_s, s_pad, width), BF16)
        full = lax.dynamic_update_slice(full, cache.reshape(b_s, past, width).astype(BF16), (0, 0, 0))
        return lax.dynamic_update_slice(full, new[n_p:].astype(BF16).reshape(b_s, t_s, width), (0, past, 0))

    def sample_heads(a, heads, dim):
        a = a[:, n_p:].reshape(heads, b_s, t_s, dim).transpose(1, 0, 2, 3)
        return jnp.pad(a, ((0, 0), (0, 0), (0, tq - t_s), (0, 0)))

    kx_s = jnp.pad(kx[n_p:].reshape(b_s, t_s, LANES), ((0, 0), (0, tq - t_s), (0, 0)))
    o_s = dsa_attention(
        sample_heads(q, ATTN_HEADS, ATTN_HEAD_DIM), sample_heads(qi, IDX_HEADS, IDX_HEAD_DIM), kx_s,
        with_cache(cache_k, k_new, ATTN_KV_DIM), with_cache(cache_v, v_new, ATTN_KV_DIM),
        with_cache(cache_kidx, ki_new, IDX_HEAD_DIM),
        n_q=tq, tq=tq, tk=tk_s, q_pos0=past, k_top=min(TOPK_MAX, s_all // 4),
        into=(o_p, n_p, t_s))

    x = matmul_residual(o_s[0], w_out, x, layer=layer)
    return x, k_new, v_new, ki_new


def _rec_layer(x, g, w_in, w_out, gnorm, rec_lb, state, layer, n_p, b_s, t_s):
    p = norm_matmul(x, g, w_in, layer=layer)
    s0_p = jnp.zeros((1,) + state.shape[1:], F32)
    o_p, s_p = hgrn_scan(p[None], rec_lb, gnorm, s0_p, layer=layer, n_rows=n_p, out_rows=x.shape[0])
    o_s, s_s = hgrn_scan(p[n_p:].reshape(b_s, t_s, -1), rec_lb, gnorm, state, layer=layer,
                         into=(o_p, n_p, t_s))
    x = matmul_residual(o_s[0], w_out, x, layer=layer)
    return x, s_p, s_s


def kernel(x_prompt, x_sample, cache_k, cache_v, cache_kidx, state_s, norm_mix, norm_mlp, norm_final,
           attn_w_in, attn_w_out, rec_w_in, rec_w_out, rec_gnorm, rec_lb, mlp_w_up, mlp_w_down):
    b_p, l_p, d = x_prompt.shape
    b_s, t_s, _ = x_sample.shape
    assert b_p == 1
    n_p = b_p * l_p
    x = jnp.concatenate([x_prompt.reshape(n_p, d), x_sample.reshape(b_s * t_s, d)], axis=0)

    kp, vp, kip, sp = [], [], [], []
    ks, vs, kis, ss = [], [], [], []
    for layer in range(DEPTH):
        j = layer // N_MIXERS
        if layer % N_MIXERS == 0:
            x, k_new, v_new, ki_new = _attn_layer(
                x, norm_mix[layer], attn_w_in, attn_w_out, j,
                cache_k[j], cache_v[j], cache_kidx[j], n_p, b_s, t_s)
            kp.append(k_new[:n_p].reshape(b_p, l_p, ATTN_KV_HEADS, ATTN_HEAD_DIM))
            vp.append(v_new[:n_p].reshape(b_p, l_p, ATTN_KV_HEADS, ATTN_HEAD_DIM))
            kip.append(ki_new[:n_p].reshape(b_p, l_p, IDX_HEAD_DIM))
            ks.append(k_new[n_p:].reshape(b_s, t_s, ATTN_KV_HEADS, ATTN_HEAD_DIM))
            vs.append(v_new[n_p:].reshape(b_s, t_s, ATTN_KV_HEADS, ATTN_HEAD_DIM))
            kis.append(ki_new[n_p:].reshape(b_s, t_s, IDX_HEAD_DIM))
        else:
            x, s_p, s_s = _rec_layer(
                x, norm_mix[layer], rec_w_in, rec_w_out, rec_gnorm[j], rec_lb, state_s[j],
                j, n_p, b_s, t_s)
            sp.append(s_p)
            ss.append(s_s)
        x = mlp_residual(x, norm_mlp[layer], mlp_w_up, mlp_w_down, layer=layer,
                         g_final=norm_final if layer == DEPTH - 1 else None)

    y = x
    return (y[:n_p].reshape(b_p, l_p, d), y[n_p:].reshape(b_s, t_s, d),
            jnp.stack(kp), jnp.stack(vp), jnp.stack(kip), jnp.stack(sp),
            jnp.stack(ks), jnp.stack(vs), jnp.stack(kis), jnp.stack(ss))
```

```python
import functools

import numpy as np
import jax
import jax.numpy as jnp
from jax import lax
from jax.experimental import pallas as pl
from jax.experimental.pallas import tpu as pltpu

D_MODEL = 2048
DEPTH = 4
CHUNK = 64
N_MIXERS = 2
ATTN_HEADS = 16
ATTN_KV_HEADS = 4
ATTN_HEAD_DIM = D_MODEL // ATTN_HEADS
ATTN_GROUP = ATTN_HEADS // ATTN_KV_HEADS
IDX_HEADS = 16
IDX_HEAD_DIM = 64
TOPK_MAX = 256
HGRN_HEAD_DIM = 128
HGRN_HEADS = D_MODEL // HGRN_HEAD_DIM
D_FF = 4 * D_MODEL
RMS_EPS = 1e-6
ATTN_Q_DIM = ATTN_HEADS * ATTN_HEAD_DIM
ATTN_KV_DIM = ATTN_KV_HEADS * ATTN_HEAD_DIM
IDX_Q_DIM = IDX_HEADS * IDX_HEAD_DIM

LANES = 128
SUBLANES = 8
VMEM_LIMIT = 56 * 1024 * 1024

F32 = jnp.float32
BF16 = jnp.bfloat16
INT_MIN = np.int32(-(2 ** 31))
NEG_BIG = -1e30
LOG2_E = 1.4426950408889634

_NT = (((1,), (1,)), ((), ()))


def _params(*sem):
    return pltpu.CompilerParams(dimension_semantics=sem, vmem_limit_bytes=VMEM_LIMIT)


def _rms(x, g):
    ms = jnp.mean(x * x, axis=-1, keepdims=True)
    return (x * lax.rsqrt(ms + RMS_EPS)) * g


def _norm_matmul_kernel(x_ref, g_ref, w_ref, o_ref, h_ref):
    @pl.when(pl.program_id(1) == 0)
    def _():
        h_ref[...] = _rms(x_ref[...], g_ref[...]).astype(BF16)

    o_ref[...] = jnp.dot(h_ref[...], w_ref[...].astype(BF16), preferred_element_type=F32)


def _row_tile(m, cap=1088):
    return max(t for t in range(16, cap + 1, 16) if m % t == 0)


_ONE_BUFFER = dict(pipeline_mode=pl.Buffered(1))


def norm_matmul(x, g, w, *, layer, tn=512):
    m, d = x.shape
    n = w.shape[2]
    tm = _row_tile(m)
    assert n % tn == 0
    return pl.pallas_call(
        _norm_matmul_kernel,
        grid=(m // tm, n // tn),
        in_specs=[
            pl.BlockSpec((tm, d), lambda i, j: (i, 0), **_ONE_BUFFER),
            pl.BlockSpec((1, d), lambda i, j: (0, 0)),
            pl.BlockSpec((None, d, tn), lambda i, j: (layer, 0, j)),
        ],
        out_specs=pl.BlockSpec((tm, tn), lambda i, j: (i, j)),
        out_shape=jax.ShapeDtypeStruct((m, n), F32),
        scratch_shapes=[pltpu.VMEM((tm, d), BF16)],
        compiler_params=_params("parallel", "arbitrary"),
    )(x, g.reshape(1, d), w)


def _attn_in_proj_kernel(x_ref, g_ref, w_ref, wkx_ref, q_ref, kv_ref, qi_ref, kx_ref, h_ref, *, n_q, n_kv, n_qi):
    j = pl.program_id(1)

    @pl.when(j == 0)
    def _():
        h_ref[...] = _rms(x_ref[...], g_ref[...]).astype(BF16)

    def tile():
        return jnp.dot(h_ref[...], w_ref[...].astype(BF16), preferred_element_type=F32)

    @pl.when(j < n_q)
    def _():
        res = tile()
        for hh in range(q_ref.shape[0]):
            q_ref[hh] = res[:, hh * ATTN_HEAD_DIM:(hh + 1) * ATTN_HEAD_DIM].astype(q_ref.dtype)

    @pl.when(jnp.logical_and(j >= n_q, j < n_q + n_kv))
    def _():
        kv_ref[...] = tile()

    @pl.when(jnp.logical_and(j >= n_q + n_kv, j < n_q + n_kv + n_qi))
    def _():
        res = tile()
        for hh in range(qi_ref.shape[0]):
            qi_ref[hh] = res[:, hh * IDX_HEAD_DIM:(hh + 1) * IDX_HEAD_DIM].astype(qi_ref.dtype)

    @pl.when(j == n_q + n_kv + n_qi)
    def _():
        kx_ref[...] = jnp.dot(h_ref[...], wkx_ref[...].astype(BF16), preferred_element_type=F32)


def attn_in_proj(x, g, w_in, *, layer, tn=512):
    m, d = x.shape
    tm = _row_tile(m)
    kv_dim = 2 * ATTN_KV_DIM
    n_q, n_kv, n_qi = ATTN_Q_DIM // tn, kv_dim // tn, IDX_Q_DIM // tn
    n_main = n_q + n_kv + n_qi
    tail = w_in.shape[2] - n_main * tn
    w_kx = jnp.pad(w_in[layer:layer + 1, :, n_main * tn:], ((0, 0), (0, 0), (0, LANES - tail)))
    clip = lambda v, hi: jnp.minimum(jnp.maximum(v, 0), hi)
    return pl.pallas_call(
        functools.partial(_attn_in_proj_kernel, n_q=n_q, n_kv=n_kv, n_qi=n_qi),
        grid=(m // tm, n_main + 1),
        in_specs=[
            pl.BlockSpec((tm, d), lambda i, j: (i, 0), **_ONE_BUFFER),
            pl.BlockSpec((1, d), lambda i, j: (0, 0)),
            pl.BlockSpec((None, d, tn), lambda i, j: (layer, 0, jnp.minimum(j, n_main - 1))),
            pl.BlockSpec((None, d, LANES), lambda i, j: (0, 0, 0)),
        ],
        out_specs=[
            pl.BlockSpec((tn // ATTN_HEAD_DIM, tm, ATTN_HEAD_DIM), lambda i, j: (clip(j, n_q - 1), i, 0)),
            pl.BlockSpec((tm, tn), lambda i, j: (i, clip(j - n_q, n_kv - 1))),
            pl.BlockSpec((tn // IDX_HEAD_DIM, tm, IDX_HEAD_DIM), lambda i, j: (clip(j - n_q - n_kv, n_qi - 1), i, 0)),
            pl.BlockSpec((tm, LANES), lambda i, j: (i, 0)),
        ],
        out_shape=[
            jax.ShapeDtypeStruct((ATTN_HEADS, m, ATTN_HEAD_DIM), BF16),
            jax.ShapeDtypeStruct((m, kv_dim), F32),
            jax.ShapeDtypeStruct((IDX_HEADS, m, IDX_HEAD_DIM), BF16),
            jax.ShapeDtypeStruct((m, LANES), F32),
        ],
        scratch_shapes=[pltpu.VMEM((tm, d), BF16)],
        compiler_params=_params("parallel", "arbitrary"),
    )(x, g.reshape(1, d), w_in, w_kx)


def _matmul_res_kernel(a_ref, w_ref, r_ref, o_ref):
    o_ref[...] = r_ref[...] + jnp.dot(a_ref[...], w_ref[...].astype(BF16), preferred_element_type=F32)


def matmul_residual(a, w, res, *, layer, tn=512):
    m, k = a.shape
    n = w.shape[2]
    tm = _row_tile(m)
    assert n % tn == 0
    return pl.pallas_call(
        _matmul_res_kernel,
        grid=(m // tm, n // tn),
        in_specs=[
            pl.BlockSpec((tm, k), lambda i, j: (i, 0)),
            pl.BlockSpec((None, k, tn), lambda i, j: (layer, 0, j)),
            pl.BlockSpec((tm, tn), lambda i, j: (i, j)),
        ],
        out_specs=pl.BlockSpec((tm, tn), lambda i, j: (i, j)),
        out_shape=jax.ShapeDtypeStruct((m, n), F32),
        compiler_params=_params("parallel", "parallel"),
    )(a, w, res)


def _mlp_kernel(x_ref, g_ref, wu_ref, wd_ref, gf_ref, o_ref, h_ref, *, final_norm):
    @pl.when(pl.program_id(1) == 0)
    def _():
        x = x_ref[...]
        h_ref[...] = _rms(x, g_ref[...]).astype(BF16)
        o_ref[...] = x

    u = jnp.maximum(jnp.dot(h_ref[...], wu_ref[...].astype(BF16), preferred_element_type=F32), 0.0)
    o_ref[...] += jnp.dot((u * u).astype(BF16), wd_ref[...].astype(BF16), preferred_element_type=F32)

    if final_norm:
        @pl.when(pl.program_id(1) == pl.num_programs(1) - 1)
        def _():
            o_ref[...] = _rms(o_ref[...], gf_ref[...])


def mlp_residual(x, g, w_up, w_down, *, layer, g_final=None, tf=512):
    m, d = x.shape
    f = w_up.shape[2]
    tm = _row_tile(m)
    assert f % tf == 0
    final_norm = g_final is not None
    gf = (g_final if final_norm else g).reshape(1, d)
    return pl.pallas_call(
        functools.partial(_mlp_kernel, final_norm=final_norm),
        grid=(m // tm, f // tf),
        in_specs=[
            pl.BlockSpec((tm, d), lambda i, j: (i, 0), **_ONE_BUFFER),
            pl.BlockSpec((1, d), lambda i, j: (0, 0)),
            pl.BlockSpec((None, d, tf), lambda i, j: (layer, 0, j)),
            pl.BlockSpec((None, tf, d), lambda i, j: (layer, j, 0)),
            pl.BlockSpec((1, d), lambda i, j: (0, 0)),
        ],
        out_specs=pl.BlockSpec((tm, d), lambda i, j: (i, 0), **_ONE_BUFFER),
        out_shape=jax.ShapeDtypeStruct((m, d), F32),
        scratch_shapes=[pltpu.VMEM((tm, d), BF16)],
        compiler_params=_params("parallel", "arbitrary"),
    )(x, g.reshape(1, d), w_up, w_down, gf)


def _attn_kernel(q_ref, qi_ref, kx_ref, k_ref, vt_ref, ki_ref, *rest, tq, tk, q_pos0, k_top, n_kb_max):
    o_ref, key_ref, m_ref, l_ref, acc_ref, kn_ref, p_ref = rest[-7:]
    @pl.when(pl.program_id(1) == 0)
    def _():
        def kn_body(j, best):
            kf = k_ref[0, pl.ds(pl.multiple_of(j * tk, tk), tk), :].astype(F32)
            sq = kf * kf
            for g in range(ATTN_KV_HEADS):
                n2 = jnp.sum(sq[:, g * ATTN_HEAD_DIM:(g + 1) * ATTN_HEAD_DIM], axis=1, keepdims=True)
                best = jnp.maximum(best, jnp.max(n2, axis=0, keepdims=True))
            return best

        kn_ref[...] = jnp.broadcast_to(lax.fori_loop(0, n_kb_max, kn_body, jnp.zeros((1, 1), F32)), kn_ref.shape)

    q_start = q_pos0 + pl.program_id(1) * tq
    n_adm = ((q_start + tq - 1) // CHUNK + 1) * CHUNK
    n_kb = jnp.minimum((n_adm + tk - 1) // tk, n_kb_max)

    k_row = lax.broadcasted_iota(jnp.int32, (tk, tq), 0)
    k_lim = ((q_start + lax.broadcasted_iota(jnp.int32, (tk, tq), 1)) // CHUNK + 1) * CHUNK

    w_t = kx_ref[0].T[IDX_HEAD_DIM:IDX_HEAD_DIM + IDX_HEADS, :] * (IDX_HEADS ** -0.5 * IDX_HEAD_DIM ** -0.5)
    qi = qi_ref[0].reshape(IDX_HEADS * tq, IDX_HEAD_DIM)

    def score_body(j, carry):
        kib = ki_ref[0, pl.ds(pl.multiple_of(j * tk, tk), tk), :]
        d = lax.dot_general(kib, qi, _NT, preferred_element_type=F32)
        sc = jnp.zeros((tk, tq), F32)
        for h in range(IDX_HEADS):
            sc = sc + w_t[h:h + 1, :] * jnp.maximum(d[:, h * tq:(h + 1) * tq], 0.0)
        bits = pltpu.bitcast(sc, jnp.int32)
        key = bits ^ ((bits >> 31) & jnp.int32(0x7FFFFFFF))
        key_ref[j] = jnp.where(k_row < k_lim - j * tk, key, INT_MIN)
        return carry

    lax.fori_loop(0, n_kb, score_body, 0)

    rows_acc = 8 * SUBLANES

    def count_where(pred):
        def body(j, cnt):
            blk = key_ref[j]
            hit = pred(blk, j)
            for r in range(tk // rows_acc):
                cnt = jnp.where(hit[r * rows_acc:(r + 1) * rows_acc], cnt + 1.0, cnt)
            return cnt
        cnt = lax.fori_loop(0, n_kb, body, jnp.zeros((rows_acc, tq), F32))
        return jnp.sum(cnt, axis=0, keepdims=True)

    def bit_body(p, state):
        thr, cnt_thr = state
        cand = thr + jnp.left_shift(jnp.int32(1), 31 - p)
        cnt = count_where(lambda blk, j: blk >= cand)
        keep = cnt >= float(k_top)
        return jnp.where(keep, cand, thr), jnp.where(keep, cnt, cnt_thr)

    thr0 = jnp.full((1, tq), INT_MIN, jnp.int32)
    thr_raw, cnt_thr = lax.fori_loop(0, 32, bit_body, (thr0, jnp.full((1, tq), 2.0 ** 30, F32)))
    thr = jnp.maximum(thr_raw, INT_MIN + 1)

    tied = jnp.logical_and(cnt_thr > float(k_top), thr_raw > INT_MIN)
    n_tied = jnp.max(jnp.where(tied, 1.0, 0.0))

    @pl.when(n_tied > 0.0)
    def _():
        need = float(k_top) - count_where(lambda blk, j: blk > thr)
        n_bits = max(1, int(n_kb_max * tk - 1).bit_length())

        def pos_body(b, last):
            step = jnp.left_shift(jnp.int32(1), n_bits - 1 - b)
            cand = last + step - 1
            got = count_where(lambda blk, j: jnp.logical_and(blk == thr, k_row + j * tk <= cand))
            return jnp.where(got < need, last + step, last)

        last = lax.fori_loop(0, n_bits, pos_body, jnp.zeros((1, tq), jnp.int32))

        def demote_body(j, carry):
            blk = key_ref[j]
            drop = jnp.logical_and(blk == thr, k_row + j * tk > last)
            key_ref[j] = jnp.where(drop, thr - 1, blk)
            return carry

        lax.fori_loop(0, n_kb, demote_body, 0)

    m_ref[...] = jnp.full(m_ref.shape, NEG_BIG, F32)
    l_ref[...] = jnp.zeros(l_ref.shape, F32)
    acc_ref[...] = jnp.zeros(acc_ref.shape, F32)
    c_exp = ATTN_HEAD_DIM ** -0.5 * LOG2_E
    cols = ATTN_GROUP * tq

    def logits_t(j, g):
        k0 = pl.multiple_of(j * tk, tk)
        kb = k_ref[0, pl.ds(k0, tk), g * ATTN_HEAD_DIM:(g + 1) * ATTN_HEAD_DIM]
        vtb = vt_ref[0, j, g * ATTN_HEAD_DIM:(g + 1) * ATTN_HEAD_DIM, :]
        qg = q_ref[0, g * ATTN_GROUP:(g + 1) * ATTN_GROUP].reshape(cols, ATTN_HEAD_DIM)
        return lax.dot_general(kb, qg, _NT, preferred_element_type=F32), vtb

    qf = q_ref[0].reshape(ATTN_HEADS * tq, ATTN_HEAD_DIM).astype(F32)
    qn2 = lax.dot_general(jnp.ones((SUBLANES, ATTN_HEAD_DIM), BF16), (qf * qf).astype(BF16), _NT,
                          preferred_element_type=F32)
    qn2_max = qn2[0:1, 0:tq]
    for h in range(1, ATTN_HEADS):
        qn2_max = jnp.maximum(qn2_max, qn2[0:1, h * tq:(h + 1) * tq])
    m_bound = jnp.sqrt(qn2_max * kn_ref[...]) * 1.02
    fixed_shift_ok = jnp.max(m_bound) * (2.0 * c_exp) < 100.0

    @pl.when(fixed_shift_ok)
    def _():
        shift = -c_exp * m_bound
        ones = jnp.ones((2 * SUBLANES, tk), BF16)

        def accumulate(j):
            for g in range(ATTN_KV_HEADS):
                vtb = vt_ref[0, j, g * ATTN_HEAD_DIM:(g + 1) * ATTN_HEAD_DIM, :]
                acc_ref[g] += jnp.dot(vtb, p_ref[g], preferred_element_type=F32)
                l_ref[g] += jnp.dot(ones, p_ref[g], preferred_element_type=F32)[0:1]

        def probabilities(j):
            bias = jnp.where(key_ref[j] >= thr, shift, NEG_BIG)
            bias = jnp.concatenate([bias] * ATTN_GROUP, axis=1)
            for g in range(ATTN_KV_HEADS):
                s, _ = logits_t(j, g)
                p_ref[g] = jnp.exp2(s * c_exp + bias).astype(BF16)

        probabilities(0)

        def body(j, carry):
            accumulate(j - 1)
            probabilities(j)
            return carry

        lax.fori_loop(1, n_kb, body, 0)
        accumulate(n_kb - 1)

    @pl.when(jnp.logical_not(fixed_shift_ok))
    def _():
        def body(j, carry):
            bias = jnp.where(key_ref[j] >= thr, 0.0, NEG_BIG)
            bias = jnp.concatenate([bias] * ATTN_GROUP, axis=1)
            for g in range(ATTN_KV_HEADS):
                s, vtb = logits_t(j, g)
                s = s + bias
                m_old = m_ref[g]
                m_new = jnp.maximum(m_old, jnp.max(s, axis=0, keepdims=True))
                alpha = jnp.exp2((m_old - m_new) * c_exp)
                p = jnp.exp2((s - m_new) * c_exp)
                l_ref[g] = alpha * l_ref[g] + jnp.sum(p, axis=0, keepdims=True)
                acc_ref[g] = alpha * acc_ref[g] + jnp.dot(vtb, p.astype(BF16), preferred_element_type=F32)
                m_ref[g] = m_new
            return carry

        lax.fori_loop(0, n_kb, body, 0)

    n_out = o_ref.shape[1]
    for g in range(ATTN_KV_HEADS):
        out_t = acc_ref[g] / l_ref[g]
        for hh in range(ATTN_GROUP):
            h = g * ATTN_GROUP + hh
            o_ref[0, :, h * ATTN_HEAD_DIM:(h + 1) * ATTN_HEAD_DIM] = (
                out_t[:, hh * tq:(hh + 1) * tq].T[:n_out].astype(o_ref.dtype))


def _rows_out_spec(width, n_steps, out_rows, into, dtype):
    b, rows = out_rows[0], out_rows[1]
    if into is None:
        spec = pl.BlockSpec((1, rows, width), lambda bi, i: (bi, i, 0))
        return spec, jax.ShapeDtypeStruct((b, out_rows[2], width), dtype), [], []
    buf, first_row, rows = into
    assert first_row % rows == 0 and buf.shape[0] == 1 and buf.dtype == dtype
    blk0 = first_row // rows
    spec = pl.BlockSpec((1, rows, width), lambda bi, i: (0, blk0 + bi * n_steps + i, 0))
    return spec, jax.ShapeDtypeStruct(buf.shape, dtype), [pl.BlockSpec(memory_space=pl.ANY)], [buf]


def dsa_attention(q, qi, kx, k, v, ki, *, n_q, tq, tk, q_pos0, k_top, out_rows=None, into=None):
    b = q.shape[0]
    s = k.shape[1]
    assert n_q % tq == 0 and s % tk == 0 and tq == LANES
    n_kb_max = s // tk
    cols = ATTN_GROUP * tq
    vt = v.reshape(b, n_kb_max, tk, ATTN_KV_DIM).transpose(0, 1, 3, 2)
    kern = functools.partial(_attn_kernel, tq=tq, tk=tk, q_pos0=q_pos0, k_top=k_top, n_kb_max=n_kb_max)
    resident = dict(pipeline_mode=pl.Buffered(1))
    out_spec, out_shape, buf_specs, bufs = _rows_out_spec(
        ATTN_Q_DIM, n_q // tq, (b, tq, out_rows or n_q), into, BF16)
    return pl.pallas_call(
        kern,
        grid=(b, n_q // tq),
        in_specs=[
            pl.BlockSpec((1, ATTN_HEADS, tq, ATTN_HEAD_DIM), lambda bi, i: (bi, 0, i, 0)),
            pl.BlockSpec((1, IDX_HEADS, tq, IDX_HEAD_DIM), lambda bi, i: (bi, 0, i, 0)),
            pl.BlockSpec((1, tq, LANES), lambda bi, i: (bi, i, 0)),
            pl.BlockSpec((1, s, ATTN_KV_DIM), lambda bi, i: (bi, 0, 0), **resident),
            pl.BlockSpec((1, n_kb_max, ATTN_KV_DIM, tk), lambda bi, i: (bi, 0, 0, 0), **resident),
            pl.BlockSpec((1, s, IDX_HEAD_DIM), lambda bi, i: (bi, 0, 0), **resident),
        ] + buf_specs,
        out_specs=out_spec,
        out_shape=out_shape,
        input_output_aliases={6: 0} if bufs else {},
        scratch_shapes=[
            pltpu.VMEM((n_kb_max, tk, tq), jnp.int32),
            pltpu.VMEM((ATTN_KV_HEADS, 1, cols), F32),
            pltpu.VMEM((ATTN_KV_HEADS, 1, cols), F32),
            pltpu.VMEM((ATTN_KV_HEADS, ATTN_HEAD_DIM, cols), F32),
            pltpu.VMEM((1, tq), F32),
            pltpu.VMEM((ATTN_KV_HEADS, tk, cols), BF16),
        ],
        compiler_params=_params("parallel", "arbitrary"),
    )(q, qi, kx, k, vt, ki, *bufs)


_LEVELS = (32, 16, 8, 4, 2, 1)


def _hgrn_tables():
    c = CHUNK
    t = np.arange(c)[:, None]
    u = np.arange(c)[None, :]
    mats = [(u <= t), (u > t)]
    masks = []
    for w in _LEVELS:
        r = (t // (2 * w)) * (2 * w) + w - 1
        upper = (t % (2 * w)) >= w
        mats.append((upper & (u > r) & (u <= t)) | ((~upper) & (u > t) & (u <= r)))
        s = np.arange(c)[None, :]
        masks.append(((t // (2 * w)) == (s // (2 * w))) & upper & ((s % (2 * w)) < w))
    masks.append(t == np.arange(c)[None, :])
    table = np.concatenate(mats, axis=0).astype(np.float32)
    return np.concatenate([table] * 3, axis=1), np.stack(masks).astype(np.float32)


def _hgrn_kernel(q_ref, f_ref, i_ref, gt_ref, lb_ref, gn_ref, tab_ref, msk_ref, s0_ref, *rest, layer):
    o_ref, sfin_ref, st_ref = rest[-3:]
    c = CHUNK
    dh = HGRN_HEAD_DIM
    step = pl.program_id(1)

    @pl.when(step == 0)
    def _():
        for h in range(HGRN_HEADS):
            st_ref[h] = s0_ref[0, h].T

    lb_all = lb_ref[...]
    e = jnp.exp(lb_all - jnp.max(lb_all, axis=0, keepdims=True))
    p_lb = e / jnp.sum(e, axis=0, keepdims=True)
    lower = jnp.sum(p_lb[:layer + 1], axis=0, keepdims=True) - p_lb[0:1]

    def chunk_operands(ci):
        rows = slice(ci * c, (ci + 1) * c)
        f = lower + (1.0 - lower) * jax.nn.sigmoid(f_ref[0, rows])
        kk = 1.0 - f
        lf = jnp.log2(f)
        lf_hi = lf.astype(BF16)
        r1 = lf - lf_hi.astype(F32)
        lf_mid = r1.astype(BF16)
        lf_lo = (r1 - lf_mid.astype(F32)).astype(BF16)
        lf3 = jnp.concatenate([lf_hi, lf_mid, lf_lo], axis=0)
        ex = jnp.exp2(jnp.dot(tab_ref[...], lf3, preferred_element_type=F32))
        qq = q_ref[0, rows]
        vv = i_ref[0, rows]
        ops = dict(q16=qq.astype(BF16), k16=kk.astype(BF16), v16=vv.astype(BF16), vv=vv)
        ops["q_lv"] = [(qq * ex[(2 + li) * c:(3 + li) * c]).astype(BF16) for li in range(len(_LEVELS))]
        ops["k_lv"] = [(kk * ex[(2 + li) * c:(3 + li) * c]).astype(BF16) for li in range(len(_LEVELS))]
        ops["q_in"] = (qq * ex[0:c]).astype(BF16)
        ops["k_out"] = (kk * ex[c:2 * c]).astype(BF16)
        ops["decay"] = ex[c - 1:c]
        ops["gate"] = jax.nn.sigmoid(gt_ref[0, rows])
        return ops

    def chunk_outputs(ci, ops):
        rows = slice(ci * c, (ci + 1) * c)
        scores = []
        for h in range(HGRN_HEADS):
            sl = slice(h * dh, (h + 1) * dh)
            s_h = msk_ref[len(_LEVELS)] * lax.dot_general(
                ops["q16"][:, sl], ops["k16"][:, sl], _NT, preferred_element_type=F32)
            for li in range(len(_LEVELS)):
                s_h = s_h + msk_ref[li] * lax.dot_general(
                    ops["q_lv"][li][:, sl], ops["k_lv"][li][:, sl], _NT, preferred_element_type=F32)
            scores.append(s_h.astype(BF16))
        for h in range(HGRN_HEADS):
            sl = slice(h * dh, (h + 1) * dh)
            st = st_ref[h]
            o_h = jnp.dot(scores[h], ops["v16"][:, sl], preferred_element_type=F32)
            o_h = o_h + lax.dot_general(ops["q_in"][:, sl], st.astype(BF16), _NT, preferred_element_type=F32)
            st_ref[h] = st * ops["decay"][:, sl] + jnp.dot(
                ops["vv"][:, sl].T.astype(BF16), ops["k_out"][:, sl], preferred_element_type=F32)
            o_ref[0, rows, sl] = (_rms(o_h, gn_ref[:, sl]) * ops["gate"][:, sl]).astype(o_ref.dtype)

    n_sub = q_ref.shape[1] // c
    ops = chunk_operands(0)
    for ci in range(n_sub):
        ops_next = chunk_operands(ci + 1) if ci + 1 < n_sub else None
        chunk_outputs(ci, ops)
        ops = ops_next

    @pl.when(step == pl.num_programs(1) - 1)
    def _():
        for h in range(HGRN_HEADS):
            sfin_ref[0, h] = st_ref[h].T


def hgrn_scan(p, lb, gnorm, s0, *, layer, n_rows=None, out_rows=None, into=None):
    b = p.shape[0]
    l = p.shape[1] if n_rows is None else n_rows
    d = D_MODEL
    c = max(r for r in (4 * CHUNK, 2 * CHUNK, CHUNK) if l % r == 0)
    tab, msk = _hgrn_tables()
    kern = functools.partial(_hgrn_kernel, layer=layer)
    col = lambda j: pl.BlockSpec((1, c, d), lambda bi, t, j=j: (bi, t, j))
    state_spec = pl.BlockSpec((1, HGRN_HEADS, HGRN_HEAD_DIM, HGRN_HEAD_DIM), lambda bi, t: (bi, 0, 0, 0))
    out_spec, out_shape, buf_specs, bufs = _rows_out_spec(d, l // c, (b, c, out_rows or l), into, BF16)
    return pl.pallas_call(
        kern,
        grid=(b, l // c),
        in_specs=[
            col(0), col(1), col(2), col(3),
            pl.BlockSpec(lb.shape, lambda bi, t: (0, 0)),
            pl.BlockSpec((1, d), lambda bi, t: (0, 0)),
            pl.BlockSpec(tab.shape, lambda bi, t: (0, 0)),
            pl.BlockSpec(msk.shape, lambda bi, t: (0, 0, 0)),
            state_spec,
        ] + buf_specs,
        out_specs=[out_spec, state_spec],
        out_shape=[out_shape, jax.ShapeDtypeStruct(s0.shape, F32)],
        input_output_aliases={9: 0} if bufs else {},
        scratch_shapes=[pltpu.VMEM((HGRN_HEADS, HGRN_HEAD_DIM, HGRN_HEAD_DIM), F32)],
        compiler_params=_params("parallel", "arbitrary"),
    )(p, p, p, p, lb, gnorm.reshape(1, d), jnp.asarray(tab, BF16), jnp.asarray(msk), s0, *bufs)


def _attn_layer(x, g, w_in, w_out, layer, cache_k, cache_v, cache_kidx, n_p, b_s, t_s):
    q, kv, qi, kx = attn_in_proj(x, g, w_in, layer=layer)

    k_new, v_new, ki_new = kv[:, :ATTN_KV_DIM], kv[:, ATTN_KV_DIM:], kx[:, :IDX_HEAD_DIM]
    past = cache_k.shape[1]
    tq = LANES

    o_p = dsa_attention(
        q[None], qi[None], kx[None],
        k_new[:n_p].astype(BF16)[None], v_new[:n_p].astype(BF16)[None], ki_new[:n_p].astype(BF16)[None],
        n_q=n_p, tq=tq, tk=512, q_pos0=0, k_top=min(TOPK_MAX, n_p // 4),
        out_rows=x.shape[0])

    tk_s = 256
    s_all = past + t_s
    n_adm_pad = ((past + tq - 1) // CHUNK + 1) * CHUNK
    s_pad = -(-max(s_all, n_adm_pad) // tk_s) * tk_s

    def with_cache(cache, new, width):
        full = jnp.zeros((b_s, s_pad, width), BF16)
        full = lax.dynamic_update_slice(full, cache.reshape(b_s, past, width).astype(BF16), (0, 0, 0))
        return lax.dynamic_update_slice(full, new[n_p:].astype(BF16).reshape(b_s, t_s, width), (0, past, 0))

    def sample_heads(a, heads, dim):
        a = a[:, n_p:].reshape(heads, b_s, t_s, dim).transpose(1, 0, 2, 3)
        return jnp.pad(a, ((0, 0), (0, 0), (0, tq - t_s), (0, 0)))

    kx_s = jnp.pad(kx[n_p:].reshape(b_s, t_s, LANES), ((0, 0), (0, tq - t_s), (0, 0)))
    o_s = dsa_attention(
        sample_heads(q, ATTN_HEADS, ATTN_HEAD_DIM), sample_heads(qi, IDX_HEADS, IDX_HEAD_DIM), kx_s,
        with_cache(cache_k, k_new, ATTN_KV_DIM), with_cache(cache_v, v_new, ATTN_KV_DIM),
        with_cache(cache_kidx, ki_new, IDX_HEAD_DIM),
        n_q=tq, tq=tq, tk=tk_s, q_pos0=past, k_top=min(TOPK_MAX, s_all // 4),
        into=(o_p, n_p, t_s))

    x = matmul_residual(o_s[0], w_out, x, layer=layer)
    return x, k_new, v_new, ki_new


def _rec_layer(x, g, w_in, w_out, gnorm, rec_lb, state, layer, n_p, b_s, t_s):
    p = norm_matmul(x, g, w_in, layer=layer)
    s0_p = jnp.zeros((1,) + state.shape[1:], F32)
    o_p, s_p = hgrn_scan(p[None], rec_lb, gnorm, s0_p, layer=layer, n_rows=n_p, out_rows=x.shape[0])
    o_s, s_s = hgrn_scan(p[n_p:].reshape(b_s, t_s, -1), rec_lb, gnorm, state, layer=layer,
                         into=(o_p, n_p, t_s))
    x = matmul_residual(o_s[0], w_out, x, layer=layer)
    return x, s_p, s_s


def kernel(x_prompt, x_sample, cache_k, cache_v, cache_kidx, state_s, norm_mix, norm_mlp, norm_final,
           attn_w_in, attn_w_out, rec_w_in, rec_w_out, rec_gnorm, rec_lb, mlp_w_up, mlp_w_down):
    b_p, l_p, d = x_prompt.shape
    b_s, t_s, _ = x_sample.shape
    assert b_p == 1
    n_p = b_p * l_p
    x = jnp.concatenate([x_prompt.reshape(n_p, d), x_sample.reshape(b_s * t_s, d)], axis=0)

    kp, vp, kip, sp = [], [], [], []
    ks, vs, kis, ss = [], [], [], []
    for layer in range(DEPTH):
        j = layer // N_MIXERS
        if layer % N_MIXERS == 0:
            x, k_new, v_new, ki_new = _attn_layer(
                x, norm_mix[layer], attn_w_in, attn_w_out, j,
                cache_k[j], cache_v[j], cache_kidx[j], n_p, b_s, t_s)
            kp.append(k_new[:n_p].reshape(b_p, l_p, ATTN_KV_HEADS, ATTN_HEAD_DIM))
            vp.append(v_new[:n_p].reshape(b_p, l_p, ATTN_KV_HEADS, ATTN_HEAD_DIM))
            kip.append(ki_new[:n_p].reshape(b_p, l_p, IDX_HEAD_DIM))
            ks.append(k_new[n_p:].reshape(b_s, t_s, ATTN_KV_HEADS, ATTN_HEAD_DIM))
            vs.append(v_new[n_p:].reshape(b_s, t_s, ATTN_KV_HEADS, ATTN_HEAD_DIM))
            kis.append(ki_new[n_p:].reshape(b_s, t_s, IDX_HEAD_DIM))
        else:
            x, s_p, s_s = _rec_layer(
                x, norm_mix[layer], rec_w_in, rec_w_out, rec_gnorm[j], rec_lb, state_s[j],
                j, n_p, b_s, t_s)
            sp.append(s_p)
            ss.append(s_s)
        x = mlp_residual(x, norm_mlp[layer], mlp_w_up, mlp_w_down, layer=layer,
                         g_final=norm_final if layer == DEPTH - 1 else None)

    y = x
    return (y[:n_p].reshape(b_p, l_p, d), y[n_p:].reshape(b_s, t_s, d),
            jnp.stack(kp), jnp.stack(vp), jnp.stack(kip), jnp.stack(sp),
            jnp.stack(ks), jnp.stack(vs), jnp.stack(kis), jnp.stack(ss))
```

```python
import functools

import numpy as np
import jax
import jax.numpy as jnp
from jax import lax
from jax.experimental import pallas as pl
from jax.experimental.pallas import tpu as pltpu

D_MODEL = 2048
DEPTH = 4
CHUNK = 64
N_MIXERS = 2
ATTN_HEADS = 16
ATTN_KV_HEADS = 4
ATTN_HEAD_DIM = D_MODEL // ATTN_HEADS
ATTN_GROUP = ATTN_HEADS // ATTN_KV_HEADS
IDX_HEADS = 16
IDX_HEAD_DIM = 64
TOPK_MAX = 256
HGRN_HEAD_DIM = 128
HGRN_HEADS = D_MODEL // HGRN_HEAD_DIM
D_FF = 4 * D_MODEL
RMS_EPS = 1e-6
ATTN_Q_DIM = ATTN_HEADS * ATTN_HEAD_DIM
ATTN_KV_DIM = ATTN_KV_HEADS * ATTN_HEAD_DIM
IDX_Q_DIM = IDX_HEADS * IDX_HEAD_DIM

LANES = 128
SUBLANES = 8
VMEM_LIMIT = 56 * 1024 * 1024

F32 = jnp.float32
BF16 = jnp.bfloat16
INT_MIN = np.int32(-(2 ** 31))
NEG_BIG = -1e30
LOG2_E = 1.4426950408889634

_NT = (((1,), (1,)), ((), ()))


def _params(*sem):
    return pltpu.CompilerParams(dimension_semantics=sem, vmem_limit_bytes=VMEM_LIMIT)


def _rms(x, g):
    ms = jnp.mean(x * x, axis=-1, keepdims=True)
    return (x * lax.rsqrt(ms + RMS_EPS)) * g


def _norm_matmul_kernel(x_ref, g_ref, w_ref, o_ref, h_ref):
    @pl.when(pl.program_id(1) == 0)
    def _():
        h_ref[...] = _rms(x_ref[...], g_ref[...]).astype(BF16)

    o_ref[...] = jnp.dot(h_ref[...], w_ref[...].astype(BF16), preferred_element_type=F32)


def _row_tile(m, cap=1088):
    return max(t for t in range(16, cap + 1, 16) if m % t == 0)


_ONE_BUFFER = dict(pipeline_mode=pl.Buffered(1))


def norm_matmul(x, g, w, *, layer, tn=512):
    m, d = x.shape
    n = w.shape[2]
    tm = _row_tile(m)
    assert n % tn == 0
    return pl.pallas_call(
        _norm_matmul_kernel,
        grid=(m // tm, n // tn),
        in_specs=[
            pl.BlockSpec((tm, d), lambda i, j: (i, 0), **_ONE_BUFFER),
            pl.BlockSpec((1, d), lambda i, j: (0, 0)),
            pl.BlockSpec((None, d, tn), lambda i, j: (layer, 0, j)),
        ],
        out_specs=pl.BlockSpec((tm, tn), lambda i, j: (i, j)),
        out_shape=jax.ShapeDtypeStruct((m, n), F32),
        scratch_shapes=[pltpu.VMEM((tm, d), BF16)],
        compiler_params=_params("parallel", "arbitrary"),
    )(x, g.reshape(1, d), w)


def _attn_in_proj_kernel(x_ref, g_ref, w_ref, wkx_ref, q_ref, kv_ref, qi_ref, kx_ref, h_ref, *, n_q, n_kv, n_qi):
    j = pl.program_id(1)

    @pl.when(j == 0)
    def _():
        h_ref[...] = _rms(x_ref[...], g_ref[...]).astype(BF16)

    def tile():
        return jnp.dot(h_ref[...], w_ref[...].astype(BF16), preferred_element_type=F32)

    @pl.when(j < n_q)
    def _():
        res = tile()
        for hh in range(q_ref.shape[0]):
            q_ref[hh] = res[:, hh * ATTN_HEAD_DIM:(hh + 1) * ATTN_HEAD_DIM].astype(q_ref.dtype)

    @pl.when(jnp.logical_and(j >= n_q, j < n_q + n_kv))
    def _():
        kv_ref[...] = tile()

    @pl.when(jnp.logical_and(j >= n_q + n_kv, j < n_q + n_kv + n_qi))
    def _():
        res = tile()
        for hh in range(qi_ref.shape[0]):
            qi_ref[hh] = res[:, hh * IDX_HEAD_DIM:(hh + 1) * IDX_HEAD_DIM].astype(qi_ref.dtype)

    @pl.when(j == n_q + n_kv + n_qi)
    def _():
        kx_ref[...] = jnp.dot(h_ref[...], wkx_ref[...].astype(BF16), preferred_element_type=F32)


def attn_in_proj(x, g, w_in, *, layer, tn=512):
    m, d = x.shape
    tm = _row_tile(m)
    kv_dim = 2 * ATTN_KV_DIM
    n_q, n_kv, n_qi = ATTN_Q_DIM // tn, kv_dim // tn, IDX_Q_DIM // tn
    n_main = n_q + n_kv + n_qi
    tail = w_in.shape[2] - n_main * tn
    w_kx = jnp.pad(w_in[layer:layer + 1, :, n_main * tn:], ((0, 0), (0, 0), (0, LANES - tail)))
    clip = lambda v, hi: jnp.minimum(jnp.maximum(v, 0), hi)
    return pl.pallas_call(
        functools.partial(_attn_in_proj_kernel, n_q=n_q, n_kv=n_kv, n_qi=n_qi),
        grid=(m // tm, n_main + 1),
        in_specs=[
            pl.BlockSpec((tm, d), lambda i, j: (i, 0), **_ONE_BUFFER),
            pl.BlockSpec((1, d), lambda i, j: (0, 0)),
            pl.BlockSpec((None, d, tn), lambda i, j: (layer, 0, jnp.minimum(j, n_main - 1))),
            pl.BlockSpec((None, d, LANES), lambda i, j: (0, 0, 0)),
        ],
        out_specs=[
            pl.BlockSpec((tn // ATTN_HEAD_DIM, tm, ATTN_HEAD_DIM), lambda i, j: (clip(j, n_q - 1), i, 0)),
            pl.BlockSpec((tm, tn), lambda i, j: (i, clip(j - n_q, n_kv - 1))),
            pl.BlockSpec((tn // IDX_HEAD_DIM, tm, IDX_HEAD_DIM), lambda i, j: (clip(j - n_q - n_kv, n_qi - 1), i, 0)),
            pl.BlockSpec((tm, LANES), lambda i, j: (i, 0)),
        ],
        out_shape=[
            jax.ShapeDtypeStruct((ATTN_HEADS, m, ATTN_HEAD_DIM), BF16),
            jax.ShapeDtypeStruct((m, kv_dim), F32),
            jax.ShapeDtypeStruct((IDX_HEADS, m, IDX_HEAD_DIM), BF16),
            jax.ShapeDtypeStruct((m, LANES), F32),
        ],
        scratch_shapes=[pltpu.VMEM((tm, d), BF16)],
        compiler_params=_params("parallel", "arbitrary"),
    )(x, g.reshape(1, d), w_in, w_kx)


def _matmul_res_kernel(a_ref, w_ref, r_ref, o_ref):
    o_ref[...] = r_ref[...] + jnp.dot(a_ref[...], w_ref[...].astype(BF16), preferred_element_type=F32)


def matmul_residual(a, w, res, *, layer, tn=512):
    m, k = a.shape
    n = w.shape[2]
    tm = _row_tile(m)
    assert n % tn == 0
    return pl.pallas_call(
        _matmul_res_kernel,
        grid=(m // tm, n // tn),
        in_specs=[
            pl.BlockSpec((tm, k), lambda i, j: (i, 0)),
            pl.BlockSpec((None, k, tn), lambda i, j: (layer, 0, j)),
            pl.BlockSpec((tm, tn), lambda i, j: (i, j)),
        ],
        out_specs=pl.BlockSpec((tm, tn), lambda i, j: (i, j)),
        out_shape=jax.ShapeDtypeStruct((m, n), F32),
        compiler_params=_params("parallel", "parallel"),
    )(a, w, res)


def _mlp_kernel(x_ref, g_ref, wu_ref, wd_ref, gf_ref, o_ref, h_ref, *, final_norm):
    @pl.when(pl.program_id(1) == 0)
    def _():
        x = x_ref[...]
        h_ref[...] = _rms(x, g_ref[...]).astype(BF16)
        o_ref[...] = x

    u = jnp.maximum(jnp.dot(h_ref[...], wu_ref[...].astype(BF16), preferred_element_type=F32), 0.0)
    o_ref[...] += jnp.dot((u * u).astype(BF16), wd_ref[...].astype(BF16), preferred_element_type=F32)

    if final_norm:
        @pl.when(pl.program_id(1) == pl.num_programs(1) - 1)
        def _():
            o_ref[...] = _rms(o_ref[...], gf_ref[...])


def mlp_residual(x, g, w_up, w_down, *, layer, g_final=None, tf=512):
    m, d = x.shape
    f = w_up.shape[2]
    tm = _row_tile(m)
    assert f % tf == 0
    final_norm = g_final is not None
    gf = (g_final if final_norm else g).reshape(1, d)
    return pl.pallas_call(
        functools.partial(_mlp_kernel, final_norm=final_norm),
        grid=(m // tm, f // tf),
        in_specs=[
            pl.BlockSpec((tm, d), lambda i, j: (i, 0), **_ONE_BUFFER),
            pl.BlockSpec((1, d), lambda i, j: (0, 0)),
            pl.BlockSpec((None, d, tf), lambda i, j: (layer, 0, j)),
            pl.BlockSpec((None, tf, d), lambda i, j: (layer, j, 0)),
            pl.BlockSpec((1, d), lambda i, j: (0, 0)),
        ],
        out_specs=pl.BlockSpec((tm, d), lambda i, j: (i, 0), **_ONE_BUFFER),
        out_shape=jax.ShapeDtypeStruct((m, d), F32),
        scratch_shapes=[pltpu.VMEM((tm, d), BF16)],
        compiler_params=_params("parallel", "arbitrary"),
    )(x, g.reshape(1, d), w_up, w_down, gf)


def _attn_kernel(q_ref, qi_ref, kx_ref, k_ref, vt_ref, ki_ref, *rest, tq, tk, q_pos0, k_top, n_kb_max):
    o_ref, key_ref, m_ref, l_ref, acc_ref, kn_ref, p_ref = rest[-7:]
    @pl.when(pl.program_id(1) == 0)
    def _():
        def kn_body(j, best):
            kf = k_ref[0, pl.ds(pl.multiple_of(j * tk, tk), tk), :].astype(F32)
            sq = kf * kf
            for g in range(ATTN_KV_HEADS):
                n2 = jnp.sum(sq[:, g * ATTN_HEAD_DIM:(g + 1) * ATTN_HEAD_DIM], axis=1, keepdims=True)
                best = jnp.maximum(best, jnp.max(n2, axis=0, keepdims=True))
            return best

        kn_ref[...] = jnp.broadcast_to(lax.fori_loop(0, n_kb_max, kn_body, jnp.zeros((1, 1), F32)), kn_ref.shape)

    q_start = q_pos0 + pl.program_id(1) * tq
    n_adm = ((q_start + tq - 1) // CHUNK + 1) * CHUNK
    n_kb = jnp.minimum((n_adm + tk - 1) // tk, n_kb_max)

    k_row = lax.broadcasted_iota(jnp.int32, (tk, tq), 0)
    k_lim = ((q_start + lax.broadcasted_iota(jnp.int32, (tk, tq), 1)) // CHUNK + 1) * CHUNK

    w_t = kx_ref[0].T[IDX_HEAD_DIM:IDX_HEAD_DIM + IDX_HEADS, :] * (IDX_HEADS ** -0.5 * IDX_HEAD_DIM ** -0.5)
    qi = qi_ref[0].reshape(IDX_HEADS * tq, IDX_HEAD_DIM)

    def score_body(j, carry):
        kib = ki_ref[0, pl.ds(pl.multiple_of(j * tk, tk), tk), :]
        d = lax.dot_general(kib, qi, _NT, preferred_element_type=F32)
        sc = jnp.zeros((tk, tq), F32)
        for h in range(IDX_HEADS):
            sc = sc + w_t[h:h + 1, :] * jnp.maximum(d[:, h * tq:(h + 1) * tq], 0.0)
        bits = pltpu.bitcast(sc, jnp.int32)
        key = bits ^ ((bits >> 31) & jnp.int32(0x7FFFFFFF))
        key_ref[j] = jnp.where(k_row < k_lim - j * tk, key, INT_MIN)
        return carry

    lax.fori_loop(0, n_kb, score_body, 0)

    rows_acc = 8 * SUBLANES

    def count_where(pred):
        def body(j, cnt):
            blk = key_ref[j]
            hit = pred(blk, j)
            for r in range(tk // rows_acc):
                cnt = jnp.where(hit[r * rows_acc:(r + 1) * rows_acc], cnt + 1.0, cnt)
            return cnt
        cnt = lax.fori_loop(0, n_kb, body, jnp.zeros((rows_acc, tq), F32))
        return jnp.sum(cnt, axis=0, keepdims=True)

    def bit_body(p, state):
        thr, cnt_thr = state
        cand = thr + jnp.left_shift(jnp.int32(1), 31 - p)
        cnt = count_where(lambda blk, j: blk >= cand)
        keep = cnt >= float(k_top)
        return jnp.where(keep, cand, thr), jnp.where(keep, cnt, cnt_thr)

    def search_bits(lo, hi, state):
        return lax.fori_loop(lo, hi, bit_body, state)

    def settled(state):
        return jnp.min(jnp.where(state[1] == float(k_top), 1.0, 0.0)) > 0.0

    state = search_bits(0, 26, (jnp.full((1, tq), INT_MIN, jnp.int32), jnp.full((1, tq), 2.0 ** 30, F32)))
    for lo, hi in ((26, 28), (28, 30), (30, 32)):
        state = lax.cond(settled(state), lambda s: s, functools.partial(search_bits, lo, hi), state)
    thr_raw, cnt_thr = state
    thr = jnp.maximum(thr_raw, INT_MIN + 1)

    tied = jnp.logical_and(cnt_thr > float(k_top), thr_raw > INT_MIN)
    n_tied = jnp.max(jnp.where(tied, 1.0, 0.0))

    @pl.when(n_tied > 0.0)
    def _():
        need = float(k_top) - count_where(lambda blk, j: blk > thr)
        n_bits = max(1, int(n_kb_max * tk - 1).bit_length())

        def pos_body(b, last):
            step = jnp.left_shift(jnp.int32(1), n_bits - 1 - b)
            cand = last + step - 1
            got = count_where(lambda blk, j: jnp.logical_and(blk == thr, k_row + j * tk <= cand))
            return jnp.where(got < need, last + step, last)

        last = lax.fori_loop(0, n_bits, pos_body, jnp.zeros((1, tq), jnp.int32))

        def demote_body(j, carry):
            blk = key_ref[j]
            drop = jnp.logical_and(blk == thr, k_row + j * tk > last)
            key_ref[j] = jnp.where(drop, thr - 1, blk)
            return carry

        lax.fori_loop(0, n_kb, demote_body, 0)

    m_ref[...] = jnp.full(m_ref.shape, NEG_BIG, F32)
    l_ref[...] = jnp.zeros(l_ref.shape, F32)
    acc_ref[...] = jnp.zeros(acc_ref.shape, F32)
    c_exp = ATTN_HEAD_DIM ** -0.5 * LOG2_E
    cols = ATTN_GROUP * tq

    def logits_t(j, g):
        k0 = pl.multiple_of(j * tk, tk)
        kb = k_ref[0, pl.ds(k0, tk), g * ATTN_HEAD_DIM:(g + 1) * ATTN_HEAD_DIM]
        vtb = vt_ref[0, j, g * ATTN_HEAD_DIM:(g + 1) * ATTN_HEAD_DIM, :]
        qg = q_ref[0, g * ATTN_GROUP:(g + 1) * ATTN_GROUP].reshape(cols, ATTN_HEAD_DIM)
        return lax.dot_general(kb, qg, _NT, preferred_element_type=F32), vtb

    qf = q_ref[0].reshape(ATTN_HEADS * tq, ATTN_HEAD_DIM).astype(F32)
    qn2 = lax.dot_general(jnp.ones((SUBLANES, ATTN_HEAD_DIM), BF16), (qf * qf).astype(BF16), _NT,
                          preferred_element_type=F32)
    qn2_max = qn2[0:1, 0:tq]
    for h in range(1, ATTN_HEADS):
        qn2_max = jnp.maximum(qn2_max, qn2[0:1, h * tq:(h + 1) * tq])
    m_bound = jnp.sqrt(qn2_max * kn_ref[...]) * 1.02
    fixed_shift_ok = jnp.max(m_bound) * (2.0 * c_exp) < 100.0

    @pl.when(fixed_shift_ok)
    def _():
        shift = -c_exp * m_bound
        ones = jnp.ones((2 * SUBLANES, tk), BF16)

        def accumulate(j):
            for g in range(ATTN_KV_HEADS):
                vtb = vt_ref[0, j, g * ATTN_HEAD_DIM:(g + 1) * ATTN_HEAD_DIM, :]
                acc_ref[g] += jnp.dot(vtb, p_ref[g], preferred_element_type=F32)
                l_ref[g] += jnp.dot(ones, p_ref[g], preferred_element_type=F32)[0:1]

        def probabilities(j):
            bias = jnp.where(key_ref[j] >= thr, shift, NEG_BIG)
            bias = jnp.concatenate([bias] * ATTN_GROUP, axis=1)
            for g in range(ATTN_KV_HEADS):
                s, _ = logits_t(j, g)
                p_ref[g] = jnp.exp2(s * c_exp + bias).astype(BF16)

        probabilities(0)

        def body(j, carry):
            accumulate(j - 1)
            probabilities(j)
            return carry

        lax.fori_loop(1, n_kb, body, 0)
        accumulate(n_kb - 1)

    @pl.when(jnp.logical_not(fixed_shift_ok))
    def _():
        def body(j, carry):
            bias = jnp.where(key_ref[j] >= thr, 0.0, NEG_BIG)
            bias = jnp.concatenate([bias] * ATTN_GROUP, axis=1)
            for g in range(ATTN_KV_HEADS):
                s, vtb = logits_t(j, g)
                s = s + bias
                m_old = m_ref[g]
                m_new = jnp.maximum(m_old, jnp.max(s, axis=0, keepdims=True))
                alpha = jnp.exp2((m_old - m_new) * c_exp)
                p = jnp.exp2((s - m_new) * c_exp)
                l_ref[g] = alpha * l_ref[g] + jnp.sum(p, axis=0, keepdims=True)
                acc_ref[g] = alpha * acc_ref[g] + jnp.dot(vtb, p.astype(BF16), preferred_element_type=F32)
                m_ref[g] = m_new
            return carry

        lax.fori_loop(0, n_kb, body, 0)

    n_out = o_ref.shape[1]
    for g in range(ATTN_KV_HEADS):
        out_t = acc_ref[g] / l_ref[g]
        for hh in range(ATTN_GROUP):
            h = g * ATTN_GROUP + hh
            o_ref[0, :, h * ATTN_HEAD_DIM:(h + 1) * ATTN_HEAD_DIM] = (
                out_t[:, hh * tq:(hh + 1) * tq].T[:n_out].astype(o_ref.dtype))


def _rows_out_spec(width, n_steps, out_rows, into, dtype):
    b, rows = out_rows[0], out_rows[1]
    if into is None:
        spec = pl.BlockSpec((1, rows, width), lambda bi, i: (bi, i, 0))
        return spec, jax.ShapeDtypeStruct((b, out_rows[2], width), dtype), [], []
    buf, first_row, rows = into
    assert first_row % rows == 0 and buf.shape[0] == 1 and buf.dtype == dtype
    blk0 = first_row // rows
    spec = pl.BlockSpec((1, rows, width), lambda bi, i: (0, blk0 + bi * n_steps + i, 0))
    return spec, jax.ShapeDtypeStruct(buf.shape, dtype), [pl.BlockSpec(memory_space=pl.ANY)], [buf]


def dsa_attention(q, qi, kx, k, v, ki, *, n_q, tq, tk, q_pos0, k_top, out_rows=None, into=None):
    b = q.shape[0]
    s = k.shape[1]
    assert n_q % tq == 0 and s % tk == 0 and tq == LANES
    n_kb_max = s // tk
    cols = ATTN_GROUP * tq
    vt = v.reshape(b, n_kb_max, tk, ATTN_KV_DIM).transpose(0, 1, 3, 2)
    kern = functools.partial(_attn_kernel, tq=tq, tk=tk, q_pos0=q_pos0, k_top=k_top, n_kb_max=n_kb_max)
    resident = dict(pipeline_mode=pl.Buffered(1))
    out_spec, out_shape, buf_specs, bufs = _rows_out_spec(
        ATTN_Q_DIM, n_q // tq, (b, tq, out_rows or n_q), into, BF16)
    return pl.pallas_call(
        kern,
        grid=(b, n_q // tq),
        in_specs=[
            pl.BlockSpec((1, ATTN_HEADS, tq, ATTN_HEAD_DIM), lambda bi, i: (bi, 0, i, 0)),
            pl.BlockSpec((1, IDX_HEADS, tq, IDX_HEAD_DIM), lambda bi, i: (bi, 0, i, 0)),
            pl.BlockSpec((1, tq, LANES), lambda bi, i: (bi, i, 0)),
            pl.BlockSpec((1, s, ATTN_KV_DIM), lambda bi, i: (bi, 0, 0), **resident),
            pl.BlockSpec((1, n_kb_max, ATTN_KV_DIM, tk), lambda bi, i: (bi, 0, 0, 0), **resident),
            pl.BlockSpec((1, s, IDX_HEAD_DIM), lambda bi, i: (bi, 0, 0), **resident),
        ] + buf_specs,
        out_specs=out_spec,
        out_shape=out_shape,
        input_output_aliases={6: 0} if bufs else {},
        scratch_shapes=[
            pltpu.VMEM((n_kb_max, tk, tq), jnp.int32),
            pltpu.VMEM((ATTN_KV_HEADS, 1, cols), F32),
            pltpu.VMEM((ATTN_KV_HEADS, 1, cols), F32),
            pltpu.VMEM((ATTN_KV_HEADS, ATTN_HEAD_DIM, cols), F32),
            pltpu.VMEM((1, tq), F32),
            pltpu.VMEM((ATTN_KV_HEADS, tk, cols), BF16),
        ],
        compiler_params=_params("parallel", "arbitrary"),
    )(q, qi, kx, k, vt, ki, *bufs)


_LEVELS = (32, 16, 8, 4, 2, 1)


def _hgrn_tables():
    c = CHUNK
    t = np.arange(c)[:, None]
    u = np.arange(c)[None, :]
    mats = [(u <= t), (u > t)]
    masks = []
    for w in _LEVELS:
        r = (t // (2 * w)) * (2 * w) + w - 1
        upper = (t % (2 * w)) >= w
        mats.append((upper & (u > r) & (u <= t)) | ((~upper) & (u > t) & (u <= r)))
        s = np.arange(c)[None, :]
        masks.append(((t // (2 * w)) == (s // (2 * w))) & upper & ((s % (2 * w)) < w))
    masks.append(t == np.arange(c)[None, :])
    table = np.concatenate(mats, axis=0).astype(np.float32)
    return np.concatenate([table] * 3, axis=1), np.stack(masks).astype(np.float32)


def _hgrn_kernel(q_ref, f_ref, i_ref, gt_ref, lb_ref, gn_ref, tab_ref, msk_ref, s0_ref, *rest, layer):
    o_ref, sfin_ref, st_ref = rest[-3:]
    c = CHUNK
    dh = HGRN_HEAD_DIM
    step = pl.program_id(1)

    @pl.when(step == 0)
    def _():
        for h in range(HGRN_HEADS):
            st_ref[h] = s0_ref[0, h].T

    lb_all = lb_ref[...]
    e = jnp.exp(lb_all - jnp.max(lb_all, axis=0, keepdims=True))
    p_lb = e / jnp.sum(e, axis=0, keepdims=True)
    lower = jnp.sum(p_lb[:layer + 1], axis=0, keepdims=True) - p_lb[0:1]

    def chunk_operands(ci):
        rows = slice(ci * c, (ci + 1) * c)
        f = lower + (1.0 - lower) * jax.nn.sigmoid(f_ref[0, rows])
        kk = 1.0 - f
        lf = jnp.log2(f)
        lf_hi = lf.astype(BF16)
        r1 = lf - lf_hi.astype(F32)
        lf_mid = r1.astype(BF16)
        lf_lo = (r1 - lf_mid.astype(F32)).astype(BF16)
        lf3 = jnp.concatenate([lf_hi, lf_mid, lf_lo], axis=0)
        ex = jnp.exp2(jnp.dot(tab_ref[...], lf3, preferred_element_type=F32))
        qq = q_ref[0, rows]
        vv = i_ref[0, rows]
        ops = dict(q16=qq.astype(BF16), k16=kk.astype(BF16), v16=vv.astype(BF16), vv=vv)
        ops["q_lv"] = [(qq * ex[(2 + li) * c:(3 + li) * c]).astype(BF16) for li in range(len(_LEVELS))]
        ops["k_lv"] = [(kk * ex[(2 + li) * c:(3 + li) * c]).astype(BF16) for li in range(len(_LEVELS))]
        ops["q_in"] = (qq * ex[0:c]).astype(BF16)
        ops["k_out"] = (kk * ex[c:2 * c]).astype(BF16)
        ops["decay"] = ex[c - 1:c]
        ops["gate"] = jax.nn.sigmoid(gt_ref[0, rows])
        return ops

    def chunk_outputs(ci, ops):
        rows = slice(ci * c, (ci + 1) * c)
        scores = []
        for h in range(HGRN_HEADS):
            sl = slice(h * dh, (h + 1) * dh)
            s_h = msk_ref[len(_LEVELS)] * lax.dot_general(
                ops["q16"][:, sl], ops["k16"][:, sl], _NT, preferred_element_type=F32)
            for li in range(len(_LEVELS)):
                s_h = s_h + msk_ref[li] * lax.dot_general(
                    ops["q_lv"][li][:, sl], ops["k_lv"][li][:, sl], _NT, preferred_element_type=F32)
            scores.append(s_h.astype(BF16))
        for h in range(HGRN_HEADS):
            sl = slice(h * dh, (h + 1) * dh)
            st = st_ref[h]
            o_h = jnp.dot(scores[h], ops["v16"][:, sl], preferred_element_type=F32)
            o_h = o_h + lax.dot_general(ops["q_in"][:, sl], st.astype(BF16), _NT, preferred_element_type=F32)
            st_ref[h] = st * ops["decay"][:, sl] + jnp.dot(
                ops["vv"][:, sl].T.astype(BF16), ops["k_out"][:, sl], preferred_element_type=F32)
            o_ref[0, rows, sl] = (_rms(o_h, gn_ref[:, sl]) * ops["gate"][:, sl]).astype(o_ref.dtype)

    n_sub = q_ref.shape[1] // c
    ops = chunk_operands(0)
    for ci in range(n_sub):
        ops_next = chunk_operands(ci + 1) if ci + 1 < n_sub else None
        chunk_outputs(ci, ops)
        ops = ops_next

    @pl.when(step == pl.num_programs(1) - 1)
    def _():
        for h in range(HGRN_HEADS):
            sfin_ref[0, h] = st_ref[h].T


def hgrn_scan(p, lb, gnorm, s0, *, layer, n_rows=None, out_rows=None, into=None):
    b = p.shape[0]
    l = p.shape[1] if n_rows is None else n_rows
    d = D_MODEL
    c = max(r for r in (4 * CHUNK, 2 * CHUNK, CHUNK) if l % r == 0)
    tab, msk = _hgrn_tables()
    kern = functools.partial(_hgrn_kernel, layer=layer)
    col = lambda j: pl.BlockSpec((1, c, d), lambda bi, t, j=j: (bi, t, j))
    state_spec = pl.BlockSpec((1, HGRN_HEADS, HGRN_HEAD_DIM, HGRN_HEAD_DIM), lambda bi, t: (bi, 0, 0, 0))
    out_spec, out_shape, buf_specs, bufs = _rows_out_spec(d, l // c, (b, c, out_rows or l), into, BF16)
    return pl.pallas_call(
        kern,
        grid=(b, l // c),
        in_specs=[
            col(0), col(1), col(2), col(3),
            pl.BlockSpec(lb.shape, lambda bi, t: (0, 0)),
            pl.BlockSpec((1, d), lambda bi, t: (0, 0)),
            pl.BlockSpec(tab.shape, lambda bi, t: (0, 0)),
            pl.BlockSpec(msk.shape, lambda bi, t: (0, 0, 0)),
            state_spec,
        ] + buf_specs,
        out_specs=[out_spec, state_spec],
        out_shape=[out_shape, jax.ShapeDtypeStruct(s0.shape, F32)],
        input_output_aliases={9: 0} if bufs else {},
        scratch_shapes=[pltpu.VMEM((HGRN_HEADS, HGRN_HEAD_DIM, HGRN_HEAD_DIM), F32)],
        compiler_params=_params("parallel", "arbitrary"),
    )(p, p, p, p, lb, gnorm.reshape(1, d), jnp.asarray(tab, BF16), jnp.asarray(msk), s0, *bufs)


def _attn_layer(x, g, w_in, w_out, layer, cache_k, cache_v, cache_kidx, n_p, b_s, t_s):
    q, kv, qi, kx = attn_in_proj(x, g, w_in, layer=layer)

    k_new, v_new, ki_new = kv[:, :ATTN_KV_DIM], kv[:, ATTN_KV_DIM:], kx[:, :IDX_HEAD_DIM]
    past = cache_k.shape[1]
    tq = LANES

    o_p = dsa_attention(
        q[None], qi[None], kx[None],
        k_new[:n_p].astype(BF16)[None], v_new[:n_p].astype(BF16)[None], ki_new[:n_p].astype(BF16)[None],
        n_q=n_p, tq=tq, tk=512, q_pos0=0, k_top=min(TOPK_MAX, n_p // 4),
        out_rows=x.shape[0])

    tk_s = 256
    s_all = past + t_s
    n_adm_pad = ((past + tq - 1) // CHUNK + 1) * CHUNK
    s_pad = -(-max(s_all, n_adm_pad) // tk_s) * tk_s

    def with_cache(cache, new, width):
        full = jnp.zeros((b_s, s_pad, width), BF16)
        full = lax.dynamic_update_slice(full, cache.reshape(b_s, past, width).astype(BF16), (0, 0, 0))
        return lax.dynamic_update_slice(full, new[n_p:].astype(BF16).reshape(b_s, t_s, width), (0, past, 0))

    def sample_heads(a, heads, dim):
        a = a[:, n_p:].reshape(heads, b_s, t_s, dim).transpose(1, 0, 2, 3)
        return jnp.pad(a, ((0, 0), (0, 0), (0, tq - t_s), (0, 0)))

    kx_s = jnp.pad(kx[n_p:].reshape(b_s, t_s, LANES), ((0, 0), (0, tq - t_s), (0, 0)))
    o_s = dsa_attention(
        sample_heads(q, ATTN_HEADS, ATTN_HEAD_DIM), sample_heads(qi, IDX_HEADS, IDX_HEAD_DIM), kx_s,
        with_cache(cache_k, k_new, ATTN_KV_DIM), with_cache(cache_v, v_new, ATTN_KV_DIM),
        with_cache(cache_kidx, ki_new, IDX_HEAD_DIM),
        n_q=tq, tq=tq, tk=tk_s, q_pos0=past, k_top=min(TOPK_MAX, s_all // 4),
        into=(o_p, n_p, t_s))

    x = matmul_residual(o_s[0], w_out, x, layer=layer)
    return x, k_new, v_new, ki_new


def _rec_layer(x, g, w_in, w_out, gnorm, rec_lb, state, layer, n_p, b_s, t_s):
    p = norm_matmul(x, g, w_in, layer=layer)
    s0_p = jnp.zeros((1,) + state.shape[1:], F32)
    o_p, s_p = hgrn_scan(p[None], rec_lb, gnorm, s0_p, layer=layer, n_rows=n_p, out_rows=x.shape[0])
    o_s, s_s = hgrn_scan(p[n_p:].reshape(b_s, t_s, -1), rec_lb, gnorm, state, layer=layer,
                         into=(o_p, n_p, t_s))
    x = matmul_residual(o_s[0], w_out, x, layer=layer)
    return x, s_p, s_s


def kernel(x_prompt, x_sample, cache_k, cache_v, cache_kidx, state_s, norm_mix, norm_mlp, norm_final,
           attn_w_in, attn_w_out, rec_w_in, rec_w_out, rec_gnorm, rec_lb, mlp_w_up, mlp_w_down):
    b_p, l_p, d = x_prompt.shape
    b_s, t_s, _ = x_sample.shape
    assert b_p == 1
    n_p = b_p * l_p
    x = jnp.concatenate([x_prompt.reshape(n_p, d), x_sample.reshape(b_s * t_s, d)], axis=0)

    kp, vp, kip, sp = [], [], [], []
    ks, vs, kis, ss = [], [], [], []
    for layer in range(DEPTH):
        j = layer // N_MIXERS
        if layer % N_MIXERS == 0:
            x, k_new, v_new, ki_new = _attn_layer(
                x, norm_mix[layer], attn_w_in, attn_w_out, j,
                cache_k[j], cache_v[j], cache_kidx[j], n_p, b_s, t_s)
            kp.append(k_new[:n_p].reshape(b_p, l_p, ATTN_KV_HEADS, ATTN_HEAD_DIM))
            vp.append(v_new[:n_p].reshape(b_p, l_p, ATTN_KV_HEADS, ATTN_HEAD_DIM))
            kip.append(ki_new[:n_p].reshape(b_p, l_p, IDX_HEAD_DIM))
            ks.append(k_new[n_p:].reshape(b_s, t_s, ATTN_KV_HEADS, ATTN_HEAD_DIM))
            vs.append(v_new[n_p:].reshape(b_s, t_s, ATTN_KV_HEADS, ATTN_HEAD_DIM))
            kis.append(ki_new[n_p:].reshape(b_s, t_s, IDX_HEAD_DIM))
        else:
            x, s_p, s_s = _rec_layer(
                x, norm_mix[layer], rec_w_in, rec_w_out, rec_gnorm[j], rec_lb, state_s[j],
                j, n_p, b_s, t_s)
            sp.append(s_p)
            ss.append(s_s)
        x = mlp_residual(x, norm_mlp[layer], mlp_w_up, mlp_w_down, layer=layer,
                         g_final=norm_final if layer == DEPTH - 1 else None)

    y = x
    return (y[:n_p].reshape(b_p, l_p, d), y[n_p:].reshape(b_s, t_s, d),
            jnp.stack(kp), jnp.stack(vp), jnp.stack(kip), jnp.stack(sp),
            jnp.stack(ks), jnp.stack(vs), jnp.stack(kis), jnp.stack(ss))
```

```python
import functools

import numpy as np
import jax
import jax.numpy as jnp
from jax import lax
from jax.experimental import pallas as pl
from jax.experimental.pallas import tpu as pltpu

D_MODEL = 2048
DEPTH = 4
CHUNK = 64
N_MIXERS = 2
ATTN_HEADS = 16
ATTN_KV_HEADS = 4
ATTN_HEAD_DIM = D_MODEL // ATTN_HEADS
ATTN_GROUP = ATTN_HEADS // ATTN_KV_HEADS
IDX_HEADS = 16
IDX_HEAD_DIM = 64
TOPK_MAX = 256
HGRN_HEAD_DIM = 128
HGRN_HEADS = D_MODEL // HGRN_HEAD_DIM
D_FF = 4 * D_MODEL
RMS_EPS = 1e-6
ATTN_Q_DIM = ATTN_HEADS * ATTN_HEAD_DIM
ATTN_KV_DIM = ATTN_KV_HEADS * ATTN_HEAD_DIM
IDX_Q_DIM = IDX_HEADS * IDX_HEAD_DIM

LANES = 128
SUBLANES = 8
VMEM_LIMIT = 56 * 1024 * 1024

F32 = jnp.float32
BF16 = jnp.bfloat16
INT_MIN = np.int32(-(2 ** 31))
NEG_BIG = -1e30
LOG2_E = 1.4426950408889634

_NT = (((1,), (1,)), ((), ()))


def _params(*sem):
    return pltpu.CompilerParams(dimension_semantics=sem, vmem_limit_bytes=VMEM_LIMIT)


def _rms(x, g):
    ms = jnp.mean(x * x, axis=-1, keepdims=True)
    return (x * lax.rsqrt(ms + RMS_EPS)) * g


def _norm_matmul_kernel(x_ref, g_ref, w_ref, o_ref, h_ref):
    @pl.when(pl.program_id(1) == 0)
    def _():
        h_ref[...] = _rms(x_ref[...], g_ref[...]).astype(BF16)

    o_ref[...] = jnp.dot(h_ref[...], w_ref[...].astype(BF16), preferred_element_type=F32)


def _row_tile(m, cap=1088):
    return max(t for t in range(16, cap + 1, 16) if m % t == 0)


_ONE_BUFFER = dict(pipeline_mode=pl.Buffered(1))


def norm_matmul(x, g, w, *, layer, tn=512):
    m, d = x.shape
    n = w.shape[2]
    tm = _row_tile(m)
    assert n % tn == 0
    return pl.pallas_call(
        _norm_matmul_kernel,
        grid=(m // tm, n // tn),
        in_specs=[
            pl.BlockSpec((tm, d), lambda i, j: (i, 0), **_ONE_BUFFER),
            pl.BlockSpec((1, d), lambda i, j: (0, 0)),
            pl.BlockSpec((None, d, tn), lambda i, j: (layer, 0, j)),
        ],
        out_specs=pl.BlockSpec((tm, tn), lambda i, j: (i, j)),
        out_shape=jax.ShapeDtypeStruct((m, n), F32),
        scratch_shapes=[pltpu.VMEM((tm, d), BF16)],
        compiler_params=_params("parallel", "arbitrary"),
    )(x, g.reshape(1, d), w)


def _attn_in_proj_kernel(x_ref, g_ref, w_ref, wkx_ref, *rest, n_q, n_qi, n_prompt_tiles):
    (q_ref, qi_ref, kx_ref, kp_ref, vp_ref, kip_ref, ks_ref, vs_ref, kis_ref,
     k16_ref, vt_ref, ki16_ref, h_ref) = rest[-13:]
    i = pl.program_id(0)
    j = pl.program_id(1)
    is_prompt = i < n_prompt_tiles
    is_sample = jnp.logical_not(is_prompt)

    @pl.when(j == 0)
    def _():
        h_ref[...] = _rms(x_ref[...], g_ref[...]).astype(BF16)

    def tile():
        return jnp.dot(h_ref[...], w_ref[...].astype(BF16), preferred_element_type=F32)

    @pl.when(j < n_q)
    def _():
        res = tile()
        for hh in range(q_ref.shape[0]):
            q_ref[hh] = res[:, hh * ATTN_HEAD_DIM:(hh + 1) * ATTN_HEAD_DIM].astype(q_ref.dtype)

    @pl.when(jnp.logical_and(j == n_q, is_prompt))
    def _():
        res = tile()
        kp_ref[...] = res
        k16_ref[...] = res.astype(BF16)

    @pl.when(jnp.logical_and(j == n_q, is_sample))
    def _():
        ks_ref[...] = tile()

    @pl.when(jnp.logical_and(j == n_q + 1, is_prompt))
    def _():
        res = tile()
        vp_ref[...] = res
        vt_ref[0] = res.T.astype(BF16)

    @pl.when(jnp.logical_and(j == n_q + 1, is_sample))
    def _():
        vs_ref[...] = tile()

    @pl.when(jnp.logical_and(j >= n_q + 2, j < n_q + 2 + n_qi))
    def _():
        res = tile()
        for hh in range(qi_ref.shape[0]):
            qi_ref[hh] = res[:, hh * IDX_HEAD_DIM:(hh + 1) * IDX_HEAD_DIM].astype(qi_ref.dtype)

    @pl.when(j == n_q + 2 + n_qi)
    def _():
        res = jnp.dot(h_ref[...], wkx_ref[...].astype(BF16), preferred_element_type=F32)
        kx_ref[...] = res

        @pl.when(is_prompt)
        def _():
            kip_ref[...] = res[:, :IDX_HEAD_DIM]
            ki16_ref[...] = res[:, :IDX_HEAD_DIM].astype(BF16)

        @pl.when(is_sample)
        def _():
            kis_ref[...] = res[:, :IDX_HEAD_DIM]


def attn_in_proj(x, g, w_in, *, layer, n_prompt, new_kv=None):
    m, d = x.shape
    tm = tn = ATTN_KV_DIM
    n_layers = w_in.shape[0]
    n_sample = m - n_prompt
    assert n_prompt % tm == 0 and n_sample % tm == 0 and n_sample > 0
    npt, nst = n_prompt // tm, n_sample // tm
    n_q, n_qi = ATTN_Q_DIM // tn, IDX_Q_DIM // tn
    n_main = n_q + 2 + n_qi
    tail = w_in.shape[2] - n_main * tn
    w_kx = jnp.pad(w_in[layer:layer + 1, :, n_main * tn:], ((0, 0), (0, 0), (0, LANES - tail)))
    clip = lambda v, hi: jnp.minimum(jnp.maximum(v, 0), hi)
    p_row = lambda i: jnp.minimum(i, npt - 1)
    s_row = lambda i: clip(i - npt, nst - 1)
    slab = lambda rows, width: jax.ShapeDtypeStruct((n_layers, rows, width), F32)
    kv_shapes = [slab(n_prompt, tn), slab(n_prompt, tn), slab(n_prompt, IDX_HEAD_DIM),
                 slab(n_sample, tn), slab(n_sample, tn), slab(n_sample, IDX_HEAD_DIM)]
    bufs = [] if new_kv is None else list(new_kv)
    outs = pl.pallas_call(
        functools.partial(_attn_in_proj_kernel, n_q=n_q, n_qi=n_qi, n_prompt_tiles=npt),
        grid=(m // tm, n_main + 1),
        in_specs=[
            pl.BlockSpec((tm, d), lambda i, j: (i, 0), **_ONE_BUFFER),
            pl.BlockSpec((1, d), lambda i, j: (0, 0)),
            pl.BlockSpec((None, d, tn), lambda i, j: (layer, 0, jnp.minimum(j, n_main - 1))),
            pl.BlockSpec((None, d, LANES), lambda i, j: (0, 0, 0)),
        ] + [pl.BlockSpec(memory_space=pl.ANY)] * len(bufs),
        out_specs=[
            pl.BlockSpec((tn // ATTN_HEAD_DIM, tm, ATTN_HEAD_DIM), lambda i, j: (clip(j, n_q - 1), i, 0)),
            pl.BlockSpec((tn // IDX_HEAD_DIM, tm, IDX_HEAD_DIM), lambda i, j: (clip(j - n_q - 2, n_qi - 1), i, 0)),
            pl.BlockSpec((tm, LANES), lambda i, j: (i, 0)),
            pl.BlockSpec((None, tm, tn), lambda i, j: (layer, p_row(i), 0)),
            pl.BlockSpec((None, tm, tn), lambda i, j: (layer, p_row(i), 0)),
            pl.BlockSpec((None, tm, IDX_HEAD_DIM), lambda i, j: (layer, p_row(i), 0)),
            pl.BlockSpec((None, tm, tn), lambda i, j: (layer, s_row(i), 0)),
            pl.BlockSpec((None, tm, tn), lambda i, j: (layer, s_row(i), 0)),
            pl.BlockSpec((None, tm, IDX_HEAD_DIM), lambda i, j: (layer, s_row(i), 0)),
            pl.BlockSpec((tm, tn), lambda i, j: (p_row(i), 0)),
            pl.BlockSpec((1, tn, tm), lambda i, j: (p_row(i), 0, 0)),
            pl.BlockSpec((tm, IDX_HEAD_DIM), lambda i, j: (p_row(i), 0)),
        ],
        out_shape=[
            jax.ShapeDtypeStruct((ATTN_HEADS, m, ATTN_HEAD_DIM), BF16),
            jax.ShapeDtypeStruct((IDX_HEADS, m, IDX_HEAD_DIM), BF16),
            jax.ShapeDtypeStruct((m, LANES), F32),
        ] + kv_shapes + [
            jax.ShapeDtypeStruct((n_prompt, tn), BF16),
            jax.ShapeDtypeStruct((npt, tn, tm), BF16),
            jax.ShapeDtypeStruct((n_prompt, IDX_HEAD_DIM), BF16),
        ],
        input_output_aliases={4 + n: 3 + n for n in range(len(bufs))},
        scratch_shapes=[pltpu.VMEM((tm, d), BF16)],
        compiler_params=_params("parallel", "arbitrary"),
    )(x, g.reshape(1, d), w_in, w_kx, *bufs)
    q, qi, kx = outs[:3]
    return q, qi, kx, tuple(outs[3:9]), outs[9], outs[10], outs[11]


def _matmul_res_kernel(a_ref, w_ref, r_ref, o_ref):
    o_ref[...] = r_ref[...] + jnp.dot(a_ref[...], w_ref[...].astype(BF16), preferred_element_type=F32)


def matmul_residual(a, w, res, *, layer, tn=512):
    m, k = a.shape
    n = w.shape[2]
    tm = _row_tile(m)
    assert n % tn == 0
    return pl.pallas_call(
        _matmul_res_kernel,
        grid=(m // tm, n // tn),
        in_specs=[
            pl.BlockSpec((tm, k), lambda i, j: (i, 0)),
            pl.BlockSpec((None, k, tn), lambda i, j: (layer, 0, j)),
            pl.BlockSpec((tm, tn), lambda i, j: (i, j)),
        ],
        out_specs=pl.BlockSpec((tm, tn), lambda i, j: (i, j)),
        out_shape=jax.ShapeDtypeStruct((m, n), F32),
        compiler_params=_params("parallel", "parallel"),
    )(a, w, res)


def _mlp_kernel(x_ref, g_ref, wu_ref, wd_ref, gf_ref, o_ref, h_ref, *, final_norm):
    @pl.when(pl.program_id(1) == 0)
    def _():
        x = x_ref[...]
        h_ref[...] = _rms(x, g_ref[...]).astype(BF16)
        o_ref[...] = x

    u = jnp.maximum(jnp.dot(h_ref[...], wu_ref[...].astype(BF16), preferred_element_type=F32), 0.0)
    o_ref[...] += jnp.dot((u * u).astype(BF16), wd_ref[...].astype(BF16), preferred_element_type=F32)

    if final_norm:
        @pl.when(pl.program_id(1) == pl.num_programs(1) - 1)
        def _():
            o_ref[...] = _rms(o_ref[...], gf_ref[...])


def mlp_residual(x, g, w_up, w_down, *, layer, g_final=None, tf=512):
    m, d = x.shape
    f = w_up.shape[2]
    tm = _row_tile(m)
    assert f % tf == 0
    final_norm = g_final is not None
    gf = (g_final if final_norm else g).reshape(1, d)
    return pl.pallas_call(
        functools.partial(_mlp_kernel, final_norm=final_norm),
        grid=(m // tm, f // tf),
        in_specs=[
            pl.BlockSpec((tm, d), lambda i, j: (i, 0), **_ONE_BUFFER),
            pl.BlockSpec((1, d), lambda i, j: (0, 0)),
            pl.BlockSpec((None, d, tf), lambda i, j: (layer, 0, j)),
            pl.BlockSpec((None, tf, d), lambda i, j: (layer, j, 0)),
            pl.BlockSpec((1, d), lambda i, j: (0, 0)),
        ],
        out_specs=pl.BlockSpec((tm, d), lambda i, j: (i, 0), **_ONE_BUFFER),
        out_shape=jax.ShapeDtypeStruct((m, d), F32),
        scratch_shapes=[pltpu.VMEM((tm, d), BF16)],
        compiler_params=_params("parallel", "arbitrary"),
    )(x, g.reshape(1, d), w_up, w_down, gf)


def _attn_kernel(q_ref, qi_ref, kx_ref, k_ref, vt_ref, ki_ref, *rest, tq, tk, q_pos0, k_top, n_kb_max):
    o_ref, key_ref, m_ref, l_ref, acc_ref, kn_ref, p_ref = rest[-7:]
    @pl.when(pl.program_id(1) == 0)
    def _():
        def kn_body(j, best):
            kf = k_ref[0, pl.ds(pl.multiple_of(j * tk, tk), tk), :].astype(F32)
            sq = kf * kf
            for g in range(ATTN_KV_HEADS):
                n2 = jnp.sum(sq[:, g * ATTN_HEAD_DIM:(g + 1) * ATTN_HEAD_DIM], axis=1, keepdims=True)
                best = jnp.maximum(best, jnp.max(n2, axis=0, keepdims=True))
            return best

        kn_ref[...] = jnp.broadcast_to(lax.fori_loop(0, n_kb_max, kn_body, jnp.zeros((1, 1), F32)), kn_ref.shape)

    q_start = q_pos0 + pl.program_id(1) * tq
    n_adm = ((q_start + tq - 1) // CHUNK + 1) * CHUNK
    n_kb = jnp.minimum((n_adm + tk - 1) // tk, n_kb_max)

    k_row = lax.broadcasted_iota(jnp.int32, (tk, tq), 0)
    k_lim = ((q_start + lax.broadcasted_iota(jnp.int32, (tk, tq), 1)) // CHUNK + 1) * CHUNK

    w_t = kx_ref[0].T[IDX_HEAD_DIM:IDX_HEAD_DIM + IDX_HEADS, :] * (IDX_HEADS ** -0.5 * IDX_HEAD_DIM ** -0.5)
    qi = qi_ref[0].reshape(IDX_HEADS * tq, IDX_HEAD_DIM)

    def score_body(j, carry):
        kib = ki_ref[0, pl.ds(pl.multiple_of(j * tk, tk), tk), :]
        d = lax.dot_general(kib, qi, _NT, preferred_element_type=F32)
        sc = jnp.zeros((tk, tq), F32)
        for h in range(IDX_HEADS):
            sc = sc + w_t[h:h + 1, :] * jnp.maximum(d[:, h * tq:(h + 1) * tq], 0.0)
        bits = pltpu.bitcast(sc, jnp.int32)
        key = bits ^ ((bits >> 31) & jnp.int32(0x7FFFFFFF))
        key_ref[j] = jnp.where(k_row < k_lim - j * tk, key, INT_MIN)
        return carry

    lax.fori_loop(0, n_kb, score_body, 0)

    rows_acc = 8 * SUBLANES

    def count_where(pred):
        def body(j, cnt):
            blk = key_ref[j]
            hit = pred(blk, j)
            for r in range(tk // rows_acc):
                cnt = jnp.where(hit[r * rows_acc:(r + 1) * rows_acc], cnt + 1.0, cnt)
            return cnt
        cnt = lax.fori_loop(0, n_kb, body, jnp.zeros((rows_acc, tq), F32))
        return jnp.sum(cnt, axis=0, keepdims=True)

    def bit_body(p, state):
        thr, cnt_thr = state
        cand = thr + jnp.left_shift(jnp.int32(1), 31 - p)
        cnt = count_where(lambda blk, j: blk >= cand)
        keep = cnt >= float(k_top)
        return jnp.where(keep, cand, thr), jnp.where(keep, cnt, cnt_thr)

    def search_bits(lo, hi, state):
        return lax.fori_loop(lo, hi, bit_body, state)

    def settled(state):
        return jnp.min(jnp.where(state[1] == float(k_top), 1.0, 0.0)) > 0.0

    state = search_bits(0, 22, (jnp.full((1, tq), INT_MIN, jnp.int32), jnp.full((1, tq), 2.0 ** 30, F32)))
    for lo in range(22, 32, 2):
        state = lax.cond(settled(state), lambda s: s, functools.partial(search_bits, lo, lo + 2), state)
    thr_raw, cnt_thr = state
    thr = jnp.maximum(thr_raw, INT_MIN + 1)

    tied = jnp.logical_and(cnt_thr > float(k_top), thr_raw > INT_MIN)
    n_tied = jnp.max(jnp.where(tied, 1.0, 0.0))

    @pl.when(n_tied > 0.0)
    def _():
        need = float(k_top) - count_where(lambda blk, j: blk > thr)
        n_bits = max(1, int(n_kb_max * tk - 1).bit_length())

        def pos_body(b, last):
            step = jnp.left_shift(jnp.int32(1), n_bits - 1 - b)
            cand = last + step - 1
            got = count_where(lambda blk, j: jnp.logical_and(blk == thr, k_row + j * tk <= cand))
            return jnp.where(got < need, last + step, last)

        last = lax.fori_loop(0, n_bits, pos_body, jnp.zeros((1, tq), jnp.int32))

        def demote_body(j, carry):
            blk = key_ref[j]
            drop = jnp.logical_and(blk == thr, k_row + j * tk > last)
            key_ref[j] = jnp.where(drop, thr - 1, blk)
            return carry

        lax.fori_loop(0, n_kb, demote_body, 0)

    m_ref[...] = jnp.full(m_ref.shape, NEG_BIG, F32)
    l_ref[...] = jnp.zeros(l_ref.shape, F32)
    acc_ref[...] = jnp.zeros(acc_ref.shape, F32)
    c_exp = ATTN_HEAD_DIM ** -0.5 * LOG2_E
    cols = ATTN_GROUP * tq

    def logits_t(j, g):
        k0 = pl.multiple_of(j * tk, tk)
        kb = k_ref[0, pl.ds(k0, tk), g * ATTN_HEAD_DIM:(g + 1) * ATTN_HEAD_DIM]
        vtb = vt_ref[0, j, g * ATTN_HEAD_DIM:(g + 1) * ATTN_HEAD_DIM, :]
        qg = q_ref[0, g * ATTN_GROUP:(g + 1) * ATTN_GROUP].reshape(cols, ATTN_HEAD_DIM)
        return lax.dot_general(kb, qg, _NT, preferred_element_type=F32), vtb

    qf = q_ref[0].reshape(ATTN_HEADS * tq, ATTN_HEAD_DIM).astype(F32)
    qn2 = lax.dot_general(jnp.ones((SUBLANES, ATTN_HEAD_DIM), BF16), (qf * qf).astype(BF16), _NT,
                          preferred_element_type=F32)
    qn2_max = qn2[0:1, 0:tq]
    for h in range(1, ATTN_HEADS):
        qn2_max = jnp.maximum(qn2_max, qn2[0:1, h * tq:(h + 1) * tq])
    m_bound = jnp.sqrt(qn2_max * kn_ref[...]) * 1.02
    fixed_shift_ok = jnp.max(m_bound) * (2.0 * c_exp) < 100.0

    @pl.when(fixed_shift_ok)
    def _():
        shift = -c_exp * m_bound
        ones = jnp.ones((2 * SUBLANES, tk), BF16)

        def accumulate(j):
            for g in range(ATTN_KV_HEADS):
                vtb = vt_ref[0, j, g * ATTN_HEAD_DIM:(g + 1) * ATTN_HEAD_DIM, :]
                acc_ref[g] += jnp.dot(vtb, p_ref[g], preferred_element_type=F32)
                l_ref[g] += jnp.dot(ones, p_ref[g], preferred_element_type=F32)[0:1]

        def probabilities(j):
            bias = jnp.where(key_ref[j] >= thr, shift, NEG_BIG)
            bias = jnp.concatenate([bias] * ATTN_GROUP, axis=1)
            for g in range(ATTN_KV_HEADS):
                s, _ = logits_t(j, g)
                p_ref[g] = jnp.exp2(s * c_exp + bias).astype(BF16)

        probabilities(0)

        def body(j, carry):
            accumulate(j - 1)
            probabilities(j)
            return carry

        lax.fori_loop(1, n_kb, body, 0)
        accumulate(n_kb - 1)

    @pl.when(jnp.logical_not(fixed_shift_ok))
    def _():
        def body(j, carry):
            bias = jnp.where(key_ref[j] >= thr, 0.0, NEG_BIG)
            bias = jnp.concatenate([bias] * ATTN_GROUP, axis=1)
            for g in range(ATTN_KV_HEADS):
                s, vtb = logits_t(j, g)
                s = s + bias
                m_old = m_ref[g]
                m_new = jnp.maximum(m_old, jnp.max(s, axis=0, keepdims=True))
                alpha = jnp.exp2((m_old - m_new) * c_exp)
                p = jnp.exp2((s - m_new) * c_exp)
                l_ref[g] = alpha * l_ref[g] + jnp.sum(p, axis=0, keepdims=True)
                acc_ref[g] = alpha * acc_ref[g] + jnp.dot(vtb, p.astype(BF16), preferred_element_type=F32)
                m_ref[g] = m_new
            return carry

        lax.fori_loop(0, n_kb, body, 0)

    n_out = o_ref.shape[1]
    for g in range(ATTN_KV_HEADS):
        out_t = acc_ref[g] / l_ref[g]
        for hh in range(ATTN_GROUP):
            h = g * ATTN_GROUP + hh
            o_ref[0, :, h * ATTN_HEAD_DIM:(h + 1) * ATTN_HEAD_DIM] = (
                out_t[:, hh * tq:(hh + 1) * tq].T[:n_out].astype(o_ref.dtype))


def _rows_out_spec(width, n_steps, out_rows, into, dtype):
    b, rows = out_rows[0], out_rows[1]
    if into is None:
        spec = pl.BlockSpec((1, rows, width), lambda bi, i: (bi, i, 0))
        return spec, jax.ShapeDtypeStruct((b, out_rows[2], width), dtype), [], []
    buf, first_row, rows = into
    assert first_row % rows == 0 and buf.shape[0] == 1 and buf.dtype == dtype
    blk0 = first_row // rows
    spec = pl.BlockSpec((1, rows, width), lambda bi, i: (0, blk0 + bi * n_steps + i, 0))
    return spec, jax.ShapeDtypeStruct(buf.shape, dtype), [pl.BlockSpec(memory_space=pl.ANY)], [buf]


def dsa_attention(q, qi, kx, k, v, ki, *, n_q, tq, tk, q_pos0, k_top, v_is_transposed=False,
                  out_rows=None, into=None):
    b = q.shape[0]
    s = k.shape[1]
    assert n_q % tq == 0 and s % tk == 0 and tq == LANES
    n_kb_max = s // tk
    cols = ATTN_GROUP * tq
    vt = v if v_is_transposed else v.reshape(b, n_kb_max, tk, ATTN_KV_DIM).transpose(0, 1, 3, 2)
    assert vt.shape == (b, n_kb_max, ATTN_KV_DIM, tk)
    kern = functools.partial(_attn_kernel, tq=tq, tk=tk, q_pos0=q_pos0, k_top=k_top, n_kb_max=n_kb_max)
    resident = dict(pipeline_mode=pl.Buffered(1))
    out_spec, out_shape, buf_specs, bufs = _rows_out_spec(
        ATTN_Q_DIM, n_q // tq, (b, tq, out_rows or n_q), into, BF16)
    return pl.pallas_call(
        kern,
        grid=(b, n_q // tq),
        in_specs=[
            pl.BlockSpec((1, ATTN_HEADS, tq, ATTN_HEAD_DIM), lambda bi, i: (bi, 0, i, 0)),
            pl.BlockSpec((1, IDX_HEADS, tq, IDX_HEAD_DIM), lambda bi, i: (bi, 0, i, 0)),
            pl.BlockSpec((1, tq, LANES), lambda bi, i: (bi, i, 0)),
            pl.BlockSpec((1, s, ATTN_KV_DIM), lambda bi, i: (bi, 0, 0), **resident),
            pl.BlockSpec((1, n_kb_max, ATTN_KV_DIM, tk), lambda bi, i: (bi, 0, 0, 0), **resident),
            pl.BlockSpec((1, s, IDX_HEAD_DIM), lambda bi, i: (bi, 0, 0), **resident),
        ] + buf_specs,
        out_specs=out_spec,
        out_shape=out_shape,
        input_output_aliases={6: 0} if bufs else {},
        scratch_shapes=[
            pltpu.VMEM((n_kb_max, tk, tq), jnp.int32),
            pltpu.VMEM((ATTN_KV_HEADS, 1, cols), F32),
            pltpu.VMEM((ATTN_KV_HEADS, 1, cols), F32),
            pltpu.VMEM((ATTN_KV_HEADS, ATTN_HEAD_DIM, cols), F32),
            pltpu.VMEM((1, tq), F32),
            pltpu.VMEM((ATTN_KV_HEADS, tk, cols), BF16),
        ],
        compiler_params=_params("parallel", "arbitrary"),
    )(q, qi, kx, k, vt, ki, *bufs)


_LEVELS = (32, 16, 8, 4, 2, 1)


def _hgrn_tables():
    c = CHUNK
    t = np.arange(c)[:, None]
    u = np.arange(c)[None, :]
    mats = [(u <= t), (u > t)]
    masks = []
    for w in _LEVELS:
        r = (t // (2 * w)) * (2 * w) + w - 1
        upper = (t % (2 * w)) >= w
        mats.append((upper & (u > r) & (u <= t)) | ((~upper) & (u > t) & (u <= r)))
        s = np.arange(c)[None, :]
        masks.append(((t // (2 * w)) == (s // (2 * w))) & upper & ((s % (2 * w)) < w))
    masks.append(t == np.arange(c)[None, :])
    table = np.concatenate(mats, axis=0).astype(np.float32)
    return np.concatenate([table] * 3, axis=1), np.stack(masks).astype(np.float32)


def _hgrn_kernel(q_ref, f_ref, i_ref, gt_ref, lb_ref, gn_ref, tab_ref, msk_ref, s0_ref, *rest, layer):
    o_ref, sfin_ref, st_ref = rest[-3:]
    c = CHUNK
    dh = HGRN_HEAD_DIM
    step = pl.program_id(1)

    @pl.when(step == 0)
    def _():
        for h in range(HGRN_HEADS):
            st_ref[h] = s0_ref[0, h].T

    lb_all = lb_ref[...]
    e = jnp.exp(lb_all - jnp.max(lb_all, axis=0, keepdims=True))
    p_lb = e / jnp.sum(e, axis=0, keepdims=True)
    lower = jnp.sum(p_lb[:layer + 1], axis=0, keepdims=True) - p_lb[0:1]

    def chunk_operands(ci):
        rows = slice(ci * c, (ci + 1) * c)
        f = lower + (1.0 - lower) * jax.nn.sigmoid(f_ref[0, rows])
        kk = 1.0 - f
        lf = jnp.log2(f)
        lf_hi = lf.astype(BF16)
        r1 = lf - lf_hi.astype(F32)
        lf_mid = r1.astype(BF16)
        lf_lo = (r1 - lf_mid.astype(F32)).astype(BF16)
        lf3 = jnp.concatenate([lf_hi, lf_mid, lf_lo], axis=0)
        ex = jnp.exp2(jnp.dot(tab_ref[...], lf3, preferred_element_type=F32))
        qq = q_ref[0, rows]
        vv = i_ref[0, rows]
        ops = dict(q16=qq.astype(BF16), k16=kk.astype(BF16), v16=vv.astype(BF16), vv=vv)
        ops["q_lv"] = [(qq * ex[(2 + li) * c:(3 + li) * c]).astype(BF16) for li in range(len(_LEVELS))]
        ops["k_lv"] = [(kk * ex[(2 + li) * c:(3 + li) * c]).astype(BF16) for li in range(len(_LEVELS))]
        ops["q_in"] = (qq * ex[0:c]).astype(BF16)
        ops["k_out"] = (kk * ex[c:2 * c]).astype(BF16)
        ops["decay"] = ex[c - 1:c]
        ops["gate"] = jax.nn.sigmoid(gt_ref[0, rows])
        return ops

    def chunk_outputs(ci, ops):
        rows = slice(ci * c, (ci + 1) * c)
        scores = []
        for h in range(HGRN_HEADS):
            sl = slice(h * dh, (h + 1) * dh)
            s_h = msk_ref[len(_LEVELS)] * lax.dot_general(
                ops["q16"][:, sl], ops["k16"][:, sl], _NT, preferred_element_type=F32)
            for li in range(len(_LEVELS)):
                s_h = s_h + msk_ref[li] * lax.dot_general(
                    ops["q_lv"][li][:, sl], ops["k_lv"][li][:, sl], _NT, preferred_element_type=F32)
            scores.append(s_h.astype(BF16))
        for h in range(HGRN_HEADS):
            sl = slice(h * dh, (h + 1) * dh)
            st = st_ref[h]
            o_h = jnp.dot(scores[h], ops["v16"][:, sl], preferred_element_type=F32)
            o_h = o_h + lax.dot_general(ops["q_in"][:, sl], st.astype(BF16), _NT, preferred_element_type=F32)
            st_ref[h] = st * ops["decay"][:, sl] + jnp.dot(
                ops["vv"][:, sl].T.astype(BF16), ops["k_out"][:, sl], preferred_element_type=F32)
            o_ref[0, rows, sl] = (_rms(o_h, gn_ref[:, sl]) * ops["gate"][:, sl]).astype(o_ref.dtype)

    n_sub = q_ref.shape[1] // c
    ops = chunk_operands(0)
    for ci in range(n_sub):
        ops_next = chunk_operands(ci + 1) if ci + 1 < n_sub else None
        chunk_outputs(ci, ops)
        ops = ops_next

    @pl.when(step == pl.num_programs(1) - 1)
    def _():
        for h in range(HGRN_HEADS):
            sfin_ref[0, h] = st_ref[h].T


def hgrn_scan(p, lb, gnorm, s0, *, layer, n_rows=None, out_rows=None, into=None):
    b = p.shape[0]
    l = p.shape[1] if n_rows is None else n_rows
    d = D_MODEL
    c = max(r for r in (4 * CHUNK, 2 * CHUNK, CHUNK) if l % r == 0)
    tab, msk = _hgrn_tables()
    kern = functools.partial(_hgrn_kernel, layer=layer)
    col = lambda j: pl.BlockSpec((1, c, d), lambda bi, t, j=j: (bi, t, j))
    state_spec = pl.BlockSpec((1, HGRN_HEADS, HGRN_HEAD_DIM, HGRN_HEAD_DIM), lambda bi, t: (bi, 0, 0, 0))
    out_spec, out_shape, buf_specs, bufs = _rows_out_spec(d, l // c, (b, c, out_rows or l), into, BF16)
    return pl.pallas_call(
        kern,
        grid=(b, l // c),
        in_specs=[
            col(0), col(1), col(2), col(3),
            pl.BlockSpec(lb.shape, lambda bi, t: (0, 0)),
            pl.BlockSpec((1, d), lambda bi, t: (0, 0)),
            pl.BlockSpec(tab.shape, lambda bi, t: (0, 0)),
            pl.BlockSpec(msk.shape, lambda bi, t: (0, 0, 0)),
            state_spec,
        ] + buf_specs,
        out_specs=[out_spec, state_spec],
        out_shape=[out_shape, jax.ShapeDtypeStruct(s0.shape, F32)],
        input_output_aliases={9: 0} if bufs else {},
        scratch_shapes=[pltpu.VMEM((HGRN_HEADS, HGRN_HEAD_DIM, HGRN_HEAD_DIM), F32)],
        compiler_params=_params("parallel", "arbitrary"),
    )(p, p, p, p, lb, gnorm.reshape(1, d), jnp.asarray(tab, BF16), jnp.asarray(msk), s0, *bufs)


def _attn_layer(x, g, w_in, w_out, layer, new_kv, cache_k, cache_v, cache_kidx, n_p, b_s, t_s):
    q, qi, kx, new_kv, k16, vt, ki16 = attn_in_proj(x, g, w_in, layer=layer, n_prompt=n_p, new_kv=new_kv)
    k_new, v_new, ki_new = new_kv[3][layer], new_kv[4][layer], new_kv[5][layer]
    past = cache_k.shape[1]
    tq = LANES

    o_p = dsa_attention(
        q[None], qi[None], kx[None], k16[None], vt[None], ki16[None], v_is_transposed=True,
        n_q=n_p, tq=tq, tk=vt.shape[2], q_pos0=0, k_top=min(TOPK_MAX, n_p // 4),
        out_rows=x.shape[0])

    tk_s = 256
    s_all = past + t_s
    n_adm_pad = ((past + tq - 1) // CHUNK + 1) * CHUNK
    s_pad = -(-max(s_all, n_adm_pad) // tk_s) * tk_s

    def with_cache(cache, new, width):
        full = jnp.zeros((b_s, s_pad, width), BF16)
        full = lax.dynamic_update_slice(full, cache.reshape(b_s, past, width).astype(BF16), (0, 0, 0))
        return lax.dynamic_update_slice(full, new.astype(BF16).reshape(b_s, t_s, width), (0, past, 0))

    def sample_heads(a, heads, dim):
        a = a[:, n_p:].reshape(heads, b_s, t_s, dim).transpose(1, 0, 2, 3)
        return jnp.pad(a, ((0, 0), (0, 0), (0, tq - t_s), (0, 0)))

    kx_s = jnp.pad(kx[n_p:].reshape(b_s, t_s, LANES), ((0, 0), (0, tq - t_s), (0, 0)))
    o_s = dsa_attention(
        sample_heads(q, ATTN_HEADS, ATTN_HEAD_DIM), sample_heads(qi, IDX_HEADS, IDX_HEAD_DIM), kx_s,
        with_cache(cache_k, k_new, ATTN_KV_DIM), with_cache(cache_v, v_new, ATTN_KV_DIM),
        with_cache(cache_kidx, ki_new, IDX_HEAD_DIM),
        n_q=tq, tq=tq, tk=tk_s, q_pos0=past, k_top=min(TOPK_MAX, s_all // 4),
        into=(o_p, n_p, t_s))

    x = matmul_residual(o_s[0], w_out, x, layer=layer)
    return x, new_kv


def _rec_layer(x, g, w_in, w_out, gnorm, rec_lb, state, layer, n_p, b_s, t_s):
    p = norm_matmul(x, g, w_in, layer=layer)
    s0_p = jnp.zeros((1,) + state.shape[1:], F32)
    o_p, s_p = hgrn_scan(p[None], rec_lb, gnorm, s0_p, layer=layer, n_rows=n_p, out_rows=x.shape[0])
    o_s, s_s = hgrn_scan(p[n_p:].reshape(b_s, t_s, -1), rec_lb, gnorm, state, layer=layer,
                         into=(o_p, n_p, t_s))
    x = matmul_residual(o_s[0], w_out, x, layer=layer)
    return x, s_p, s_s


def kernel(x_prompt, x_sample, cache_k, cache_v, cache_kidx, state_s, norm_mix, norm_mlp, norm_final,
           attn_w_in, attn_w_out, rec_w_in, rec_w_out, rec_gnorm, rec_lb, mlp_w_up, mlp_w_down):
    b_p, l_p, d = x_prompt.shape
    b_s, t_s, _ = x_sample.shape
    assert b_p == 1
    n_p = b_p * l_p
    x = jnp.concatenate([x_prompt.reshape(n_p, d), x_sample.reshape(b_s * t_s, d)], axis=0)

    sp, ss = [], []
    new_kv = None
    for layer in range(DEPTH):
        j = layer // N_MIXERS
        if layer % N_MIXERS == 0:
            x, new_kv = _attn_layer(
                x, norm_mix[layer], attn_w_in, attn_w_out, j, new_kv,
                cache_k[j], cache_v[j], cache_kidx[j], n_p, b_s, t_s)
        else:
            x, s_p, s_s = _rec_layer(
                x, norm_mix[layer], rec_w_in, rec_w_out, rec_gnorm[j], rec_lb, state_s[j],
                j, n_p, b_s, t_s)
            sp.append(s_p)
            ss.append(s_s)
        x = mlp_residual(x, norm_mlp[layer], mlp_w_up, mlp_w_down, layer=layer,
                         g_final=norm_final if layer == DEPTH - 1 else None)

    y = x
    kp, vp, kip, ks, vs, kis = new_kv
    n_l = kp.shape[0]
    heads = (ATTN_KV_HEADS, ATTN_HEAD_DIM)
    return (y[:n_p].reshape(b_p, l_p, d), y[n_p:].reshape(b_s, t_s, d),
            kp.reshape(n_l, b_p, l_p, *heads), vp.reshape(n_l, b_p, l_p, *heads),
            kip.reshape(n_l, b_p, l_p, IDX_HEAD_DIM), jnp.stack(sp),
            ks.reshape(n_l, b_s, t_s, *heads), vs.reshape(n_l, b_s, t_s, *heads),
            kis.reshape(n_l, b_s, t_s, IDX_HEAD_DIM), jnp.stack(ss))
```

```python
import functools

import numpy as np
import jax
import jax.numpy as jnp
from jax import lax
from jax.experimental import pallas as pl
from jax.experimental.pallas import tpu as pltpu

D_MODEL = 2048
DEPTH = 4
CHUNK = 64
N_MIXERS = 2
ATTN_HEADS = 16
ATTN_KV_HEADS = 4
ATTN_HEAD_DIM = D_MODEL // ATTN_HEADS
ATTN_GROUP = ATTN_HEADS // ATTN_KV_HEADS
IDX_HEADS = 16
IDX_HEAD_DIM = 64
TOPK_MAX = 256
HGRN_HEAD_DIM = 128
HGRN_HEADS = D_MODEL // HGRN_HEAD_DIM
D_FF = 4 * D_MODEL
RMS_EPS = 1e-6
ATTN_Q_DIM = ATTN_HEADS * ATTN_HEAD_DIM
ATTN_KV_DIM = ATTN_KV_HEADS * ATTN_HEAD_DIM
IDX_Q_DIM = IDX_HEADS * IDX_HEAD_DIM

LANES = 128
SUBLANES = 8
VMEM_LIMIT = 56 * 1024 * 1024

F32 = jnp.float32
BF16 = jnp.bfloat16
INT_MIN = np.int32(-(2 ** 31))
NEG_BIG = -1e30
LOG2_E = 1.4426950408889634

_NT = (((1,), (1,)), ((), ()))


def _params(*sem):
    return pltpu.CompilerParams(dimension_semantics=sem, vmem_limit_bytes=VMEM_LIMIT)


def _rms(x, g):
    ms = jnp.mean(x * x, axis=-1, keepdims=True)
    return (x * lax.rsqrt(ms + RMS_EPS)) * g


def _norm_matmul_kernel(x_ref, g_ref, w_ref, o_ref, h_ref):
    @pl.when(pl.program_id(1) == 0)
    def _():
        h_ref[...] = _rms(x_ref[...], g_ref[...]).astype(BF16)

    o_ref[...] = jnp.dot(h_ref[...], w_ref[...].astype(BF16), preferred_element_type=F32)


def _row_tile(m, cap=1088):
    return max(t for t in range(16, cap + 1, 16) if m % t == 0)


_ONE_BUFFER = dict(pipeline_mode=pl.Buffered(1))


def norm_matmul(x, g, w, *, layer, tn=512):
    m, d = x.shape
    n = w.shape[2]
    tm = _row_tile(m)
    assert n % tn == 0
    return pl.pallas_call(
        _norm_matmul_kernel,
        grid=(m // tm, n // tn),
        in_specs=[
            pl.BlockSpec((tm, d), lambda i, j: (i, 0), **_ONE_BUFFER),
            pl.BlockSpec((1, d), lambda i, j: (0, 0)),
            pl.BlockSpec((None, d, tn), lambda i, j: (layer, 0, j)),
        ],
        out_specs=pl.BlockSpec((tm, tn), lambda i, j: (i, j)),
        out_shape=jax.ShapeDtypeStruct((m, n), F32),
        scratch_shapes=[pltpu.VMEM((tm, d), BF16)],
        compiler_params=_params("parallel", "arbitrary"),
    )(x, g.reshape(1, d), w)


def _round_bf16_kernel(w_ref, o_ref):
    o_ref[...] = w_ref[...].astype(BF16)


def round_bf16(w, *, layer, n_cols, tn=512):
    d = w.shape[1]
    assert n_cols % tn == 0
    return pl.pallas_call(
        _round_bf16_kernel,
        grid=(n_cols // tn,),
        in_specs=[pl.BlockSpec((None, d, tn), lambda j: (layer, 0, j))],
        out_specs=pl.BlockSpec((d, tn), lambda j: (0, j)),
        out_shape=jax.ShapeDtypeStruct((d, n_cols), BF16),
        compiler_params=_params("parallel"),
    )(w)


def _attn_in_proj_kernel(x_ref, g_ref, w_ref, wkx_ref, *rest, n_q, n_qi, n_prompt_tiles):
    (q_ref, qi_ref, kx_ref, kp_ref, vp_ref, kip_ref, ks_ref, vs_ref, kis_ref,
     k16_ref, vt_ref, ki16_ref, h_ref) = rest[-13:]
    i = pl.program_id(0)
    j = pl.program_id(1)
    is_prompt = i < n_prompt_tiles
    is_sample = jnp.logical_not(is_prompt)

    @pl.when(j == 0)
    def _():
        h_ref[...] = _rms(x_ref[...], g_ref[...]).astype(BF16)

    def tile():
        return jnp.dot(h_ref[...], w_ref[...].astype(BF16), preferred_element_type=F32)

    @pl.when(j < n_q)
    def _():
        res = tile()
        for hh in range(q_ref.shape[0]):
            q_ref[hh] = res[:, hh * ATTN_HEAD_DIM:(hh + 1) * ATTN_HEAD_DIM].astype(q_ref.dtype)

    @pl.when(jnp.logical_and(j == n_q, is_prompt))
    def _():
        res = tile()
        kp_ref[...] = res
        k16_ref[...] = res.astype(BF16)

    @pl.when(jnp.logical_and(j == n_q, is_sample))
    def _():
        ks_ref[...] = tile()

    @pl.when(jnp.logical_and(j == n_q + 1, is_prompt))
    def _():
        res = tile()
        vp_ref[...] = res
        vt_ref[0] = res.T.astype(BF16)

    @pl.when(jnp.logical_and(j == n_q + 1, is_sample))
    def _():
        vs_ref[...] = tile()

    @pl.when(jnp.logical_and(j >= n_q + 2, j < n_q + 2 + n_qi))
    def _():
        res = tile()
        for hh in range(qi_ref.shape[0]):
            qi_ref[hh] = res[:, hh * IDX_HEAD_DIM:(hh + 1) * IDX_HEAD_DIM].astype(qi_ref.dtype)

    @pl.when(j == n_q + 2 + n_qi)
    def _():
        res = jnp.dot(h_ref[...], wkx_ref[...].astype(BF16), preferred_element_type=F32)
        kx_ref[...] = res

        @pl.when(is_prompt)
        def _():
            kip_ref[...] = res[:, :IDX_HEAD_DIM]
            ki16_ref[...] = res[:, :IDX_HEAD_DIM].astype(BF16)

        @pl.when(is_sample)
        def _():
            kis_ref[...] = res[:, :IDX_HEAD_DIM]


def attn_in_proj(x, g, w_in, *, layer, n_prompt, new_kv=None):
    m, d = x.shape
    tm = tn = ATTN_KV_DIM
    n_layers = w_in.shape[0]
    n_sample = m - n_prompt
    assert n_prompt % tm == 0 and n_sample % tm == 0 and n_sample > 0
    npt, nst = n_prompt // tm, n_sample // tm
    n_q, n_qi = ATTN_Q_DIM // tn, IDX_Q_DIM // tn
    n_main = n_q + 2 + n_qi
    tail = w_in.shape[2] - n_main * tn
    w_kx = jnp.pad(w_in[layer:layer + 1, :, n_main * tn:], ((0, 0), (0, 0), (0, LANES - tail)))
    w_main = round_bf16(w_in, layer=layer, n_cols=n_main * tn)
    clip = lambda v, hi: jnp.minimum(jnp.maximum(v, 0), hi)
    p_row = lambda i: jnp.minimum(i, npt - 1)
    s_row = lambda i: clip(i - npt, nst - 1)
    slab = lambda rows, width: jax.ShapeDtypeStruct((n_layers, rows, width), F32)
    kv_shapes = [slab(n_prompt, tn), slab(n_prompt, tn), slab(n_prompt, IDX_HEAD_DIM),
                 slab(n_sample, tn), slab(n_sample, tn), slab(n_sample, IDX_HEAD_DIM)]
    bufs = [] if new_kv is None else list(new_kv)
    outs = pl.pallas_call(
        functools.partial(_attn_in_proj_kernel, n_q=n_q, n_qi=n_qi, n_prompt_tiles=npt),
        grid=(m // tm, n_main + 1),
        in_specs=[
            pl.BlockSpec((tm, d), lambda i, j: (i, 0), **_ONE_BUFFER),
            pl.BlockSpec((1, d), lambda i, j: (0, 0)),
            pl.BlockSpec((d, tn), lambda i, j: (0, jnp.minimum(j, n_main - 1))),
            pl.BlockSpec((None, d, LANES), lambda i, j: (0, 0, 0)),
        ] + [pl.BlockSpec(memory_space=pl.ANY)] * len(bufs),
        out_specs=[
            pl.BlockSpec((tn // ATTN_HEAD_DIM, tm, ATTN_HEAD_DIM), lambda i, j: (clip(j, n_q - 1), i, 0)),
            pl.BlockSpec((tn // IDX_HEAD_DIM, tm, IDX_HEAD_DIM), lambda i, j: (clip(j - n_q - 2, n_qi - 1), i, 0)),
            pl.BlockSpec((tm, LANES), lambda i, j: (i, 0)),
            pl.BlockSpec((None, tm, tn), lambda i, j: (layer, p_row(i), 0)),
            pl.BlockSpec((None, tm, tn), lambda i, j: (layer, p_row(i), 0)),
            pl.BlockSpec((None, tm, IDX_HEAD_DIM), lambda i, j: (layer, p_row(i), 0)),
            pl.BlockSpec((None, tm, tn), lambda i, j: (layer, s_row(i), 0)),
            pl.BlockSpec((None, tm, tn), lambda i, j: (layer, s_row(i), 0)),
            pl.BlockSpec((None, tm, IDX_HEAD_DIM), lambda i, j: (layer, s_row(i), 0)),
            pl.BlockSpec((tm, tn), lambda i, j: (p_row(i), 0)),
            pl.BlockSpec((1, tn, tm), lambda i, j: (p_row(i), 0, 0)),
            pl.BlockSpec((tm, IDX_HEAD_DIM), lambda i, j: (p_row(i), 0)),
        ],
        out_shape=[
            jax.ShapeDtypeStruct((ATTN_HEADS, m, ATTN_HEAD_DIM), BF16),
            jax.ShapeDtypeStruct((IDX_HEADS, m, IDX_HEAD_DIM), BF16),
            jax.ShapeDtypeStruct((m, LANES), F32),
        ] + kv_shapes + [
            jax.ShapeDtypeStruct((n_prompt, tn), BF16),
            jax.ShapeDtypeStruct((npt, tn, tm), BF16),
            jax.ShapeDtypeStruct((n_prompt, IDX_HEAD_DIM), BF16),
        ],
        input_output_aliases={4 + n: 3 + n for n in range(len(bufs))},
        scratch_shapes=[pltpu.VMEM((tm, d), BF16)],
        compiler_params=_params("parallel", "arbitrary"),
    )(x, g.reshape(1, d), w_main, w_kx, *bufs)
    q, qi, kx = outs[:3]
    return q, qi, kx, tuple(outs[3:9]), outs[9], outs[10], outs[11]


def _matmul_res_kernel(a_ref, w_ref, r_ref, o_ref):
    o_ref[...] = r_ref[...] + jnp.dot(a_ref[...], w_ref[...].astype(BF16), preferred_element_type=F32)


def matmul_residual(a, w, res, *, layer, tn=512):
    m, k = a.shape
    n = w.shape[2]
    tm = _row_tile(m)
    assert n % tn == 0
    return pl.pallas_call(
        _matmul_res_kernel,
        grid=(m // tm, n // tn),
        in_specs=[
            pl.BlockSpec((tm, k), lambda i, j: (i, 0)),
            pl.BlockSpec((None, k, tn), lambda i, j: (layer, 0, j)),
            pl.BlockSpec((tm, tn), lambda i, j: (i, j)),
        ],
        out_specs=pl.BlockSpec((tm, tn), lambda i, j: (i, j)),
        out_shape=jax.ShapeDtypeStruct((m, n), F32),
        compiler_params=_params("parallel", "parallel"),
    )(a, w, res)


def _mlp_kernel(x_ref, g_ref, wu_ref, wd_ref, gf_ref, o_ref, h_ref, *, final_norm):
    @pl.when(pl.program_id(1) == 0)
    def _():
        x = x_ref[...]
        h_ref[...] = _rms(x, g_ref[...]).astype(BF16)
        o_ref[...] = x

    u = jnp.maximum(jnp.dot(h_ref[...], wu_ref[...].astype(BF16), preferred_element_type=F32), 0.0)
    o_ref[...] += jnp.dot((u * u).astype(BF16), wd_ref[...].astype(BF16), preferred_element_type=F32)

    if final_norm:
        @pl.when(pl.program_id(1) == pl.num_programs(1) - 1)
        def _():
            o_ref[...] = _rms(o_ref[...], gf_ref[...])


def mlp_residual(x, g, w_up, w_down, *, layer, g_final=None, tf=512):
    m, d = x.shape
    f = w_up.shape[2]
    tm = _row_tile(m)
    assert f % tf == 0
    final_norm = g_final is not None
    gf = (g_final if final_norm else g).reshape(1, d)
    return pl.pallas_call(
        functools.partial(_mlp_kernel, final_norm=final_norm),
        grid=(m // tm, f // tf),
        in_specs=[
            pl.BlockSpec((tm, d), lambda i, j: (i, 0), **_ONE_BUFFER),
            pl.BlockSpec((1, d), lambda i, j: (0, 0)),
            pl.BlockSpec((None, d, tf), lambda i, j: (layer, 0, j)),
            pl.BlockSpec((None, tf, d), lambda i, j: (layer, j, 0)),
            pl.BlockSpec((1, d), lambda i, j: (0, 0)),
        ],
        out_specs=pl.BlockSpec((tm, d), lambda i, j: (i, 0), **_ONE_BUFFER),
        out_shape=jax.ShapeDtypeStruct((m, d), F32),
        scratch_shapes=[pltpu.VMEM((tm, d), BF16)],
        compiler_params=_params("parallel", "arbitrary"),
    )(x, g.reshape(1, d), w_up, w_down, gf)


def _attn_kernel(q_ref, qi_ref, kx_ref, k_ref, vt_ref, ki_ref, *rest, tq, tk, q_pos0, k_top, n_kb_max):
    o_ref, key_ref, m_ref, l_ref, acc_ref, kn_ref, p_ref = rest[-7:]
    @pl.when(pl.program_id(1) == 0)
    def _():
        def kn_body(j, best):
            kf = k_ref[0, pl.ds(pl.multiple_of(j * tk, tk), tk), :].astype(F32)
            sq = kf * kf
            for g in range(ATTN_KV_HEADS):
                n2 = jnp.sum(sq[:, g * ATTN_HEAD_DIM:(g + 1) * ATTN_HEAD_DIM], axis=1, keepdims=True)
                best = jnp.maximum(best, jnp.max(n2, axis=0, keepdims=True))
            return best

        kn_ref[...] = jnp.broadcast_to(lax.fori_loop(0, n_kb_max, kn_body, jnp.zeros((1, 1), F32)), kn_ref.shape)

    q_start = q_pos0 + pl.program_id(1) * tq
    n_adm = ((q_start + tq - 1) // CHUNK + 1) * CHUNK
    n_kb = jnp.minimum((n_adm + tk - 1) // tk, n_kb_max)

    k_row = lax.broadcasted_iota(jnp.int32, (tk, tq), 0)
    k_lim = ((q_start + lax.broadcasted_iota(jnp.int32, (tk, tq), 1)) // CHUNK + 1) * CHUNK

    w_t = kx_ref[0].T[IDX_HEAD_DIM:IDX_HEAD_DIM + IDX_HEADS, :] * (IDX_HEADS ** -0.5 * IDX_HEAD_DIM ** -0.5)
    qi = qi_ref[0].reshape(IDX_HEADS * tq, IDX_HEAD_DIM)

    def score_body(j, carry):
        kib = ki_ref[0, pl.ds(pl.multiple_of(j * tk, tk), tk), :]
        d = lax.dot_general(kib, qi, _NT, preferred_element_type=F32)
        sc = jnp.zeros((tk, tq), F32)
        for h in range(IDX_HEADS):
            sc = sc + w_t[h:h + 1, :] * jnp.maximum(d[:, h * tq:(h + 1) * tq], 0.0)
        bits = pltpu.bitcast(sc, jnp.int32)
        key = bits ^ ((bits >> 31) & jnp.int32(0x7FFFFFFF))
        key_ref[j] = jnp.where(k_row < k_lim - j * tk, key, INT_MIN)
        return carry

    lax.fori_loop(0, n_kb, score_body, 0)

    rows_acc = 8 * SUBLANES

    def count_where(pred):
        def body(j, cnt):
            blk = key_ref[j]
            hit = pred(blk, j)
            for r in range(tk // rows_acc):
                cnt = jnp.where(hit[r * rows_acc:(r + 1) * rows_acc], cnt + 1.0, cnt)
            return cnt
        cnt = lax.fori_loop(0, n_kb, body, jnp.zeros((rows_acc, tq), F32))
        return jnp.sum(cnt, axis=0, keepdims=True)

    def bit_body(p, state):
        thr, cnt_thr = state
        cand = thr + jnp.left_shift(jnp.int32(1), 31 - p)
        cnt = count_where(lambda blk, j: blk >= cand)
        keep = cnt >= float(k_top)
        return jnp.where(keep, cand, thr), jnp.where(keep, cnt, cnt_thr)

    def search_bits(lo, hi, state):
        return lax.fori_loop(lo, hi, bit_body, state)

    def settled(state):
        return jnp.min(jnp.where(state[1] == float(k_top), 1.0, 0.0)) > 0.0

    state = search_bits(0, 22, (jnp.full((1, tq), INT_MIN, jnp.int32), jnp.full((1, tq), 2.0 ** 30, F32)))
    for lo in range(22, 32, 2):
        state = lax.cond(settled(state), lambda s: s, functools.partial(search_bits, lo, lo + 2), state)
    thr_raw, cnt_thr = state
    thr = jnp.maximum(thr_raw, INT_MIN + 1)

    tied = jnp.logical_and(cnt_thr > float(k_top), thr_raw > INT_MIN)
    n_tied = jnp.max(jnp.where(tied, 1.0, 0.0))

    @pl.when(n_tied > 0.0)
    def _():
        need = float(k_top) - count_where(lambda blk, j: blk > thr)
        n_bits = max(1, int(n_kb_max * tk - 1).bit_length())

        def pos_body(b, last):
            step = jnp.left_shift(jnp.int32(1), n_bits - 1 - b)
            cand = last + step - 1
            got = count_where(lambda blk, j: jnp.logical_and(blk == thr, k_row + j * tk <= cand))
            return jnp.where(got < need, last + step, last)

        last = lax.fori_loop(0, n_bits, pos_body, jnp.zeros((1, tq), jnp.int32))

        def demote_body(j, carry):
            blk = key_ref[j]
            drop = jnp.logical_and(blk == thr, k_row + j * tk > last)
            key_ref[j] = jnp.where(drop, thr - 1, blk)
            return carry

        lax.fori_loop(0, n_kb, demote_body, 0)

    m_ref[...] = jnp.full(m_ref.shape, NEG_BIG, F32)
    l_ref[...] = jnp.zeros(l_ref.shape, F32)
    acc_ref[...] = jnp.zeros(acc_ref.shape, F32)
    c_exp = ATTN_HEAD_DIM ** -0.5 * LOG2_E
    cols = ATTN_GROUP * tq

    def logits_t(j, g):
        k0 = pl.multiple_of(j * tk, tk)
        kb = k_ref[0, pl.ds(k0, tk), g * ATTN_HEAD_DIM:(g + 1) * ATTN_HEAD_DIM]
        vtb = vt_ref[0, j, g * ATTN_HEAD_DIM:(g + 1) * ATTN_HEAD_DIM, :]
        qg = q_ref[0, g * ATTN_GROUP:(g + 1) * ATTN_GROUP].reshape(cols, ATTN_HEAD_DIM)
        return lax.dot_general(kb, qg, _NT, preferred_element_type=F32), vtb

    qf = q_ref[0].reshape(ATTN_HEADS * tq, ATTN_HEAD_DIM).astype(F32)
    qn2 = lax.dot_general(jnp.ones((SUBLANES, ATTN_HEAD_DIM), BF16), (qf * qf).astype(BF16), _NT,
                          preferred_element_type=F32)
    qn2_max = qn2[0:1, 0:tq]
    for h in range(1, ATTN_HEADS):
        qn2_max = jnp.maximum(qn2_max, qn2[0:1, h * tq:(h + 1) * tq])
    m_bound = jnp.sqrt(qn2_max * kn_ref[...]) * 1.02
    fixed_shift_ok = jnp.max(m_bound) * (2.0 * c_exp) < 100.0

    @pl.when(fixed_shift_ok)
    def _():
        shift = -c_exp * m_bound
        ones = jnp.ones((2 * SUBLANES, tk), BF16)

        def accumulate(j):
            for g in range(ATTN_KV_HEADS):
                vtb = vt_ref[0, j, g * ATTN_HEAD_DIM:(g + 1) * ATTN_HEAD_DIM, :]
                acc_ref[g] += jnp.dot(vtb, p_ref[g], preferred_element_type=F32)
                l_ref[g] += jnp.dot(ones, p_ref[g], preferred_element_type=F32)[0:1]

        def probabilities(j):
            bias = jnp.where(key_ref[j] >= thr, shift, NEG_BIG)
            bias = jnp.concatenate([bias] * ATTN_GROUP, axis=1)
            for g in range(ATTN_KV_HEADS):
                s, _ = logits_t(j, g)
                p_ref[g] = jnp.exp2(s * c_exp + bias).astype(BF16)

        probabilities(0)

        def body(j, carry):
            accumulate(j - 1)
            probabilities(j)
            return carry

        lax.fori_loop(1, n_kb, body, 0)
        accumulate(n_kb - 1)

    @pl.when(jnp.logical_not(fixed_shift_ok))
    def _():
        def body(j, carry):
            bias = jnp.where(key_ref[j] >= thr, 0.0, NEG_BIG)
            bias = jnp.concatenate([bias] * ATTN_GROUP, axis=1)
            for g in range(ATTN_KV_HEADS):
                s, vtb = logits_t(j, g)
                s = s + bias
                m_old = m_ref[g]
                m_new = jnp.maximum(m_old, jnp.max(s, axis=0, keepdims=True))
                alpha = jnp.exp2((m_old - m_new) * c_exp)
                p = jnp.exp2((s - m_new) * c_exp)
                l_ref[g] = alpha * l_ref[g] + jnp.sum(p, axis=0, keepdims=True)
                acc_ref[g] = alpha * acc_ref[g] + jnp.dot(vtb, p.astype(BF16), preferred_element_type=F32)
                m_ref[g] = m_new
            return carry

        lax.fori_loop(0, n_kb, body, 0)

    n_out = o_ref.shape[1]
    for g in range(ATTN_KV_HEADS):
        out_t = acc_ref[g] / l_ref[g]
        for hh in range(ATTN_GROUP):
            h = g * ATTN_GROUP + hh
            o_ref[0, :, h * ATTN_HEAD_DIM:(h + 1) * ATTN_HEAD_DIM] = (
                out_t[:, hh * tq:(hh + 1) * tq].T[:n_out].astype(o_ref.dtype))


def _rows_out_spec(width, n_steps, out_rows, into, dtype):
    b, rows = out_rows[0], out_rows[1]
    if into is None:
        spec = pl.BlockSpec((1, rows, width), lambda bi, i: (bi, i, 0))
        return spec, jax.ShapeDtypeStruct((b, out_rows[2], width), dtype), [], []
    buf, first_row, rows = into
    assert first_row % rows == 0 and buf.shape[0] == 1 and buf.dtype == dtype
    blk0 = first_row // rows
    spec = pl.BlockSpec((1, rows, width), lambda bi, i: (0, blk0 + bi * n_steps + i, 0))
    return spec, jax.ShapeDtypeStruct(buf.shape, dtype), [pl.BlockSpec(memory_space=pl.ANY)], [buf]


def dsa_attention(q, qi, kx, k, v, ki, *, n_q, tq, tk, q_pos0, k_top, v_is_transposed=False,
                  out_rows=None, into=None):
    b = q.shape[0]
    s = k.shape[1]
    assert n_q % tq == 0 and s % tk == 0 and tq == LANES
    n_kb_max = s // tk
    cols = ATTN_GROUP * tq
    vt = v if v_is_transposed else v.reshape(b, n_kb_max, tk, ATTN_KV_DIM).transpose(0, 1, 3, 2)
    assert vt.shape == (b, n_kb_max, ATTN_KV_DIM, tk)
    kern = functools.partial(_attn_kernel, tq=tq, tk=tk, q_pos0=q_pos0, k_top=k_top, n_kb_max=n_kb_max)
    resident = dict(pipeline_mode=pl.Buffered(1))
    out_spec, out_shape, buf_specs, bufs = _rows_out_spec(
        ATTN_Q_DIM, n_q // tq, (b, tq, out_rows or n_q), into, BF16)
    return pl.pallas_call(
        kern,
        grid=(b, n_q // tq),
        in_specs=[
            pl.BlockSpec((1, ATTN_HEADS, tq, ATTN_HEAD_DIM), lambda bi, i: (bi, 0, i, 0)),
            pl.BlockSpec((1, IDX_HEADS, tq, IDX_HEAD_DIM), lambda bi, i: (bi, 0, i, 0)),
            pl.BlockSpec((1, tq, LANES), lambda bi, i: (bi, i, 0)),
            pl.BlockSpec((1, s, ATTN_KV_DIM), lambda bi, i: (bi, 0, 0), **resident),
            pl.BlockSpec((1, n_kb_max, ATTN_KV_DIM, tk), lambda bi, i: (bi, 0, 0, 0), **resident),
            pl.BlockSpec((1, s, IDX_HEAD_DIM), lambda bi, i: (bi, 0, 0), **resident),
        ] + buf_specs,
        out_specs=out_spec,
        out_shape=out_shape,
        input_output_aliases={6: 0} if bufs else {},
        scratch_shapes=[
            pltpu.VMEM((n_kb_max, tk, tq), jnp.int32),
            pltpu.VMEM((ATTN_KV_HEADS, 1, cols), F32),
            pltpu.VMEM((ATTN_KV_HEADS, 1, cols), F32),
            pltpu.VMEM((ATTN_KV_HEADS, ATTN_HEAD_DIM, cols), F32),
            pltpu.VMEM((1, tq), F32),
            pltpu.VMEM((ATTN_KV_HEADS, tk, cols), BF16),
        ],
        compiler_params=_params("parallel", "arbitrary"),
    )(q, qi, kx, k, vt, ki, *bufs)


_LEVELS = (32, 16, 8, 4, 2, 1)


def _hgrn_tables():
    c = CHUNK
    t = np.arange(c)[:, None]
    u = np.arange(c)[None, :]
    mats = [(u <= t), (u > t)]
    masks = []
    for w in _LEVELS:
        r = (t // (2 * w)) * (2 * w) + w - 1
        upper = (t % (2 * w)) >= w
        mats.append((upper & (u > r) & (u <= t)) | ((~upper) & (u > t) & (u <= r)))
        s = np.arange(c)[None, :]
        masks.append(((t // (2 * w)) == (s // (2 * w))) & upper & ((s % (2 * w)) < w))
    masks.append(t == np.arange(c)[None, :])
    table = np.concatenate(mats, axis=0).astype(np.float32)
    return np.concatenate([table] * 3, axis=1), np.stack(masks).astype(np.float32)


def _hgrn_kernel(q_ref, f_ref, i_ref, gt_ref, lb_ref, gn_ref, tab_ref, msk_ref, s0_ref, *rest, layer):
    o_ref, sfin_ref, st_ref = rest[-3:]
    c = CHUNK
    dh = HGRN_HEAD_DIM
    step = pl.program_id(1)

    @pl.when(step == 0)
    def _():
        for h in range(HGRN_HEADS):
            st_ref[h] = s0_ref[0, h].T

    lb_all = lb_ref[...]
    e = jnp.exp(lb_all - jnp.max(lb_all, axis=0, keepdims=True))
    p_lb = e / jnp.sum(e, axis=0, keepdims=True)
    lower = jnp.sum(p_lb[:layer + 1], axis=0, keepdims=True) - p_lb[0:1]

    def chunk_operands(ci):
        rows = slice(ci * c, (ci + 1) * c)
        f = lower + (1.0 - lower) * jax.nn.sigmoid(f_ref[0, rows])
        kk = 1.0 - f
        lf = jnp.log2(f)
        lf_hi = lf.astype(BF16)
        r1 = lf - lf_hi.astype(F32)
        lf_mid = r1.astype(BF16)
        lf_lo = (r1 - lf_mid.astype(F32)).astype(BF16)
        lf3 = jnp.concatenate([lf_hi, lf_mid, lf_lo], axis=0)
        ex = jnp.exp2(jnp.dot(tab_ref[...], lf3, preferred_element_type=F32))
        qq = q_ref[0, rows]
        vv = i_ref[0, rows]
        ops = dict(q16=qq.astype(BF16), k16=kk.astype(BF16), v16=vv.astype(BF16), vv=vv)
        ops["q_lv"] = [(qq * ex[(2 + li) * c:(3 + li) * c]).astype(BF16) for li in range(len(_LEVELS))]
        ops["k_lv"] = [(kk * ex[(2 + li) * c:(3 + li) * c]).astype(BF16) for li in range(len(_LEVELS))]
        ops["q_in"] = (qq * ex[0:c]).astype(BF16)
        ops["k_out"] = (kk * ex[c:2 * c]).astype(BF16)
        ops["decay"] = ex[c - 1:c]
        ops["gate"] = jax.nn.sigmoid(gt_ref[0, rows])
        return ops

    def chunk_outputs(ci, ops):
        rows = slice(ci * c, (ci + 1) * c)
        scores = []
        for h in range(HGRN_HEADS):
            sl = slice(h * dh, (h + 1) * dh)
            s_h = msk_ref[len(_LEVELS)] * lax.dot_general(
                ops["q16"][:, sl], ops["k16"][:, sl], _NT, preferred_element_type=F32)
            for li in range(len(_LEVELS)):
                s_h = s_h + msk_ref[li] * lax.dot_general(
                    ops["q_lv"][li][:, sl], ops["k_lv"][li][:, sl], _NT, preferred_element_type=F32)
            scores.append(s_h.astype(BF16))
        for h in range(HGRN_HEADS):
            sl = slice(h * dh, (h + 1) * dh)
            st = st_ref[h]
            o_h = jnp.dot(scores[h], ops["v16"][:, sl], preferred_element_type=F32)
            o_h = o_h + lax.dot_general(ops["q_in"][:, sl], st.astype(BF16), _NT, preferred_element_type=F32)
            st_ref[h] = st * ops["decay"][:, sl] + jnp.dot(
                ops["vv"][:, sl].T.astype(BF16), ops["k_out"][:, sl], preferred_element_type=F32)
            o_ref[0, rows, sl] = (_rms(o_h, gn_ref[:, sl]) * ops["gate"][:, sl]).astype(o_ref.dtype)

    n_sub = q_ref.shape[1] // c
    ops = chunk_operands(0)
    for ci in range(n_sub):
        ops_next = chunk_operands(ci + 1) if ci + 1 < n_sub else None
        chunk_outputs(ci, ops)
        ops = ops_next

    @pl.when(step == pl.num_programs(1) - 1)
    def _():
        for h in range(HGRN_HEADS):
            sfin_ref[0, h] = st_ref[h].T


def hgrn_scan(p, lb, gnorm, s0, *, layer, n_rows=None, out_rows=None, into=None, states=None):
    b = p.shape[0]
    l = p.shape[1] if n_rows is None else n_rows
    d = D_MODEL
    c = max(r for r in (4 * CHUNK, 2 * CHUNK, CHUNK) if l % r == 0)
    tab, msk = _hgrn_tables()
    kern = functools.partial(_hgrn_kernel, layer=layer)
    col = lambda j: pl.BlockSpec((1, c, d), lambda bi, t, j=j: (bi, t, j))
    state_spec = pl.BlockSpec((1, HGRN_HEADS, HGRN_HEAD_DIM, HGRN_HEAD_DIM), lambda bi, t: (bi, 0, 0, 0))
    out_spec, out_shape, buf_specs, bufs = _rows_out_spec(d, l // c, (b, c, out_rows or l), into, BF16)
    aliases = {9: 0} if bufs else {}
    states_spec = pl.BlockSpec((None,) + state_spec.block_shape, lambda bi, t: (layer, bi, 0, 0, 0))
    if states is not None:
        aliases[9 + len(bufs)] = 1
        buf_specs = buf_specs + [pl.BlockSpec(memory_space=pl.ANY)]
        bufs = bufs + [states]
    return pl.pallas_call(
        kern,
        grid=(b, l // c),
        in_specs=[
            col(0), col(1), col(2), col(3),
            pl.BlockSpec(lb.shape, lambda bi, t: (0, 0)),
            pl.BlockSpec((1, d), lambda bi, t: (0, 0)),
            pl.BlockSpec(tab.shape, lambda bi, t: (0, 0)),
            pl.BlockSpec(msk.shape, lambda bi, t: (0, 0, 0)),
            state_spec,
        ] + buf_specs,
        out_specs=[out_spec, states_spec],
        out_shape=[out_shape, jax.ShapeDtypeStruct((lb.shape[0],) + s0.shape, F32)],
        input_output_aliases=aliases,
        scratch_shapes=[pltpu.VMEM((HGRN_HEADS, HGRN_HEAD_DIM, HGRN_HEAD_DIM), F32)],
        compiler_params=_params("parallel", "arbitrary"),
    )(p, p, p, p, lb, gnorm.reshape(1, d), jnp.asarray(tab, BF16), jnp.asarray(msk), s0, *bufs)


def _attn_layer(x, g, w_in, w_out, layer, new_kv, cache_k, cache_v, cache_kidx, n_p, b_s, t_s):
    q, qi, kx, new_kv, k16, vt, ki16 = attn_in_proj(x, g, w_in, layer=layer, n_prompt=n_p, new_kv=new_kv)
    k_new, v_new, ki_new = new_kv[3][layer], new_kv[4][layer], new_kv[5][layer]
    past = cache_k.shape[1]
    tq = LANES

    o_p = dsa_attention(
        q[None], qi[None], kx[None], k16[None], vt[None], ki16[None], v_is_transposed=True,
        n_q=n_p, tq=tq, tk=vt.shape[2], q_pos0=0, k_top=min(TOPK_MAX, n_p // 4),
        out_rows=x.shape[0])

    tk_s = 256
    s_all = past + t_s
    n_adm_pad = ((past + tq - 1) // CHUNK + 1) * CHUNK
    s_pad = -(-max(s_all, n_adm_pad) // tk_s) * tk_s

    def with_cache(cache, new, width):
        full = jnp.zeros((b_s, s_pad, width), BF16)
        full = lax.dynamic_update_slice(full, cache.reshape(b_s, past, width).astype(BF16), (0, 0, 0))
        return lax.dynamic_update_slice(full, new.astype(BF16).reshape(b_s, t_s, width), (0, past, 0))

    def sample_heads(a, heads, dim):
        a = a[:, n_p:].reshape(heads, b_s, t_s, dim).transpose(1, 0, 2, 3)
        return jnp.pad(a, ((0, 0), (0, 0), (0, tq - t_s), (0, 0)))

    kx_s = jnp.pad(kx[n_p:].reshape(b_s, t_s, LANES), ((0, 0), (0, tq - t_s), (0, 0)))
    o_s = dsa_attention(
        sample_heads(q, ATTN_HEADS, ATTN_HEAD_DIM), sample_heads(qi, IDX_HEADS, IDX_HEAD_DIM), kx_s,
        with_cache(cache_k, k_new, ATTN_KV_DIM), with_cache(cache_v, v_new, ATTN_KV_DIM),
        with_cache(cache_kidx, ki_new, IDX_HEAD_DIM),
        n_q=tq, tq=tq, tk=tk_s, q_pos0=past, k_top=min(TOPK_MAX, s_all // 4),
        into=(o_p, n_p, t_s))

    x = matmul_residual(o_s[0], w_out, x, layer=layer)
    return x, new_kv


def _rec_layer(x, g, w_in, w_out, gnorm, rec_lb, state, layer, new_states, n_p, b_s, t_s):
    p = norm_matmul(x, g, w_in, layer=layer)
    s0_p = jnp.zeros((1,) + state.shape[1:], F32)
    o_p, s_p = hgrn_scan(p[None], rec_lb, gnorm, s0_p, layer=layer, n_rows=n_p, out_rows=x.shape[0],
                         states=new_states[0])
    o_s, s_s = hgrn_scan(p[n_p:].reshape(b_s, t_s, -1), rec_lb, gnorm, state, layer=layer,
                         into=(o_p, n_p, t_s),
                         states=new_states[1])
    x = matmul_residual(o_s[0], w_out, x, layer=layer)
    return x, (s_p, s_s)


def kernel(x_prompt, x_sample, cache_k, cache_v, cache_kidx, state_s, norm_mix, norm_mlp, norm_final,
           attn_w_in, attn_w_out, rec_w_in, rec_w_out, rec_gnorm, rec_lb, mlp_w_up, mlp_w_down):
    b_p, l_p, d = x_prompt.shape
    b_s, t_s, _ = x_sample.shape
    assert b_p == 1
    n_p = b_p * l_p
    x = jnp.concatenate([x_prompt.reshape(n_p, d), x_sample.reshape(b_s * t_s, d)], axis=0)

    new_states = (None, None)
    new_kv = None
    for layer in range(DEPTH):
        j = layer // N_MIXERS
        if layer % N_MIXERS == 0:
            x, new_kv = _attn_layer(
                x, norm_mix[layer], attn_w_in, attn_w_out, j, new_kv,
                cache_k[j], cache_v[j], cache_kidx[j], n_p, b_s, t_s)
        else:
            x, new_states = _rec_layer(
                x, norm_mix[layer], rec_w_in, rec_w_out, rec_gnorm[j], rec_lb, state_s[j],
                j, new_states, n_p, b_s, t_s)
        x = mlp_residual(x, norm_mlp[layer], mlp_w_up, mlp_w_down, layer=layer,
                         g_final=norm_final if layer == DEPTH - 1 else None)

    y = x
    kp, vp, kip, ks, vs, kis = new_kv
    n_l = kp.shape[0]
    heads = (ATTN_KV_HEADS, ATTN_HEAD_DIM)
    return (y[:n_p].reshape(b_p, l_p, d), y[n_p:].reshape(b_s, t_s, d),
            kp.reshape(n_l, b_p, l_p, *heads), vp.reshape(n_l, b_p, l_p, *heads),
            kip.reshape(n_l, b_p, l_p, IDX_HEAD_DIM), new_states[0],
            ks.reshape(n_l, b_s, t_s, *heads), vs.reshape(n_l, b_s, t_s, *heads),
            kis.reshape(n_l, b_s, t_s, IDX_HEAD_DIM), new_states[1])
```

```python
import functools

import numpy as np
import jax
import jax.numpy as jnp
from jax import lax
from jax.experimental import pallas as pl
from jax.experimental.pallas import tpu as pltpu

D_MODEL = 2048
DEPTH = 4
CHUNK = 64
N_MIXERS = 2
ATTN_HEADS = 16
ATTN_KV_HEADS = 4
ATTN_HEAD_DIM = D_MODEL // ATTN_HEADS
ATTN_GROUP = ATTN_HEADS // ATTN_KV_HEADS
IDX_HEADS = 16
IDX_HEAD_DIM = 64
TOPK_MAX = 256
HGRN_HEAD_DIM = 128
HGRN_HEADS = D_MODEL // HGRN_HEAD_DIM
D_FF = 4 * D_MODEL
RMS_EPS = 1e-6
ATTN_Q_DIM = ATTN_HEADS * ATTN_HEAD_DIM
ATTN_KV_DIM = ATTN_KV_HEADS * ATTN_HEAD_DIM
IDX_Q_DIM = IDX_HEADS * IDX_HEAD_DIM

LANES = 128
SUBLANES = 8
VMEM_LIMIT = 56 * 1024 * 1024

F32 = jnp.float32
BF16 = jnp.bfloat16
INT_MIN = np.int32(-(2 ** 31))
NEG_BIG = -1e30
LOG2_E = 1.4426950408889634

_NT = (((1,), (1,)), ((), ()))


def _params(*sem):
    return pltpu.CompilerParams(dimension_semantics=sem, vmem_limit_bytes=VMEM_LIMIT)


def _rms(x, g):
    ms = jnp.mean(x * x, axis=-1, keepdims=True)
    return (x * lax.rsqrt(ms + RMS_EPS)) * g


def _norm_matmul_kernel(x_ref, g_ref, w_ref, o_ref, h_ref):
    @pl.when(pl.program_id(1) == 0)
    def _():
        h_ref[...] = _rms(x_ref[...], g_ref[...]).astype(BF16)

    o_ref[...] = jnp.dot(h_ref[...], w_ref[...].astype(BF16), preferred_element_type=F32)


def _row_tile(m, cap=1088):
    return max(t for t in range(16, cap + 1, 16) if m % t == 0)


_ONE_BUFFER = dict(pipeline_mode=pl.Buffered(1))


def norm_matmul(x, g, w, *, layer, tn=512):
    m, d = x.shape
    n = w.shape[2]
    tm = _row_tile(m)
    assert n % tn == 0
    return pl.pallas_call(
        _norm_matmul_kernel,
        grid=(m // tm, n // tn),
        in_specs=[
            pl.BlockSpec((tm, d), lambda i, j: (i, 0), **_ONE_BUFFER),
            pl.BlockSpec((1, d), lambda i, j: (0, 0)),
            pl.BlockSpec((None, d, tn), lambda i, j: (layer, 0, j)),
        ],
        out_specs=pl.BlockSpec((tm, tn), lambda i, j: (i, j)),
        out_shape=jax.ShapeDtypeStruct((m, n), F32),
        scratch_shapes=[pltpu.VMEM((tm, d), BF16)],
        compiler_params=_params("parallel", "arbitrary"),
    )(x, g.reshape(1, d), w)


def _round_bf16_kernel(w_ref, o_ref):
    o_ref[...] = w_ref[...].astype(BF16)


def round_bf16(w, *, layer, n_cols, tn=512):
    d = w.shape[1]
    assert n_cols % tn == 0
    return pl.pallas_call(
        _round_bf16_kernel,
        grid=(n_cols // tn,),
        in_specs=[pl.BlockSpec((None, d, tn), lambda j: (layer, 0, j))],
        out_specs=pl.BlockSpec((d, tn), lambda j: (0, j)),
        out_shape=jax.ShapeDtypeStruct((d, n_cols), BF16),
        compiler_params=_params("parallel"),
    )(w)


def _attn_in_proj_kernel(x_ref, g_ref, w_ref, wkx_ref, *rest, n_q, n_qi, prompt, tk):
    n_out = 9 if prompt else 6
    outs, h_ref = rest[-(n_out + 1):-1], rest[-1]
    q_ref, qi_ref, kx_ref, k_ref, v_ref, ki_ref = outs[:6]
    j = pl.program_id(1)

    @pl.when(j == 0)
    def _():
        h_ref[...] = _rms(x_ref[...], g_ref[...]).astype(BF16)

    def tile():
        return jnp.dot(h_ref[...], w_ref[...], preferred_element_type=F32)

    @pl.when(j < n_q)
    def _():
        res = tile()
        for hh in range(q_ref.shape[0]):
            q_ref[hh] = res[:, hh * ATTN_HEAD_DIM:(hh + 1) * ATTN_HEAD_DIM].astype(q_ref.dtype)

    @pl.when(j == n_q)
    def _():
        res = tile()
        k_ref[...] = res
        if prompt:
            outs[6][...] = res.astype(BF16)

    @pl.when(j == n_q + 1)
    def _():
        res = tile()
        v_ref[...] = res
        if prompt:
            for t in range(outs[7].shape[0]):
                outs[7][t] = res[t * tk:(t + 1) * tk].T.astype(BF16)

    @pl.when(jnp.logical_and(j >= n_q + 2, j < n_q + 2 + n_qi))
    def _():
        res = tile()
        for hh in range(qi_ref.shape[0]):
            qi_ref[hh] = res[:, hh * IDX_HEAD_DIM:(hh + 1) * IDX_HEAD_DIM].astype(qi_ref.dtype)

    @pl.when(j == n_q + 2 + n_qi)
    def _():
        res = jnp.dot(h_ref[...], wkx_ref[...].astype(BF16), preferred_element_type=F32)
        kx_ref[...] = res
        ki_ref[...] = res[:, :IDX_HEAD_DIM]
        if prompt:
            outs[8][...] = res[:, :IDX_HEAD_DIM].astype(BF16)


def attn_in_proj(x, g, w_in, *, layer, n_prompt, new_kv=None):
    m, d = x.shape
    tn = tk = ATTN_KV_DIM
    n_layers = w_in.shape[0]
    n_q, n_qi = ATTN_Q_DIM // tn, IDX_Q_DIM // tn
    n_main = n_q + 2 + n_qi
    tail = w_in.shape[2] - n_main * tn
    w_kx = jnp.pad(w_in[layer:layer + 1, :, n_main * tn:], ((0, 0), (0, 0), (0, LANES - tail)))
    w_main = round_bf16(w_in, layer=layer, n_cols=n_main * tn)
    clip = lambda v, hi: jnp.minimum(jnp.maximum(v, 0), hi)

    def project(row0, n_rows, tm, prompt, shared, slabs):
        assert row0 % tm == 0 and n_rows % tm == 0 and (tm % tk == 0 or not prompt)
        r0 = row0 // tm
        bufs = list(shared or ()) + list(slabs or ())
        aliases = {}
        if shared:
            aliases.update({4 + n: n for n in range(3)})
        if slabs:
            aliases.update({4 + len(shared or ()) + n: 3 + n for n in range(3)})
        slab = lambda width: jax.ShapeDtypeStruct((n_layers, n_rows, width), F32)
        out_specs = [
            pl.BlockSpec((tn // ATTN_HEAD_DIM, tm, ATTN_HEAD_DIM), lambda i, j: (clip(j, n_q - 1), r0 + i, 0)),
            pl.BlockSpec((tn // IDX_HEAD_DIM, tm, IDX_HEAD_DIM),
                         lambda i, j: (clip(j - n_q - 2, n_qi - 1), r0 + i, 0)),
            pl.BlockSpec((tm, LANES), lambda i, j: (r0 + i, 0)),
            pl.BlockSpec((None, tm, tn), lambda i, j: (layer, i, 0)),
            pl.BlockSpec((None, tm, tn), lambda i, j: (layer, i, 0)),
            pl.BlockSpec((None, tm, IDX_HEAD_DIM), lambda i, j: (layer, i, 0)),
        ]
        out_shape = [
            jax.ShapeDtypeStruct((ATTN_HEADS, m, ATTN_HEAD_DIM), BF16),
            jax.ShapeDtypeStruct((IDX_HEADS, m, IDX_HEAD_DIM), BF16),
            jax.ShapeDtypeStruct((m, LANES), F32),
            slab(tn), slab(tn), slab(IDX_HEAD_DIM),
        ]
        if prompt:
            out_specs += [
                pl.BlockSpec((tm, tn), lambda i, j: (i, 0)),
                pl.BlockSpec((tm // tk, tn, tk), lambda i, j: (i, 0, 0)),
                pl.BlockSpec((tm, IDX_HEAD_DIM), lambda i, j: (i, 0)),
            ]
            out_shape += [
                jax.ShapeDtypeStruct((n_rows, tn), BF16),
                jax.ShapeDtypeStruct((n_rows // tk, tn, tk), BF16),
                jax.ShapeDtypeStruct((n_rows, IDX_HEAD_DIM), BF16),
            ]
        return pl.pallas_call(
            functools.partial(_attn_in_proj_kernel, n_q=n_q, n_qi=n_qi, prompt=prompt, tk=tk),
            grid=(n_rows // tm, n_main + 1),
            in_specs=[
                pl.BlockSpec((tm, d), lambda i, j: (r0 + i, 0), **_ONE_BUFFER),
                pl.BlockSpec((1, d), lambda i, j: (0, 0)),
                pl.BlockSpec((d, tn), lambda i, j: (0, jnp.minimum(j, n_main - 1))),
                pl.BlockSpec((None, d, LANES), lambda i, j: (0, 0, 0)),
            ] + [pl.BlockSpec(memory_space=pl.ANY)] * len(bufs),
            out_specs=out_specs,
            out_shape=out_shape,
            input_output_aliases=aliases,
            scratch_shapes=[pltpu.VMEM((tm, d), BF16)],
            compiler_params=_params("parallel", "arbitrary"),
        )(x, g.reshape(1, d), w_main, w_kx, *bufs)

    n_sample = m - n_prompt
    tm_p = max(t for t in (2 * tk, tk) if n_prompt % t == 0)
    tm_s = _row_tile(n_sample, cap=tk)
    prev_p, prev_s = (None, None) if new_kv is None else (new_kv[:3], new_kv[3:])
    q, qi, kx, kp, vp, kip, k16, vt, ki16 = project(0, n_prompt, tm_p, True, None, prev_p)
    q, qi, kx, ks, vs, kis = project(n_prompt, n_sample, tm_s, False, (q, qi, kx), prev_s)
    return q, qi, kx, (kp, vp, kip, ks, vs, kis), k16, vt, ki16


def _matmul_res_kernel(a_ref, w_ref, r_ref, o_ref):
    o_ref[...] = r_ref[...] + jnp.dot(a_ref[...], w_ref[...].astype(BF16), preferred_element_type=F32)


def matmul_residual(a, w, res, *, layer, tn=512):
    m, k = a.shape
    n = w.shape[2]
    tm = _row_tile(m)
    assert n % tn == 0
    return pl.pallas_call(
        _matmul_res_kernel,
        grid=(m // tm, n // tn),
        in_specs=[
            pl.BlockSpec((tm, k), lambda i, j: (i, 0)),
            pl.BlockSpec((None, k, tn), lambda i, j: (layer, 0, j)),
            pl.BlockSpec((tm, tn), lambda i, j: (i, j)),
        ],
        out_specs=pl.BlockSpec((tm, tn), lambda i, j: (i, j)),
        out_shape=jax.ShapeDtypeStruct((m, n), F32),
        compiler_params=_params("parallel", "parallel"),
    )(a, w, res)


def _mlp_kernel(x_ref, g_ref, wu_ref, wd_ref, gf_ref, o_ref, h_ref, *, final_norm):
    @pl.when(pl.program_id(1) == 0)
    def _():
        x = x_ref[...]
        h_ref[...] = _rms(x, g_ref[...]).astype(BF16)
        o_ref[...] = x

    u = jnp.maximum(jnp.dot(h_ref[...], wu_ref[...].astype(BF16), preferred_element_type=F32), 0.0)
    o_ref[...] += jnp.dot((u * u).astype(BF16), wd_ref[...].astype(BF16), preferred_element_type=F32)

    if final_norm:
        @pl.when(pl.program_id(1) == pl.num_programs(1) - 1)
        def _():
            o_ref[...] = _rms(o_ref[...], gf_ref[...])


def mlp_residual(x, g, w_up, w_down, *, layer, g_final=None, tf=512):
    m, d = x.shape
    f = w_up.shape[2]
    tm = _row_tile(m)
    assert f % tf == 0
    final_norm = g_final is not None
    gf = (g_final if final_norm else g).reshape(1, d)
    return pl.pallas_call(
        functools.partial(_mlp_kernel, final_norm=final_norm),
        grid=(m // tm, f // tf),
        in_specs=[
            pl.BlockSpec((tm, d), lambda i, j: (i, 0), **_ONE_BUFFER),
            pl.BlockSpec((1, d), lambda i, j: (0, 0)),
            pl.BlockSpec((None, d, tf), lambda i, j: (layer, 0, j)),
            pl.BlockSpec((None, tf, d), lambda i, j: (layer, j, 0)),
            pl.BlockSpec((1, d), lambda i, j: (0, 0)),
        ],
        out_specs=pl.BlockSpec((tm, d), lambda i, j: (i, 0), **_ONE_BUFFER),
        out_shape=jax.ShapeDtypeStruct((m, d), F32),
        scratch_shapes=[pltpu.VMEM((tm, d), BF16)],
        compiler_params=_params("parallel", "arbitrary"),
    )(x, g.reshape(1, d), w_up, w_down, gf)


def _attn_kernel(q_ref, qi_ref, kx_ref, k_ref, vt_ref, ki_ref, *rest, tq, tk, q_pos0, k_top, n_kb_max):
    o_ref, key_ref, m_ref, l_ref, acc_ref, kn_ref, p_ref = rest[-7:]
    @pl.when(pl.program_id(1) == 0)
    def _():
        def kn_body(j, best):
            kf = k_ref[0, pl.ds(pl.multiple_of(j * tk, tk), tk), :].astype(F32)
            sq = kf * kf
            for g in range(ATTN_KV_HEADS):
                n2 = jnp.sum(sq[:, g * ATTN_HEAD_DIM:(g + 1) * ATTN_HEAD_DIM], axis=1, keepdims=True)
                best = jnp.maximum(best, jnp.max(n2, axis=0, keepdims=True))
            return best

        kn_ref[...] = jnp.broadcast_to(lax.fori_loop(0, n_kb_max, kn_body, jnp.zeros((1, 1), F32)), kn_ref.shape)

    q_start = q_pos0 + pl.program_id(1) * tq
    n_adm = ((q_start + tq - 1) // CHUNK + 1) * CHUNK
    n_kb = jnp.minimum((n_adm + tk - 1) // tk, n_kb_max)

    k_row = lax.broadcasted_iota(jnp.int32, (tk, tq), 0)
    k_lim = ((q_start + lax.broadcasted_iota(jnp.int32, (tk, tq), 1)) // CHUNK + 1) * CHUNK

    w_t = kx_ref[0].T[IDX_HEAD_DIM:IDX_HEAD_DIM + IDX_HEADS, :] * (IDX_HEADS ** -0.5 * IDX_HEAD_DIM ** -0.5)
    qi = qi_ref[0].reshape(IDX_HEADS * tq, IDX_HEAD_DIM)

    def score_body(j, carry):
        kib = ki_ref[0, pl.ds(pl.multiple_of(j * tk, tk), tk), :]
        d = lax.dot_general(kib, qi, _NT, preferred_element_type=F32)
        sc = jnp.zeros((tk, tq), F32)
        for h in range(IDX_HEADS):
            sc = sc + w_t[h:h + 1, :] * jnp.maximum(d[:, h * tq:(h + 1) * tq], 0.0)
        bits = pltpu.bitcast(sc, jnp.int32)
        key = bits ^ ((bits >> 31) & jnp.int32(0x7FFFFFFF))
        key_ref[j] = jnp.where(k_row < k_lim - j * tk, key, INT_MIN)
        return carry

    lax.fori_loop(0, n_kb, score_body, 0)

    rows_acc = 8 * SUBLANES

    def count_where(pred):
        def body(j, cnt):
            blk = key_ref[j]
            hit = pred(blk, j)
            for r in range(tk // rows_acc):
                cnt = jnp.where(hit[r * rows_acc:(r + 1) * rows_acc], cnt + 1.0, cnt)
            return cnt
        cnt = lax.fori_loop(0, n_kb, body, jnp.zeros((rows_acc, tq), F32))
        return jnp.sum(cnt, axis=0, keepdims=True)

    def bit_body(p, state):
        thr, cnt_thr = state
        cand = thr + jnp.left_shift(jnp.int32(1), 31 - p)
        cnt = count_where(lambda blk, j: blk >= cand)
        keep = cnt >= float(k_top)
        return jnp.where(keep, cand, thr), jnp.where(keep, cnt, cnt_thr)

    def search_bits(lo, hi, state):
        return lax.fori_loop(lo, hi, bit_body, state)

    def settled(state):
        return jnp.min(jnp.where(state[1] == float(k_top), 1.0, 0.0)) > 0.0

    state = search_bits(0, 22, (jnp.full((1, tq), INT_MIN, jnp.int32), jnp.full((1, tq), 2.0 ** 30, F32)))
    for lo in range(22, 32, 2):
        state = lax.cond(settled(state), lambda s: s, functools.partial(search_bits, lo, lo + 2), state)
    thr_raw, cnt_thr = state
    thr = jnp.maximum(thr_raw, INT_MIN + 1)

    tied = jnp.logical_and(cnt_thr > float(k_top), thr_raw > INT_MIN)
    n_tied = jnp.max(jnp.where(tied, 1.0, 0.0))

    @pl.when(n_tied > 0.0)
    def _():
        need = float(k_top) - count_where(lambda blk, j: blk > thr)
        n_bits = max(1, int(n_kb_max * tk - 1).bit_length())

        def pos_body(b, last):
            step = jnp.left_shift(jnp.int32(1), n_bits - 1 - b)
            cand = last + step - 1
            got = count_where(lambda blk, j: jnp.logical_and(blk == thr, k_row + j * tk <= cand))
            return jnp.where(got < need, last + step, last)

        last = lax.fori_loop(0, n_bits, pos_body, jnp.zeros((1, tq), jnp.int32))

        def demote_body(j, carry):
            blk = key_ref[j]
            drop = jnp.logical_and(blk == thr, k_row + j * tk > last)
            key_ref[j] = jnp.where(drop, thr - 1, blk)
            return carry

        lax.fori_loop(0, n_kb, demote_body, 0)

    m_ref[...] = jnp.full(m_ref.shape, NEG_BIG, F32)
    l_ref[...] = jnp.zeros(l_ref.shape, F32)
    acc_ref[...] = jnp.zeros(acc_ref.shape, F32)
    c_exp = ATTN_HEAD_DIM ** -0.5 * LOG2_E
    cols = ATTN_GROUP * tq

    def logits_t(j, g):
        k0 = pl.multiple_of(j * tk, tk)
        kb = k_ref[0, pl.ds(k0, tk), g * ATTN_HEAD_DIM:(g + 1) * ATTN_HEAD_DIM]
        vtb = vt_ref[0, j, g * ATTN_HEAD_DIM:(g + 1) * ATTN_HEAD_DIM, :]
        qg = q_ref[0, g * ATTN_GROUP:(g + 1) * ATTN_GROUP].reshape(cols, ATTN_HEAD_DIM)
        return lax.dot_general(kb, qg, _NT, preferred_element_type=F32), vtb

    qf = q_ref[0].reshape(ATTN_HEADS * tq, ATTN_HEAD_DIM).astype(F32)
    qn2 = lax.dot_general(jnp.ones((SUBLANES, ATTN_HEAD_DIM), BF16), (qf * qf).astype(BF16), _NT,
                          preferred_element_type=F32)
    qn2_max = qn2[0:1, 0:tq]
    for h in range(1, ATTN_HEADS):
        qn2_max = jnp.maximum(qn2_max, qn2[0:1, h * tq:(h + 1) * tq])
    m_bound = jnp.sqrt(qn2_max * kn_ref[...]) * 1.02
    fixed_shift_ok = jnp.max(m_bound) * (2.0 * c_exp) < 100.0

    @pl.when(fixed_shift_ok)
    def _():
        shift = -c_exp * m_bound
        ones = jnp.ones((2 * SUBLANES, tk), BF16)

        def accumulate(j):
            for g in range(ATTN_KV_HEADS):
                vtb = vt_ref[0, j, g * ATTN_HEAD_DIM:(g + 1) * ATTN_HEAD_DIM, :]
                acc_ref[g] += jnp.dot(vtb, p_ref[g], preferred_element_type=F32)
                l_ref[g] += jnp.dot(ones, p_ref[g], preferred_element_type=F32)[0:1]

        def probabilities(j):
            bias = jnp.where(key_ref[j] >= thr, shift, NEG_BIG)
            bias = jnp.concatenate([bias] * ATTN_GROUP, axis=1)
            for g in range(ATTN_KV_HEADS):
                s, _ = logits_t(j, g)
                p_ref[g] = jnp.exp2(s * c_exp + bias).astype(BF16)

        probabilities(0)

        def body(j, carry):
            accumulate(j - 1)
            probabilities(j)
            return carry

        lax.fori_loop(1, n_kb, body, 0)
        accumulate(n_kb - 1)

    @pl.when(jnp.logical_not(fixed_shift_ok))
    def _():
        def body(j, carry):
            bias = jnp.where(key_ref[j] >= thr, 0.0, NEG_BIG)
            bias = jnp.concatenate([bias] * ATTN_GROUP, axis=1)
            for g in range(ATTN_KV_HEADS):
                s, vtb = logits_t(j, g)
                s = s + bias
                m_old = m_ref[g]
                m_new = jnp.maximum(m_old, jnp.max(s, axis=0, keepdims=True))
                alpha = jnp.exp2((m_old - m_new) * c_exp)
                p = jnp.exp2((s - m_new) * c_exp)
                l_ref[g] = alpha * l_ref[g] + jnp.sum(p, axis=0, keepdims=True)
                acc_ref[g] = alpha * acc_ref[g] + jnp.dot(vtb, p.astype(BF16), preferred_element_type=F32)
                m_ref[g] = m_new
            return carry

        lax.fori_loop(0, n_kb, body, 0)

    n_out = o_ref.shape[1]
    for g in range(ATTN_KV_HEADS):
        out_t = acc_ref[g] / l_ref[g]
        for hh in range(ATTN_GROUP):
            h = g * ATTN_GROUP + hh
            o_ref[0, :, h * ATTN_HEAD_DIM:(h + 1) * ATTN_HEAD_DIM] = (
                out_t[:, hh * tq:(hh + 1) * tq].T[:n_out].astype(o_ref.dtype))


def _rows_out_spec(width, n_steps, out_rows, into, dtype):
    b, rows = out_rows[0], out_rows[1]
    if into is None:
        spec = pl.BlockSpec((1, rows, width), lambda bi, i: (bi, i, 0))
        return spec, jax.ShapeDtypeStruct((b, out_rows[2], width), dtype), [], []
    buf, first_row, rows = into
    assert first_row % rows == 0 and buf.shape[0] == 1 and buf.dtype == dtype
    blk0 = first_row // rows
    spec = pl.BlockSpec((1, rows, width), lambda bi, i: (0, blk0 + bi * n_steps + i, 0))
    return spec, jax.ShapeDtypeStruct(buf.shape, dtype), [pl.BlockSpec(memory_space=pl.ANY)], [buf]


def dsa_attention(q, qi, kx, k, v, ki, *, n_q, tq, tk, q_pos0, k_top, v_is_transposed=False,
                  out_rows=None, into=None):
    b = q.shape[0]
    s = k.shape[1]
    assert n_q % tq == 0 and s % tk == 0 and tq == LANES
    n_kb_max = s // tk
    cols = ATTN_GROUP * tq
    vt = v if v_is_transposed else v.reshape(b, n_kb_max, tk, ATTN_KV_DIM).transpose(0, 1, 3, 2)
    assert vt.shape == (b, n_kb_max, ATTN_KV_DIM, tk)
    kern = functools.partial(_attn_kernel, tq=tq, tk=tk, q_pos0=q_pos0, k_top=k_top, n_kb_max=n_kb_max)
    resident = dict(pipeline_mode=pl.Buffered(1))
    out_spec, out_shape, buf_specs, bufs = _rows_out_spec(
        ATTN_Q_DIM, n_q // tq, (b, tq, out_rows or n_q), into, BF16)
    return pl.pallas_call(
        kern,
        grid=(b, n_q // tq),
        in_specs=[
            pl.BlockSpec((1, ATTN_HEADS, tq, ATTN_HEAD_DIM), lambda bi, i: (bi, 0, i, 0)),
            pl.BlockSpec((1, IDX_HEADS, tq, IDX_HEAD_DIM), lambda bi, i: (bi, 0, i, 0)),
            pl.BlockSpec((1, tq, LANES), lambda bi, i: (bi, i, 0)),
            pl.BlockSpec((1, s, ATTN_KV_DIM), lambda bi, i: (bi, 0, 0), **resident),
            pl.BlockSpec((1, n_kb_max, ATTN_KV_DIM, tk), lambda bi, i: (bi, 0, 0, 0), **resident),
            pl.BlockSpec((1, s, IDX_HEAD_DIM), lambda bi, i: (bi, 0, 0), **resident),
        ] + buf_specs,
        out_specs=out_spec,
        out_shape=out_shape,
        input_output_aliases={6: 0} if bufs else {},
        scratch_shapes=[
            pltpu.VMEM((n_kb_max, tk, tq), jnp.int32),
            pltpu.VMEM((ATTN_KV_HEADS, 1, cols), F32),
            pltpu.VMEM((ATTN_KV_HEADS, 1, cols), F32),
            pltpu.VMEM((ATTN_KV_HEADS, ATTN_HEAD_DIM, cols), F32),
            pltpu.VMEM((1, tq), F32),
            pltpu.VMEM((ATTN_KV_HEADS, tk, cols), BF16),
        ],
        compiler_params=_params("parallel", "arbitrary"),
    )(q, qi, kx, k, vt, ki, *bufs)


_LEVELS = (32, 16, 8, 4, 2, 1)


def _hgrn_tables():
    c = CHUNK
    t = np.arange(c)[:, None]
    u = np.arange(c)[None, :]
    mats = [(u <= t), (u > t)]
    masks = []
    for w in _LEVELS:
        r = (t // (2 * w)) * (2 * w) + w - 1
        upper = (t % (2 * w)) >= w
        mats.append((upper & (u > r) & (u <= t)) | ((~upper) & (u > t) & (u <= r)))
        s = np.arange(c)[None, :]
        masks.append(((t // (2 * w)) == (s // (2 * w))) & upper & ((s % (2 * w)) < w))
    masks.append(t == np.arange(c)[None, :])
    table = np.concatenate(mats, axis=0).astype(np.float32)
    return np.concatenate([table] * 3, axis=1), np.stack(masks).astype(np.float32)


def _hgrn_kernel(q_ref, f_ref, i_ref, gt_ref, lb_ref, gn_ref, tab_ref, msk_ref, s0_ref, *rest, layer):
    o_ref, sfin_ref, st_ref = rest[-3:]
    c = CHUNK
    dh = HGRN_HEAD_DIM
    step = pl.program_id(1)

    @pl.when(step == 0)
    def _():
        for h in range(HGRN_HEADS):
            st_ref[h] = s0_ref[0, h].T

    lb_all = lb_ref[...]
    e = jnp.exp(lb_all - jnp.max(lb_all, axis=0, keepdims=True))
    p_lb = e / jnp.sum(e, axis=0, keepdims=True)
    lower = jnp.sum(p_lb[:layer + 1], axis=0, keepdims=True) - p_lb[0:1]

    def chunk_operands(ci):
        rows = slice(ci * c, (ci + 1) * c)
        f = lower + (1.0 - lower) * jax.nn.sigmoid(f_ref[0, rows])
        kk = 1.0 - f
        lf = jnp.log2(f)
        lf_hi = lf.astype(BF16)
        r1 = lf - lf_hi.astype(F32)
        lf_mid = r1.astype(BF16)
        lf_lo = (r1 - lf_mid.astype(F32)).astype(BF16)
        lf3 = jnp.concatenate([lf_hi, lf_mid, lf_lo], axis=0)
        ex = jnp.exp2(jnp.dot(tab_ref[...], lf3, preferred_element_type=F32))
        qq = q_ref[0, rows]
        vv = i_ref[0, rows]
        ops = dict(q16=qq.astype(BF16), k16=kk.astype(BF16), v16=vv.astype(BF16), vv=vv)
        ops["q_lv"] = [(qq * ex[(2 + li) * c:(3 + li) * c]).astype(BF16) for li in range(len(_LEVELS))]
        ops["k_lv"] = [(kk * ex[(2 + li) * c:(3 + li) * c]).astype(BF16) for li in range(len(_LEVELS))]
        ops["q_in"] = (qq * ex[0:c]).astype(BF16)
        ops["k_out"] = (kk * ex[c:2 * c]).astype(BF16)
        ops["decay"] = ex[c - 1:c]
        ops["gate"] = jax.nn.sigmoid(gt_ref[0, rows])
        return ops

    def chunk_outputs(ci, ops):
        rows = slice(ci * c, (ci + 1) * c)
        scores = []
        for h in range(HGRN_HEADS):
            sl = slice(h * dh, (h + 1) * dh)
            s_h = msk_ref[len(_LEVELS)] * lax.dot_general(
                ops["q16"][:, sl], ops["k16"][:, sl], _NT, preferred_element_type=F32)
            for li in range(len(_LEVELS)):
                s_h = s_h + msk_ref[li] * lax.dot_general(
                    ops["q_lv"][li][:, sl], ops["k_lv"][li][:, sl], _NT, preferred_element_type=F32)
            scores.append(s_h.astype(BF16))
        for h in range(HGRN_HEADS):
            sl = slice(h * dh, (h + 1) * dh)
            st = st_ref[h]
            o_h = jnp.dot(scores[h], ops["v16"][:, sl], preferred_element_type=F32)
            o_h = o_h + lax.dot_general(ops["q_in"][:, sl], st.astype(BF16), _NT, preferred_element_type=F32)
            st_ref[h] = st * ops["decay"][:, sl] + jnp.dot(
                ops["vv"][:, sl].T.astype(BF16), ops["k_out"][:, sl], preferred_element_type=F32)
            o_ref[0, rows, sl] = (_rms(o_h, gn_ref[:, sl]) * ops["gate"][:, sl]).astype(o_ref.dtype)

    n_sub = q_ref.shape[1] // c
    ops = chunk_operands(0)
    for ci in range(n_sub):
        ops_next = chunk_operands(ci + 1) if ci + 1 < n_sub else None
        chunk_outputs(ci, ops)
        ops = ops_next

    @pl.when(step == pl.num_programs(1) - 1)
    def _():
        for h in range(HGRN_HEADS):
            sfin_ref[0, h] = st_ref[h].T


def hgrn_scan(p, lb, gnorm, s0, *, layer, n_rows=None, out_rows=None, into=None, states=None):
    b = p.shape[0]
    l = p.shape[1] if n_rows is None else n_rows
    d = D_MODEL
    c = max(r for r in (4 * CHUNK, 2 * CHUNK, CHUNK) if l % r == 0)
    tab, msk = _hgrn_tables()
    kern = functools.partial(_hgrn_kernel, layer=layer)
    col = lambda j: pl.BlockSpec((1, c, d), lambda bi, t, j=j: (bi, t, j))
    state_spec = pl.BlockSpec((1, HGRN_HEADS, HGRN_HEAD_DIM, HGRN_HEAD_DIM), lambda bi, t: (bi, 0, 0, 0))
    out_spec, out_shape, buf_specs, bufs = _rows_out_spec(d, l // c, (b, c, out_rows or l), into, BF16)
    aliases = {9: 0} if bufs else {}
    states_spec = pl.BlockSpec((None,) + state_spec.block_shape, lambda bi, t: (layer, bi, 0, 0, 0))
    if states is not None:
        aliases[9 + len(bufs)] = 1
        buf_specs = buf_specs + [pl.BlockSpec(memory_space=pl.ANY)]
        bufs = bufs + [states]
    return pl.pallas_call(
        kern,
        grid=(b, l // c),
        in_specs=[
            col(0), col(1), col(2), col(3),
            pl.BlockSpec(lb.shape, lambda bi, t: (0, 0)),
            pl.BlockSpec((1, d), lambda bi, t: (0, 0)),
            pl.BlockSpec(tab.shape, lambda bi, t: (0, 0)),
            pl.BlockSpec(msk.shape, lambda bi, t: (0, 0, 0)),
            state_spec,
        ] + buf_specs,
        out_specs=[out_spec, states_spec],
        out_shape=[out_shape, jax.ShapeDtypeStruct((lb.shape[0],) + s0.shape, F32)],
        input_output_aliases=aliases,
        scratch_shapes=[pltpu.VMEM((HGRN_HEADS, HGRN_HEAD_DIM, HGRN_HEAD_DIM), F32)],
        compiler_params=_params("parallel", "arbitrary"),
    )(p, p, p, p, lb, gnorm.reshape(1, d), jnp.asarray(tab, BF16), jnp.asarray(msk), s0, *bufs)


def _attn_layer(x, g, w_in, w_out, layer, new_kv, cache_k, cache_v, cache_kidx, n_p, b_s, t_s):
    q, qi, kx, new_kv, k16, vt, ki16 = attn_in_proj(x, g, w_in, layer=layer, n_prompt=n_p, new_kv=new_kv)
    k_new, v_new, ki_new = new_kv[3][layer], new_kv[4][layer], new_kv[5][layer]
    past = cache_k.shape[1]
    tq = LANES

    o_p = dsa_attention(
        q[None], qi[None], kx[None], k16[None], vt[None], ki16[None], v_is_transposed=True,
        n_q=n_p, tq=tq, tk=vt.shape[2], q_pos0=0, k_top=min(TOPK_MAX, n_p // 4),
        out_rows=x.shape[0])

    tk_s = 256
    s_all = past + t_s
    n_adm_pad = ((past + tq - 1) // CHUNK + 1) * CHUNK
    s_pad = -(-max(s_all, n_adm_pad) // tk_s) * tk_s

    def with_cache(cache, new, width):
        full = jnp.zeros((b_s, s_pad, width), BF16)
        full = lax.dynamic_update_slice(full, cache.reshape(b_s, past, width).astype(BF16), (0, 0, 0))
        return lax.dynamic_update_slice(full, new.astype(BF16).reshape(b_s, t_s, width), (0, past, 0))

    def sample_heads(a, heads, dim):
        a = a[:, n_p:].reshape(heads, b_s, t_s, dim).transpose(1, 0, 2, 3)
        return jnp.pad(a, ((0, 0), (0, 0), (0, tq - t_s), (0, 0)))

    kx_s = jnp.pad(kx[n_p:].reshape(b_s, t_s, LANES), ((0, 0), (0, tq - t_s), (0, 0)))
    o_s = dsa_attention(
        sample_heads(q, ATTN_HEADS, ATTN_HEAD_DIM), sample_heads(qi, IDX_HEADS, IDX_HEAD_DIM), kx_s,
        with_cache(cache_k, k_new, ATTN_KV_DIM), with_cache(cache_v, v_new, ATTN_KV_DIM),
        with_cache(cache_kidx, ki_new, IDX_HEAD_DIM),
        n_q=tq, tq=tq, tk=tk_s, q_pos0=past, k_top=min(TOPK_MAX, s_all // 4),
        into=(o_p, n_p, t_s))

    x = matmul_residual(o_s[0], w_out, x, layer=layer)
    return x, new_kv


def _rec_layer(x, g, w_in, w_out, gnorm, rec_lb, state, layer, new_states, n_p, b_s, t_s):
    p = norm_matmul(x, g, w_in, layer=layer)
    s0_p = jnp.zeros((1,) + state.shape[1:], F32)
    o_p, s_p = hgrn_scan(p[None], rec_lb, gnorm, s0_p, layer=layer, n_rows=n_p, out_rows=x.shape[0],
                         states=new_states[0])
    o_s, s_s = hgrn_scan(p[n_p:].reshape(b_s, t_s, -1), rec_lb, gnorm, state, layer=layer,
                         into=(o_p, n_p, t_s),
                         states=new_states[1])
    x = matmul_residual(o_s[0], w_out, x, layer=layer)
    return x, (s_p, s_s)


def kernel(x_prompt, x_sample, cache_k, cache_v, cache_kidx, state_s, norm_mix, norm_mlp, norm_final,
           attn_w_in, attn_w_out, rec_w_in, rec_w_out, rec_gnorm, rec_lb, mlp_w_up, mlp_w_down):
    b_p, l_p, d = x_prompt.shape
    b_s, t_s, _ = x_sample.shape
    assert b_p == 1
    n_p = b_p * l_p
    x = jnp.concatenate([x_prompt.reshape(n_p, d), x_sample.reshape(b_s * t_s, d)], axis=0)

    new_states = (None, None)
    new_kv = None
    for layer in range(DEPTH):
        j = layer // N_MIXERS
        if layer % N_MIXERS == 0:
            x, new_kv = _attn_layer(
                x, norm_mix[layer], attn_w_in, attn_w_out, j, new_kv,
                cache_k[j], cache_v[j], cache_kidx[j], n_p, b_s, t_s)
        else:
            x, new_states = _rec_layer(
                x, norm_mix[layer], rec_w_in, rec_w_out, rec_gnorm[j], rec_lb, state_s[j],
                j, new_states, n_p, b_s, t_s)
        x = mlp_residual(x, norm_mlp[layer], mlp_w_up, mlp_w_down, layer=layer,
                         g_final=norm_final if layer == DEPTH - 1 else None)

    y = x
    kp, vp, kip, ks, vs, kis = new_kv
    n_l = kp.shape[0]
    heads = (ATTN_KV_HEADS, ATTN_HEAD_DIM)
    return (y[:n_p].reshape(b_p, l_p, d), y[n_p:].reshape(b_s, t_s, d),
            kp.reshape(n_l, b_p, l_p, *heads), vp.reshape(n_l, b_p, l_p, *heads),
            kip.reshape(n_l, b_p, l_p, IDX_HEAD_DIM), new_states[0],
            ks.reshape(n_l, b_s, t_s, *heads), vs.reshape(n_l, b_s, t_s, *heads),
            kis.reshape(n_l, b_s, t_s, IDX_HEAD_DIM), new_states[1])
```

```python
import functools
import math

import numpy as np
import jax
import jax.numpy as jnp
from jax import lax
from jax.experimental import pallas as pl
from jax.experimental.pallas import tpu as pltpu

D_MODEL = 2048
DEPTH = 4
CHUNK = 64
N_MIXERS = 2
ATTN_HEADS = 16
ATTN_KV_HEADS = 4
ATTN_HEAD_DIM = D_MODEL // ATTN_HEADS
ATTN_GROUP = ATTN_HEADS // ATTN_KV_HEADS
IDX_HEADS = 16
IDX_HEAD_DIM = 64
TOPK_MAX = 256
HGRN_HEAD_DIM = 128
HGRN_HEADS = D_MODEL // HGRN_HEAD_DIM
D_FF = 4 * D_MODEL
RMS_EPS = 1e-6
ATTN_Q_DIM = ATTN_HEADS * ATTN_HEAD_DIM
ATTN_KV_DIM = ATTN_KV_HEADS * ATTN_HEAD_DIM
IDX_Q_DIM = IDX_HEADS * IDX_HEAD_DIM

LANES = 128
SUBLANES = 8
VMEM_LIMIT = 56 * 1024 * 1024

F32 = jnp.float32
BF16 = jnp.bfloat16
INT_MIN = np.int32(-(2 ** 31))
NEG_BIG = -1e30
LOG2_E = 1.4426950408889634

_NT = (((1,), (1,)), ((), ()))


def _params(*sem):
    return pltpu.CompilerParams(dimension_semantics=sem, vmem_limit_bytes=VMEM_LIMIT)


def _rms(x, g):
    ms = jnp.mean(x * x, axis=-1, keepdims=True)
    return (x * lax.rsqrt(ms + RMS_EPS)) * g


def _norm_matmul_kernel(x_ref, g_ref, w_ref, o_ref, h_ref):
    @pl.when(pl.program_id(1) == 0)
    def _():
        h_ref[...] = _rms(x_ref[...], g_ref[...]).astype(BF16)

    o_ref[...] = jnp.dot(h_ref[...], w_ref[...].astype(BF16), preferred_element_type=F32)


def _row_tile(m, cap=1088):
    return max(t for t in range(16, cap + 1, 16) if m % t == 0)


_ONE_BUFFER = dict(pipeline_mode=pl.Buffered(1))


def norm_matmul(x, g, w, *, layer, tn=512):
    m, d = x.shape
    n = w.shape[2]
    tm = _row_tile(m)
    assert n % tn == 0
    return pl.pallas_call(
        _norm_matmul_kernel,
        grid=(m // tm, n // tn),
        in_specs=[
            pl.BlockSpec((tm, d), lambda i, j: (i, 0), **_ONE_BUFFER),
            pl.BlockSpec((1, d), lambda i, j: (0, 0)),
            pl.BlockSpec((None, d, tn), lambda i, j: (layer, 0, j)),
        ],
        out_specs=pl.BlockSpec((tm, tn), lambda i, j: (i, j)),
        out_shape=jax.ShapeDtypeStruct((m, n), F32),
        scratch_shapes=[pltpu.VMEM((tm, d), BF16)],
        compiler_params=_params("parallel", "arbitrary"),
    )(x, g.reshape(1, d), w)


def _round_bf16_kernel(w_ref, o_ref):
    o_ref[...] = w_ref[...].astype(BF16)


def round_bf16(w, *, layer, n_cols, tn=512):
    d = w.shape[1]
    assert n_cols % tn == 0
    return pl.pallas_call(
        _round_bf16_kernel,
        grid=(n_cols // tn,),
        in_specs=[pl.BlockSpec((None, d, tn), lambda j: (layer, 0, j))],
        out_specs=pl.BlockSpec((d, tn), lambda j: (0, j)),
        out_shape=jax.ShapeDtypeStruct((d, n_cols), BF16),
        compiler_params=_params("parallel"),
    )(w)


def _attn_in_proj_kernel(x_ref, g_ref, w_ref, wkx_ref, *rest, n_q, n_qi, prompt, tk):
    n_out = 9 if prompt else 6
    outs, h_ref = rest[-(n_out + 1):-1], rest[-1]
    q_ref, qi_ref, kx_ref, k_ref, v_ref, ki_ref = outs[:6]
    j = pl.program_id(1)

    @pl.when(j == 0)
    def _():
        h_ref[...] = _rms(x_ref[...], g_ref[...]).astype(BF16)

    def tile():
        return jnp.dot(h_ref[...], w_ref[...], preferred_element_type=F32)

    @pl.when(j < n_q)
    def _():
        res = tile()
        for hh in range(q_ref.shape[0]):
            q_ref[hh] = res[:, hh * ATTN_HEAD_DIM:(hh + 1) * ATTN_HEAD_DIM].astype(q_ref.dtype)

    @pl.when(j == n_q)
    def _():
        res = tile()
        k_ref[...] = res
        if prompt:
            outs[6][...] = res.astype(BF16)

    @pl.when(j == n_q + 1)
    def _():
        res = tile()
        v_ref[...] = res
        if prompt:
            for t in range(outs[7].shape[0]):
                outs[7][t] = res[t * tk:(t + 1) * tk].T.astype(BF16)

    @pl.when(jnp.logical_and(j >= n_q + 2, j < n_q + 2 + n_qi))
    def _():
        res = tile()
        for hh in range(qi_ref.shape[0]):
            qi_ref[hh] = res[:, hh * IDX_HEAD_DIM:(hh + 1) * IDX_HEAD_DIM].astype(qi_ref.dtype)

    @pl.when(j == n_q + 2 + n_qi)
    def _():
        res = jnp.dot(h_ref[...], wkx_ref[...].astype(BF16), preferred_element_type=F32)
        kx_ref[...] = res
        ki_ref[...] = res[:, :IDX_HEAD_DIM]
        if prompt:
            outs[8][...] = res[:, :IDX_HEAD_DIM].astype(BF16)


def attn_in_proj(x, g, w_in, *, layer, n_prompt, new_kv=None):
    x_parts = x if isinstance(x, tuple) else None
    m = sum(part.shape[0] for part in x_parts) if x_parts else x.shape[0]
    d = w_in.shape[1]
    tn = tk = ATTN_KV_DIM
    n_layers = w_in.shape[0]
    n_q, n_qi = ATTN_Q_DIM // tn, IDX_Q_DIM // tn
    n_main = n_q + 2 + n_qi
    tail = w_in.shape[2] - n_main * tn
    w_kx = jnp.pad(w_in[layer:layer + 1, :, n_main * tn:], ((0, 0), (0, 0), (0, LANES - tail)))
    w_main = round_bf16(w_in, layer=layer, n_cols=n_main * tn)
    clip = lambda v, hi: jnp.minimum(jnp.maximum(v, 0), hi)

    def project(row0, n_rows, tm, prompt, shared, slabs):
        assert row0 % tm == 0 and n_rows % tm == 0 and (tm % tk == 0 or not prompt)
        r0 = row0 // tm
        x_rows, x_r0 = (x, r0) if x_parts is None else (x_parts[0 if prompt else 1], 0)
        bufs = list(shared or ()) + list(slabs or ())
        aliases = {}
        if shared:
            aliases.update({4 + n: n for n in range(3)})
        if slabs:
            aliases.update({4 + len(shared or ()) + n: 3 + n for n in range(3)})
        slab = lambda width: jax.ShapeDtypeStruct((n_layers, n_rows, width), F32)
        out_specs = [
            pl.BlockSpec((tn // ATTN_HEAD_DIM, tm, ATTN_HEAD_DIM), lambda i, j: (clip(j, n_q - 1), r0 + i, 0)),
            pl.BlockSpec((tn // IDX_HEAD_DIM, tm, IDX_HEAD_DIM),
                         lambda i, j: (clip(j - n_q - 2, n_qi - 1), r0 + i, 0)),
            pl.BlockSpec((tm, LANES), lambda i, j: (r0 + i, 0)),
            pl.BlockSpec((None, tm, tn), lambda i, j: (layer, i, 0)),
            pl.BlockSpec((None, tm, tn), lambda i, j: (layer, i, 0)),
            pl.BlockSpec((None, tm, IDX_HEAD_DIM), lambda i, j: (layer, i, 0)),
        ]
        out_shape = [
            jax.ShapeDtypeStruct((ATTN_HEADS, m, ATTN_HEAD_DIM), BF16),
            jax.ShapeDtypeStruct((IDX_HEADS, m, IDX_HEAD_DIM), BF16),
            jax.ShapeDtypeStruct((m, LANES), F32),
            slab(tn), slab(tn), slab(IDX_HEAD_DIM),
        ]
        if prompt:
            out_specs += [
                pl.BlockSpec((tm, tn), lambda i, j: (i, 0)),
                pl.BlockSpec((tm // tk, tn, tk), lambda i, j: (i, 0, 0)),
                pl.BlockSpec((tm, IDX_HEAD_DIM), lambda i, j: (i, 0)),
            ]
            out_shape += [
                jax.ShapeDtypeStruct((n_rows, tn), BF16),
                jax.ShapeDtypeStruct((n_rows // tk, tn, tk), BF16),
                jax.ShapeDtypeStruct((n_rows, IDX_HEAD_DIM), BF16),
            ]
        return pl.pallas_call(
            functools.partial(_attn_in_proj_kernel, n_q=n_q, n_qi=n_qi, prompt=prompt, tk=tk),
            grid=(n_rows // tm, n_main + 1),
            in_specs=[
                pl.BlockSpec((tm, d), lambda i, j: (x_r0 + i, 0), **_ONE_BUFFER),
                pl.BlockSpec((1, d), lambda i, j: (0, 0)),
                pl.BlockSpec((d, tn), lambda i, j: (0, jnp.minimum(j, n_main - 1))),
                pl.BlockSpec((None, d, LANES), lambda i, j: (0, 0, 0)),
            ] + [pl.BlockSpec(memory_space=pl.ANY)] * len(bufs),
            out_specs=out_specs,
            out_shape=out_shape,
            input_output_aliases=aliases,
            scratch_shapes=[pltpu.VMEM((tm, d), BF16)],
            compiler_params=_params("parallel", "arbitrary"),
        )(x_rows, g.reshape(1, d), w_main, w_kx, *bufs)

    n_sample = m - n_prompt
    tm_p = max(t for t in (2 * tk, tk) if n_prompt % t == 0)
    tm_s = _row_tile(n_sample, cap=tk)
    prev_p, prev_s = (None, None) if new_kv is None else (new_kv[:3], new_kv[3:])
    q, qi, kx, kp, vp, kip, k16, vt, ki16 = project(0, n_prompt, tm_p, True, None, prev_p)
    q, qi, kx, ks, vs, kis = project(n_prompt, n_sample, tm_s, False, (q, qi, kx), prev_s)
    return q, qi, kx, (kp, vp, kip, ks, vs, kis), k16, vt, ki16


def _matmul_res_kernel(a_ref, w_ref, r_ref, *rest):
    o_ref = rest[-1]
    o_ref[...] = r_ref[...] + jnp.dot(a_ref[...], w_ref[...].astype(BF16), preferred_element_type=F32)


def matmul_residual(a, w, res, *, layer, tn=512):
    m, k = a.shape
    n = w.shape[2]
    assert n % tn == 0

    def rows(res_rows, res_r0, row0, n_rows, tm, out_buf):
        assert row0 % tm == 0 and n_rows % tm == 0
        r0 = row0 // tm
        bufs = [] if out_buf is None else [out_buf]
        return pl.pallas_call(
            _matmul_res_kernel,
            grid=(n_rows // tm, n // tn),
            in_specs=[
                pl.BlockSpec((tm, k), lambda i, j: (r0 + i, 0)),
                pl.BlockSpec((None, k, tn), lambda i, j: (layer, 0, j)),
                pl.BlockSpec((tm, tn), lambda i, j: (res_r0 + i, j)),
            ] + [pl.BlockSpec(memory_space=pl.ANY)] * len(bufs),
            out_specs=pl.BlockSpec((tm, tn), lambda i, j: (r0 + i, j)),
            out_shape=jax.ShapeDtypeStruct((m, n), F32),
            input_output_aliases={3: 0} if bufs else {},
            compiler_params=_params("parallel", "parallel"),
        )(a, w, res_rows, *bufs)

    if not isinstance(res, tuple):
        return rows(res, 0, 0, m, _row_tile(m), None)
    n_first = res[0].shape[0]
    out = rows(res[0], 0, 0, n_first, _row_tile(n_first), None)
    tm_rest = _row_tile(m - n_first, cap=math.gcd(n_first, m - n_first))
    return rows(res[1], 0, n_first, m - n_first, tm_rest, out)


def _mlp_kernel(x_ref, g_ref, wu_ref, wd_ref, gf_ref, o_ref, h_ref, *, final_norm):
    @pl.when(pl.program_id(1) == 0)
    def _():
        x = x_ref[...]
        h_ref[...] = _rms(x, g_ref[...]).astype(BF16)
        o_ref[...] = x

    u = jnp.maximum(jnp.dot(h_ref[...], wu_ref[...].astype(BF16), preferred_element_type=F32), 0.0)
    o_ref[...] += jnp.dot((u * u).astype(BF16), wd_ref[...].astype(BF16), preferred_element_type=F32)

    if final_norm:
        @pl.when(pl.program_id(1) == pl.num_programs(1) - 1)
        def _():
            o_ref[...] = _rms(o_ref[...], gf_ref[...])


def mlp_residual(x, g, w_up, w_down, *, layer, g_final=None, tf=512):
    m, d = x.shape
    f = w_up.shape[2]
    tm = _row_tile(m)
    assert f % tf == 0
    final_norm = g_final is not None
    gf = (g_final if final_norm else g).reshape(1, d)
    return pl.pallas_call(
        functools.partial(_mlp_kernel, final_norm=final_norm),
        grid=(m // tm, f // tf),
        in_specs=[
            pl.BlockSpec((tm, d), lambda i, j: (i, 0), **_ONE_BUFFER),
            pl.BlockSpec((1, d), lambda i, j: (0, 0)),
            pl.BlockSpec((None, d, tf), lambda i, j: (layer, 0, j)),
            pl.BlockSpec((None, tf, d), lambda i, j: (layer, j, 0)),
            pl.BlockSpec((1, d), lambda i, j: (0, 0)),
        ],
        out_specs=pl.BlockSpec((tm, d), lambda i, j: (i, 0), **_ONE_BUFFER),
        out_shape=jax.ShapeDtypeStruct((m, d), F32),
        scratch_shapes=[pltpu.VMEM((tm, d), BF16)],
        compiler_params=_params("parallel", "arbitrary"),
    )(x, g.reshape(1, d), w_up, w_down, gf)


def _attn_kernel(q_ref, qi_ref, kx_ref, k_ref, vt_ref, ki_ref, *rest, tq, tk, q_pos0, k_top, n_kb_max):
    o_ref, key_ref, m_ref, l_ref, acc_ref, kn_ref, p_ref = rest[-7:]
    @pl.when(pl.program_id(1) == 0)
    def _():
        def kn_body(j, best):
            kf = k_ref[0, pl.ds(pl.multiple_of(j * tk, tk), tk), :].astype(F32)
            sq = kf * kf
            for g in range(ATTN_KV_HEADS):
                n2 = jnp.sum(sq[:, g * ATTN_HEAD_DIM:(g + 1) * ATTN_HEAD_DIM], axis=1, keepdims=True)
                best = jnp.maximum(best, jnp.max(n2, axis=0, keepdims=True))
            return best

        kn_ref[...] = jnp.broadcast_to(lax.fori_loop(0, n_kb_max, kn_body, jnp.zeros((1, 1), F32)), kn_ref.shape)

    q_start = q_pos0 + pl.program_id(1) * tq
    n_adm = ((q_start + tq - 1) // CHUNK + 1) * CHUNK
    n_kb = jnp.minimum((n_adm + tk - 1) // tk, n_kb_max)

    k_row = lax.broadcasted_iota(jnp.int32, (tk, tq), 0)
    k_lim = ((q_start + lax.broadcasted_iota(jnp.int32, (tk, tq), 1)) // CHUNK + 1) * CHUNK

    w_t = kx_ref[0].T[IDX_HEAD_DIM:IDX_HEAD_DIM + IDX_HEADS, :] * (IDX_HEADS ** -0.5 * IDX_HEAD_DIM ** -0.5)
    qi = qi_ref[0].reshape(IDX_HEADS * tq, IDX_HEAD_DIM)

    def score_body(j, carry):
        kib = ki_ref[0, pl.ds(pl.multiple_of(j * tk, tk), tk), :]
        d = lax.dot_general(kib, qi, _NT, preferred_element_type=F32)
        sc = jnp.zeros((tk, tq), F32)
        for h in range(IDX_HEADS):
            sc = sc + w_t[h:h + 1, :] * jnp.maximum(d[:, h * tq:(h + 1) * tq], 0.0)
        bits = pltpu.bitcast(sc, jnp.int32)
        key = bits ^ ((bits >> 31) & jnp.int32(0x7FFFFFFF))
        key_ref[j] = jnp.where(k_row < k_lim - j * tk, key, INT_MIN)
        return carry

    lax.fori_loop(0, n_kb, score_body, 0)

    rows_acc = 8 * SUBLANES

    def count_where(pred):
        def body(j, cnt):
            blk = key_ref[j]
            hit = pred(blk, j)
            for r in range(tk // rows_acc):
                cnt = jnp.where(hit[r * rows_acc:(r + 1) * rows_acc], cnt + 1.0, cnt)
            return cnt
        cnt = lax.fori_loop(0, n_kb, body, jnp.zeros((rows_acc, tq), F32))
        return jnp.sum(cnt, axis=0, keepdims=True)

    def bit_body(p, state):
        thr, cnt_thr = state
        cand = thr + jnp.left_shift(jnp.int32(1), 31 - p)
        cnt = count_where(lambda blk, j: blk >= cand)
        keep = cnt >= float(k_top)
        return jnp.where(keep, cand, thr), jnp.where(keep, cnt, cnt_thr)

    def search_bits(lo, hi, state):
        return lax.fori_loop(lo, hi, bit_body, state)

    def settled(state):
        return jnp.min(jnp.where(state[1] == float(k_top), 1.0, 0.0)) > 0.0

    state = search_bits(0, 22, (jnp.full((1, tq), INT_MIN, jnp.int32), jnp.full((1, tq), 2.0 ** 30, F32)))
    for lo in range(22, 32, 2):
        state = lax.cond(settled(state), lambda s: s, functools.partial(search_bits, lo, lo + 2), state)
    thr_raw, cnt_thr = state
    thr = jnp.maximum(thr_raw, INT_MIN + 1)

    tied = jnp.logical_and(cnt_thr > float(k_top), thr_raw > INT_MIN)
    n_tied = jnp.max(jnp.where(tied, 1.0, 0.0))

    @pl.when(n_tied > 0.0)
    def _():
        need = float(k_top) - count_where(lambda blk, j: blk > thr)
        n_bits = max(1, int(n_kb_max * tk - 1).bit_length())

        def pos_body(b, last):
            step = jnp.left_shift(jnp.int32(1), n_bits - 1 - b)
            cand = last + step - 1
            got = count_where(lambda blk, j: jnp.logical_and(blk == thr, k_row + j * tk <= cand))
            return jnp.where(got < need, last + step, last)

        last = lax.fori_loop(0, n_bits, pos_body, jnp.zeros((1, tq), jnp.int32))

        def demote_body(j, carry):
            blk = key_ref[j]
            drop = jnp.logical_and(blk == thr, k_row + j * tk > last)
            key_ref[j] = jnp.where(drop, thr - 1, blk)
            return carry

        lax.fori_loop(0, n_kb, demote_body, 0)

    m_ref[...] = jnp.full(m_ref.shape, NEG_BIG, F32)
    l_ref[...] = jnp.zeros(l_ref.shape, F32)
    acc_ref[...] = jnp.zeros(acc_ref.shape, F32)
    c_exp = ATTN_HEAD_DIM ** -0.5 * LOG2_E
    cols = ATTN_GROUP * tq

    def logits_t(j, g):
        k0 = pl.multiple_of(j * tk, tk)
        kb = k_ref[0, pl.ds(k0, tk), g * ATTN_HEAD_DIM:(g + 1) * ATTN_HEAD_DIM]
        vtb = vt_ref[0, j, g * ATTN_HEAD_DIM:(g + 1) * ATTN_HEAD_DIM, :]
        qg = q_ref[0, g * ATTN_GROUP:(g + 1) * ATTN_GROUP].reshape(cols, ATTN_HEAD_DIM)
        return lax.dot_general(kb, qg, _NT, preferred_element_type=F32), vtb

    qf = q_ref[0].reshape(ATTN_HEADS * tq, ATTN_HEAD_DIM).astype(F32)
    qn2 = lax.dot_general(jnp.ones((SUBLANES, ATTN_HEAD_DIM), BF16), (qf * qf).astype(BF16), _NT,
                          preferred_element_type=F32)
    qn2_max = qn2[0:1, 0:tq]
    for h in range(1, ATTN_HEADS):
        qn2_max = jnp.maximum(qn2_max, qn2[0:1, h * tq:(h + 1) * tq])
    m_bound = jnp.sqrt(qn2_max * kn_ref[...]) * 1.02
    fixed_shift_ok = jnp.max(m_bound) * (2.0 * c_exp) < 100.0

    @pl.when(fixed_shift_ok)
    def _():
        shift = -c_exp * m_bound
        ones = jnp.ones((2 * SUBLANES, tk), BF16)

        def accumulate(j):
            for g in range(ATTN_KV_HEADS):
                vtb = vt_ref[0, j, g * ATTN_HEAD_DIM:(g + 1) * ATTN_HEAD_DIM, :]
                acc_ref[g] += jnp.dot(vtb, p_ref[g], preferred_element_type=F32)
                l_ref[g] += jnp.dot(ones, p_ref[g], preferred_element_type=F32)[0:1]

        def probabilities(j):
            bias = jnp.where(key_ref[j] >= thr, shift, NEG_BIG)
            bias = jnp.concatenate([bias] * ATTN_GROUP, axis=1)
            for g in range(ATTN_KV_HEADS):
                s, _ = logits_t(j, g)
                p_ref[g] = jnp.exp2(s * c_exp + bias).astype(BF16)

        probabilities(0)

        def body(j, carry):
            accumulate(j - 1)
            probabilities(j)
            return carry

        lax.fori_loop(1, n_kb, body, 0)
        accumulate(n_kb - 1)

    @pl.when(jnp.logical_not(fixed_shift_ok))
    def _():
        def body(j, carry):
            bias = jnp.where(key_ref[j] >= thr, 0.0, NEG_BIG)
            bias = jnp.concatenate([bias] * ATTN_GROUP, axis=1)
            for g in range(ATTN_KV_HEADS):
                s, vtb = logits_t(j, g)
                s = s + bias
                m_old = m_ref[g]
                m_new = jnp.maximum(m_old, jnp.max(s, axis=0, keepdims=True))
                alpha = jnp.exp2((m_old - m_new) * c_exp)
                p = jnp.exp2((s - m_new) * c_exp)
                l_ref[g] = alpha * l_ref[g] + jnp.sum(p, axis=0, keepdims=True)
                acc_ref[g] = alpha * acc_ref[g] + jnp.dot(vtb, p.astype(BF16), preferred_element_type=F32)
                m_ref[g] = m_new
            return carry

        lax.fori_loop(0, n_kb, body, 0)

    n_out = o_ref.shape[1]
    for g in range(ATTN_KV_HEADS):
        out_t = acc_ref[g] / l_ref[g]
        for hh in range(ATTN_GROUP):
            h = g * ATTN_GROUP + hh
            o_ref[0, :, h * ATTN_HEAD_DIM:(h + 1) * ATTN_HEAD_DIM] = (
                out_t[:, hh * tq:(hh + 1) * tq].T[:n_out].astype(o_ref.dtype))


def _rows_out_spec(width, n_steps, out_rows, into, dtype):
    b, rows = out_rows[0], out_rows[1]
    if into is None:
        spec = pl.BlockSpec((1, rows, width), lambda bi, i: (bi, i, 0))
        return spec, jax.ShapeDtypeStruct((b, out_rows[2], width), dtype), [], []
    buf, first_row, rows = into
    assert first_row % rows == 0 and buf.shape[0] == 1 and buf.dtype == dtype
    blk0 = first_row // rows
    spec = pl.BlockSpec((1, rows, width), lambda bi, i: (0, blk0 + bi * n_steps + i, 0))
    return spec, jax.ShapeDtypeStruct(buf.shape, dtype), [pl.BlockSpec(memory_space=pl.ANY)], [buf]


def dsa_attention(q, qi, kx, k, v, ki, *, n_q, tq, tk, q_pos0, k_top, v_is_transposed=False,
                  out_rows=None, into=None):
    b = q.shape[0]
    s = k.shape[1]
    assert n_q % tq == 0 and s % tk == 0 and tq == LANES
    n_kb_max = s // tk
    cols = ATTN_GROUP * tq
    vt = v if v_is_transposed else v.reshape(b, n_kb_max, tk, ATTN_KV_DIM).transpose(0, 1, 3, 2)
    assert vt.shape == (b, n_kb_max, ATTN_KV_DIM, tk)
    kern = functools.partial(_attn_kernel, tq=tq, tk=tk, q_pos0=q_pos0, k_top=k_top, n_kb_max=n_kb_max)
    resident = dict(pipeline_mode=pl.Buffered(1))
    out_spec, out_shape, buf_specs, bufs = _rows_out_spec(
        ATTN_Q_DIM, n_q // tq, (b, tq, out_rows or n_q), into, BF16)
    return pl.pallas_call(
        kern,
        grid=(b, n_q // tq),
        in_specs=[
            pl.BlockSpec((1, ATTN_HEADS, tq, ATTN_HEAD_DIM), lambda bi, i: (bi, 0, i, 0)),
            pl.BlockSpec((1, IDX_HEADS, tq, IDX_HEAD_DIM), lambda bi, i: (bi, 0, i, 0)),
            pl.BlockSpec((1, tq, LANES), lambda bi, i: (bi, i, 0)),
            pl.BlockSpec((1, s, ATTN_KV_DIM), lambda bi, i: (bi, 0, 0), **resident),
            pl.BlockSpec((1, n_kb_max, ATTN_KV_DIM, tk), lambda bi, i: (bi, 0, 0, 0), **resident),
            pl.BlockSpec((1, s, IDX_HEAD_DIM), lambda bi, i: (bi, 0, 0), **resident),
        ] + buf_specs,
        out_specs=out_spec,
        out_shape=out_shape,
        input_output_aliases={6: 0} if bufs else {},
        scratch_shapes=[
            pltpu.VMEM((n_kb_max, tk, tq), jnp.int32),
            pltpu.VMEM((ATTN_KV_HEADS, 1, cols), F32),
            pltpu.VMEM((ATTN_KV_HEADS, 1, cols), F32),
            pltpu.VMEM((ATTN_KV_HEADS, ATTN_HEAD_DIM, cols), F32),
            pltpu.VMEM((1, tq), F32),
            pltpu.VMEM((ATTN_KV_HEADS, tk, cols), BF16),
        ],
        compiler_params=_params("parallel", "arbitrary"),
    )(q, qi, kx, k, vt, ki, *bufs)


_LEVELS = (32, 16, 8, 4, 2, 1)


def _hgrn_tables():
    c = CHUNK
    t = np.arange(c)[:, None]
    u = np.arange(c)[None, :]
    mats = [(u <= t), (u > t)]
    masks = []
    for w in _LEVELS:
        r = (t // (2 * w)) * (2 * w) + w - 1
        upper = (t % (2 * w)) >= w
        mats.append((upper & (u > r) & (u <= t)) | ((~upper) & (u > t) & (u <= r)))
        s = np.arange(c)[None, :]
        masks.append(((t // (2 * w)) == (s // (2 * w))) & upper & ((s % (2 * w)) < w))
    masks.append(t == np.arange(c)[None, :])
    table = np.concatenate(mats, axis=0).astype(np.float32)
    return np.concatenate([table] * 3, axis=1), np.stack(masks).astype(np.float32)


def _hgrn_kernel(q_ref, f_ref, i_ref, gt_ref, lb_ref, gn_ref, tab_ref, msk_ref, s0_ref, *rest, layer):
    o_ref, sfin_ref, st_ref = rest[-3:]
    c = CHUNK
    dh = HGRN_HEAD_DIM
    step = pl.program_id(1)

    @pl.when(step == 0)
    def _():
        for h in range(HGRN_HEADS):
            st_ref[h] = s0_ref[0, h].T

    lb_all = lb_ref[...]
    e = jnp.exp(lb_all - jnp.max(lb_all, axis=0, keepdims=True))
    p_lb = e / jnp.sum(e, axis=0, keepdims=True)
    lower = jnp.sum(p_lb[:layer + 1], axis=0, keepdims=True) - p_lb[0:1]

    def chunk_operands(ci):
        rows = slice(ci * c, (ci + 1) * c)
        f = lower + (1.0 - lower) * jax.nn.sigmoid(f_ref[0, rows])
        kk = 1.0 - f
        lf = jnp.log2(f)
        lf_hi = lf.astype(BF16)
        r1 = lf - lf_hi.astype(F32)
        lf_mid = r1.astype(BF16)
        lf_lo = (r1 - lf_mid.astype(F32)).astype(BF16)
        lf3 = jnp.concatenate([lf_hi, lf_mid, lf_lo], axis=0)
        ex = jnp.exp2(jnp.dot(tab_ref[...], lf3, preferred_element_type=F32))
        qq = q_ref[0, rows]
        vv = i_ref[0, rows]
        ops = dict(q16=qq.astype(BF16), k16=kk.astype(BF16), v16=vv.astype(BF16), vv=vv)
        ops["q_lv"] = [(qq * ex[(2 + li) * c:(3 + li) * c]).astype(BF16) for li in range(len(_LEVELS))]
        ops["k_lv"] = [(kk * ex[(2 + li) * c:(3 + li) * c]).astype(BF16) for li in range(len(_LEVELS))]
        ops["q_in"] = (qq * ex[0:c]).astype(BF16)
        ops["k_out"] = (kk * ex[c:2 * c]).astype(BF16)
        ops["decay"] = ex[c - 1:c]
        ops["gate"] = jax.nn.sigmoid(gt_ref[0, rows])
        return ops

    def chunk_outputs(ci, ops):
        rows = slice(ci * c, (ci + 1) * c)
        scores = []
        for h in range(HGRN_HEADS):
            sl = slice(h * dh, (h + 1) * dh)
            s_h = msk_ref[len(_LEVELS)] * lax.dot_general(
                ops["q16"][:, sl], ops["k16"][:, sl], _NT, preferred_element_type=F32)
            for li in range(len(_LEVELS)):
                s_h = s_h + msk_ref[li] * lax.dot_general(
                    ops["q_lv"][li][:, sl], ops["k_lv"][li][:, sl], _NT, preferred_element_type=F32)
            scores.append(s_h.astype(BF16))
        for h in range(HGRN_HEADS):
            sl = slice(h * dh, (h + 1) * dh)
            st = st_ref[h]
            o_h = jnp.dot(scores[h], ops["v16"][:, sl], preferred_element_type=F32)
            o_h = o_h + lax.dot_general(ops["q_in"][:, sl], st.astype(BF16), _NT, preferred_element_type=F32)
            st_ref[h] = st * ops["decay"][:, sl] + jnp.dot(
                ops["vv"][:, sl].T.astype(BF16), ops["k_out"][:, sl], preferred_element_type=F32)
            o_ref[0, rows, sl] = (_rms(o_h, gn_ref[:, sl]) * ops["gate"][:, sl]).astype(o_ref.dtype)

    n_sub = q_ref.shape[1] // c
    ops = chunk_operands(0)
    for ci in range(n_sub):
        ops_next = chunk_operands(ci + 1) if ci + 1 < n_sub else None
        chunk_outputs(ci, ops)
        ops = ops_next

    @pl.when(step == pl.num_programs(1) - 1)
    def _():
        for h in range(HGRN_HEADS):
            sfin_ref[0, h] = st_ref[h].T


def hgrn_scan(p, lb, gnorm, s0, *, layer, n_rows=None, out_rows=None, into=None, states=None):
    b = p.shape[0]
    l = p.shape[1] if n_rows is None else n_rows
    d = D_MODEL
    c = max(r for r in (4 * CHUNK, 2 * CHUNK, CHUNK) if l % r == 0)
    tab, msk = _hgrn_tables()
    kern = functools.partial(_hgrn_kernel, layer=layer)
    col = lambda j: pl.BlockSpec((1, c, d), lambda bi, t, j=j: (bi, t, j))
    state_spec = pl.BlockSpec((1, HGRN_HEADS, HGRN_HEAD_DIM, HGRN_HEAD_DIM), lambda bi, t: (bi, 0, 0, 0))
    out_spec, out_shape, buf_specs, bufs = _rows_out_spec(d, l // c, (b, c, out_rows or l), into, BF16)
    aliases = {9: 0} if bufs else {}
    states_spec = pl.BlockSpec((None,) + state_spec.block_shape, lambda bi, t: (layer, bi, 0, 0, 0))
    if states is not None:
        aliases[9 + len(bufs)] = 1
        buf_specs = buf_specs + [pl.BlockSpec(memory_space=pl.ANY)]
        bufs = bufs + [states]
    return pl.pallas_call(
        kern,
        grid=(b, l // c),
        in_specs=[
            col(0), col(1), col(2), col(3),
            pl.BlockSpec(lb.shape, lambda bi, t: (0, 0)),
            pl.BlockSpec((1, d), lambda bi, t: (0, 0)),
            pl.BlockSpec(tab.shape, lambda bi, t: (0, 0)),
            pl.BlockSpec(msk.shape, lambda bi, t: (0, 0, 0)),
            state_spec,
        ] + buf_specs,
        out_specs=[out_spec, states_spec],
        out_shape=[out_shape, jax.ShapeDtypeStruct((lb.shape[0],) + s0.shape, F32)],
        input_output_aliases=aliases,
        scratch_shapes=[pltpu.VMEM((HGRN_HEADS, HGRN_HEAD_DIM, HGRN_HEAD_DIM), F32)],
        compiler_params=_params("parallel", "arbitrary"),
    )(p, p, p, p, lb, gnorm.reshape(1, d), jnp.asarray(tab, BF16), jnp.asarray(msk), s0, *bufs)


def _attn_layer(x, g, w_in, w_out, layer, new_kv, cache_k, cache_v, cache_kidx, n_p, b_s, t_s):
    q, qi, kx, new_kv, k16, vt, ki16 = attn_in_proj(x, g, w_in, layer=layer, n_prompt=n_p, new_kv=new_kv)
    k_new, v_new, ki_new = new_kv[3][layer], new_kv[4][layer], new_kv[5][layer]
    past = cache_k.shape[1]
    tq = LANES

    o_p = dsa_attention(
        q[None], qi[None], kx[None], k16[None], vt[None], ki16[None], v_is_transposed=True,
        n_q=n_p, tq=tq, tk=vt.shape[2], q_pos0=0, k_top=min(TOPK_MAX, n_p // 4),
        out_rows=n_p + b_s * t_s)

    tk_s = 256
    s_all = past + t_s
    n_adm_pad = ((past + tq - 1) // CHUNK + 1) * CHUNK
    s_pad = -(-max(s_all, n_adm_pad) // tk_s) * tk_s

    def with_cache(cache, new, width):
        full = jnp.zeros((b_s, s_pad, width), BF16)
        full = lax.dynamic_update_slice(full, cache.reshape(b_s, past, width).astype(BF16), (0, 0, 0))
        return lax.dynamic_update_slice(full, new.astype(BF16).reshape(b_s, t_s, width), (0, past, 0))

    def sample_heads(a, heads, dim):
        a = a[:, n_p:].reshape(heads, b_s, t_s, dim).transpose(1, 0, 2, 3)
        return jnp.pad(a, ((0, 0), (0, 0), (0, tq - t_s), (0, 0)))

    kx_s = jnp.pad(kx[n_p:].reshape(b_s, t_s, LANES), ((0, 0), (0, tq - t_s), (0, 0)))
    o_s = dsa_attention(
        sample_heads(q, ATTN_HEADS, ATTN_HEAD_DIM), sample_heads(qi, IDX_HEADS, IDX_HEAD_DIM), kx_s,
        with_cache(cache_k, k_new, ATTN_KV_DIM), with_cache(cache_v, v_new, ATTN_KV_DIM),
        with_cache(cache_kidx, ki_new, IDX_HEAD_DIM),
        n_q=tq, tq=tq, tk=tk_s, q_pos0=past, k_top=min(TOPK_MAX, s_all // 4),
        into=(o_p, n_p, t_s))

    x = matmul_residual(o_s[0], w_out, x, layer=layer)
    return x, new_kv


def _rec_layer(x, g, w_in, w_out, gnorm, rec_lb, state, layer, new_states, n_p, b_s, t_s):
    p = norm_matmul(x, g, w_in, layer=layer)
    s0_p = jnp.zeros((1,) + state.shape[1:], F32)
    o_p, s_p = hgrn_scan(p[None], rec_lb, gnorm, s0_p, layer=layer, n_rows=n_p, out_rows=x.shape[0],
                         states=new_states[0])
    o_s, s_s = hgrn_scan(p[n_p:].reshape(b_s, t_s, -1), rec_lb, gnorm, state, layer=layer,
                         into=(o_p, n_p, t_s),
                         states=new_states[1])
    x = matmul_residual(o_s[0], w_out, x, layer=layer)
    return x, (s_p, s_s)


def kernel(x_prompt, x_sample, cache_k, cache_v, cache_kidx, state_s, norm_mix, norm_mlp, norm_final,
           attn_w_in, attn_w_out, rec_w_in, rec_w_out, rec_gnorm, rec_lb, mlp_w_up, mlp_w_down):
    b_p, l_p, d = x_prompt.shape
    b_s, t_s, _ = x_sample.shape
    assert b_p == 1
    n_p = b_p * l_p
    x = (x_prompt.reshape(n_p, d), x_sample.reshape(b_s * t_s, d))

    new_states = (None, None)
    new_kv = None
    for layer in range(DEPTH):
        j = layer // N_MIXERS
        if layer % N_MIXERS == 0:
            x, new_kv = _attn_layer(
                x, norm_mix[layer], attn_w_in, attn_w_out, j, new_kv,
                cache_k[j], cache_v[j], cache_kidx[j], n_p, b_s, t_s)
        else:
            x, new_states = _rec_layer(
                x, norm_mix[layer], rec_w_in, rec_w_out, rec_gnorm[j], rec_lb, state_s[j],
                j, new_states, n_p, b_s, t_s)
        x = mlp_residual(x, norm_mlp[layer], mlp_w_up, mlp_w_down, layer=layer,
                         g_final=norm_final if layer == DEPTH - 1 else None)

    y = x
    kp, vp, kip, ks, vs, kis = new_kv
    n_l = kp.shape[0]
    heads = (ATTN_KV_HEADS, ATTN_HEAD_DIM)
    return (y[:n_p].reshape(b_p, l_p, d), y[n_p:].reshape(b_s, t_s, d),
            kp.reshape(n_l, b_p, l_p, *heads), vp.reshape(n_l, b_p, l_p, *heads),
            kip.reshape(n_l, b_p, l_p, IDX_HEAD_DIM), new_states[0],
            ks.reshape(n_l, b_s, t_s, *heads), vs.reshape(n_l, b_s, t_s, *heads),
            kis.reshape(n_l, b_s, t_s, IDX_HEAD_DIM), new_states[1])
```

```python
import functools
import math

import numpy as np
import jax
import jax.numpy as jnp
from jax import lax
from jax.experimental import pallas as pl
from jax.experimental.pallas import tpu as pltpu

D_MODEL = 2048
DEPTH = 4
CHUNK = 64
N_MIXERS = 2
ATTN_HEADS = 16
ATTN_KV_HEADS = 4
ATTN_HEAD_DIM = D_MODEL // ATTN_HEADS
ATTN_GROUP = ATTN_HEADS // ATTN_KV_HEADS
IDX_HEADS = 16
IDX_HEAD_DIM = 64
TOPK_MAX = 256
HGRN_HEAD_DIM = 128
HGRN_HEADS = D_MODEL // HGRN_HEAD_DIM
D_FF = 4 * D_MODEL
RMS_EPS = 1e-6
ATTN_Q_DIM = ATTN_HEADS * ATTN_HEAD_DIM
ATTN_KV_DIM = ATTN_KV_HEADS * ATTN_HEAD_DIM
IDX_Q_DIM = IDX_HEADS * IDX_HEAD_DIM

LANES = 128
SUBLANES = 8
VMEM_LIMIT = 56 * 1024 * 1024

F32 = jnp.float32
BF16 = jnp.bfloat16
INT_MIN = np.int32(-(2 ** 31))
NEG_BIG = -1e30
LOG2_E = 1.4426950408889634

_NT = (((1,), (1,)), ((), ()))


def _params(*sem):
    return pltpu.CompilerParams(dimension_semantics=sem, vmem_limit_bytes=VMEM_LIMIT)


def _rms(x, g):
    ms = jnp.mean(x * x, axis=-1, keepdims=True)
    return (x * lax.rsqrt(ms + RMS_EPS)) * g


def _norm_matmul_kernel(x_ref, g_ref, w_ref, o_ref, h_ref):
    @pl.when(pl.program_id(1) == 0)
    def _():
        h_ref[...] = _rms(x_ref[...], g_ref[...]).astype(BF16)

    o_ref[...] = jnp.dot(h_ref[...], w_ref[...].astype(BF16), preferred_element_type=F32)


def _row_tile(m, cap=1088):
    return max(t for t in range(16, cap + 1, 16) if m % t == 0)


_ONE_BUFFER = dict(pipeline_mode=pl.Buffered(1))


def norm_matmul(x, g, w, *, layer, tn=512):
    m, d = x.shape
    n = w.shape[2]
    tm = _row_tile(m)
    assert n % tn == 0
    return pl.pallas_call(
        _norm_matmul_kernel,
        grid=(m // tm, n // tn),
        in_specs=[
            pl.BlockSpec((tm, d), lambda i, j: (i, 0), **_ONE_BUFFER),
            pl.BlockSpec((1, d), lambda i, j: (0, 0)),
            pl.BlockSpec((None, d, tn), lambda i, j: (layer, 0, j)),
        ],
        out_specs=pl.BlockSpec((tm, tn), lambda i, j: (i, j)),
        out_shape=jax.ShapeDtypeStruct((m, n), F32),
        scratch_shapes=[pltpu.VMEM((tm, d), BF16)],
        compiler_params=_params("parallel", "arbitrary"),
    )(x, g.reshape(1, d), w)


def _round_bf16_kernel(w_ref, o_ref):
    o_ref[...] = w_ref[...].astype(BF16)


def round_bf16(w_t, *, layer, n_rows, tn=512):
    d = w_t.shape[2]
    assert n_rows % tn == 0
    return pl.pallas_call(
        _round_bf16_kernel,
        grid=(n_rows // tn,),
        in_specs=[pl.BlockSpec((None, tn, d), lambda j: (layer, j, 0))],
        out_specs=pl.BlockSpec((tn, d), lambda j: (j, 0)),
        out_shape=jax.ShapeDtypeStruct((n_rows, d), BF16),
        compiler_params=_params("parallel"),
    )(w_t)


def _attn_in_proj_kernel(x_ref, g_ref, w_ref, wkx_ref, *rest, n_q, n_qi, prompt, tk):
    n_out = 9 if prompt else 6
    outs, h_ref = rest[-(n_out + 1):-1], rest[-1]
    q_ref, qi_ref, kx_ref, k_ref, v_ref, ki_ref = outs[:6]
    j = pl.program_id(1)

    @pl.when(j == 0)
    def _():
        h_ref[...] = _rms(x_ref[...], g_ref[...]).astype(BF16)

    def tile():
        return lax.dot_general(h_ref[...], w_ref[...], _NT, preferred_element_type=F32)

    @pl.when(j < n_q)
    def _():
        res = tile()
        for hh in range(q_ref.shape[0]):
            q_ref[hh] = res[:, hh * ATTN_HEAD_DIM:(hh + 1) * ATTN_HEAD_DIM].astype(q_ref.dtype)

    @pl.when(j == n_q)
    def _():
        res = tile()
        k_ref[...] = res
        if prompt:
            outs[6][...] = res.astype(BF16)

    @pl.when(j == n_q + 1)
    def _():
        res = tile()
        v_ref[...] = res
        if prompt:
            for t in range(outs[7].shape[0]):
                outs[7][t] = res[t * tk:(t + 1) * tk].T.astype(BF16)

    @pl.when(jnp.logical_and(j >= n_q + 2, j < n_q + 2 + n_qi))
    def _():
        res = tile()
        for hh in range(qi_ref.shape[0]):
            qi_ref[hh] = res[:, hh * IDX_HEAD_DIM:(hh + 1) * IDX_HEAD_DIM].astype(qi_ref.dtype)

    @pl.when(j == n_q + 2 + n_qi)
    def _():
        res = lax.dot_general(h_ref[...], wkx_ref[...].astype(BF16), _NT, preferred_element_type=F32)
        kx_ref[...] = res
        ki_ref[...] = res[:, :IDX_HEAD_DIM]
        if prompt:
            outs[8][...] = res[:, :IDX_HEAD_DIM].astype(BF16)


def attn_in_proj(x, g, w_in, *, layer, n_prompt, new_kv=None):
    x_parts = x if isinstance(x, tuple) else None
    m = sum(part.shape[0] for part in x_parts) if x_parts else x.shape[0]
    d = w_in.shape[1]
    tn = tk = ATTN_KV_DIM
    n_layers = w_in.shape[0]
    n_q, n_qi = ATTN_Q_DIM // tn, IDX_Q_DIM // tn
    n_main = n_q + 2 + n_qi
    tail = w_in.shape[2] - n_main * tn
    w_t = jnp.swapaxes(w_in, 1, 2)
    w_kx = jnp.pad(w_t[layer:layer + 1, n_main * tn:, :], ((0, 0), (0, LANES - tail), (0, 0)))
    w_main = round_bf16(w_t, layer=layer, n_rows=n_main * tn)
    clip = lambda v, hi: jnp.minimum(jnp.maximum(v, 0), hi)

    def project(row0, n_rows, tm, prompt, shared, slabs):
        assert row0 % tm == 0 and n_rows % tm == 0 and (tm % tk == 0 or not prompt)
        r0 = row0 // tm
        x_rows, x_r0 = (x, r0) if x_parts is None else (x_parts[0 if prompt else 1], 0)
        bufs = list(shared or ()) + list(slabs or ())
        aliases = {}
        if shared:
            aliases.update({4 + n: n for n in range(3)})
        if slabs:
            aliases.update({4 + len(shared or ()) + n: 3 + n for n in range(3)})
        slab = lambda width: jax.ShapeDtypeStruct((n_layers, n_rows, width), F32)
        out_specs = [
            pl.BlockSpec((tn // ATTN_HEAD_DIM, tm, ATTN_HEAD_DIM), lambda i, j: (clip(j, n_q - 1), r0 + i, 0)),
            pl.BlockSpec((tn // IDX_HEAD_DIM, tm, IDX_HEAD_DIM),
                         lambda i, j: (clip(j - n_q - 2, n_qi - 1), r0 + i, 0)),
            pl.BlockSpec((tm, LANES), lambda i, j: (r0 + i, 0)),
            pl.BlockSpec((None, tm, tn), lambda i, j: (layer, i, 0)),
            pl.BlockSpec((None, tm, tn), lambda i, j: (layer, i, 0)),
            pl.BlockSpec((None, tm, IDX_HEAD_DIM), lambda i, j: (layer, i, 0)),
        ]
        out_shape = [
            jax.ShapeDtypeStruct((ATTN_HEADS, m, ATTN_HEAD_DIM), BF16),
            jax.ShapeDtypeStruct((IDX_HEADS, m, IDX_HEAD_DIM), BF16),
            jax.ShapeDtypeStruct((m, LANES), F32),
            slab(tn), slab(tn), slab(IDX_HEAD_DIM),
        ]
        if prompt:
            out_specs += [
                pl.BlockSpec((tm, tn), lambda i, j: (i, 0)),
                pl.BlockSpec((tm // tk, tn, tk), lambda i, j: (i, 0, 0)),
                pl.BlockSpec((tm, IDX_HEAD_DIM), lambda i, j: (i, 0)),
            ]
            out_shape += [
                jax.ShapeDtypeStruct((n_rows, tn), BF16),
                jax.ShapeDtypeStruct((n_rows // tk, tn, tk), BF16),
                jax.ShapeDtypeStruct((n_rows, IDX_HEAD_DIM), BF16),
            ]
        return pl.pallas_call(
            functools.partial(_attn_in_proj_kernel, n_q=n_q, n_qi=n_qi, prompt=prompt, tk=tk),
            grid=(n_rows // tm, n_main + 1),
            in_specs=[
                pl.BlockSpec((tm, d), lambda i, j: (x_r0 + i, 0), **_ONE_BUFFER),
                pl.BlockSpec((1, d), lambda i, j: (0, 0)),
                pl.BlockSpec((tn, d), lambda i, j: (jnp.minimum(j, n_main - 1), 0)),
                pl.BlockSpec((None, LANES, d), lambda i, j: (0, 0, 0)),
            ] + [pl.BlockSpec(memory_space=pl.ANY)] * len(bufs),
            out_specs=out_specs,
            out_shape=out_shape,
            input_output_aliases=aliases,
            scratch_shapes=[pltpu.VMEM((tm, d), BF16)],
            compiler_params=_params("parallel", "arbitrary"),
        )(x_rows, g.reshape(1, d), w_main, w_kx, *bufs)

    n_sample = m - n_prompt
    tm_p = max(t for t in (2 * tk, tk) if n_prompt % t == 0)
    tm_s = _row_tile(n_sample, cap=tk)
    prev_p, prev_s = (None, None) if new_kv is None else (new_kv[:3], new_kv[3:])
    q, qi, kx, kp, vp, kip, k16, vt, ki16 = project(0, n_prompt, tm_p, True, None, prev_p)
    q, qi, kx, ks, vs, kis = project(n_prompt, n_sample, tm_s, False, (q, qi, kx), prev_s)
    return q, qi, kx, (kp, vp, kip, ks, vs, kis), k16, vt, ki16


def _matmul_res_kernel(a_ref, w_ref, r_ref, *rest):
    o_ref = rest[-1]
    o_ref[...] = r_ref[...] + jnp.dot(a_ref[...], w_ref[...].astype(BF16), preferred_element_type=F32)


def matmul_residual(a, w, res, *, layer, tn=512):
    m, k = a.shape
    n = w.shape[2]
    assert n % tn == 0

    def rows(res_rows, res_r0, row0, n_rows, tm, out_buf):
        assert row0 % tm == 0 and n_rows % tm == 0
        r0 = row0 // tm
        bufs = [] if out_buf is None else [out_buf]
        return pl.pallas_call(
            _matmul_res_kernel,
            grid=(n_rows // tm, n // tn),
            in_specs=[
                pl.BlockSpec((tm, k), lambda i, j: (r0 + i, 0)),
                pl.BlockSpec((None, k, tn), lambda i, j: (layer, 0, j)),
                pl.BlockSpec((tm, tn), lambda i, j: (res_r0 + i, j)),
            ] + [pl.BlockSpec(memory_space=pl.ANY)] * len(bufs),
            out_specs=pl.BlockSpec((tm, tn), lambda i, j: (r0 + i, j)),
            out_shape=jax.ShapeDtypeStruct((m, n), F32),
            input_output_aliases={3: 0} if bufs else {},
            compiler_params=_params("parallel", "parallel"),
        )(a, w, res_rows, *bufs)

    if not isinstance(res, tuple):
        return rows(res, 0, 0, m, _row_tile(m), None)
    n_first = res[0].shape[0]
    out = rows(res[0], 0, 0, n_first, _row_tile(n_first), None)
    tm_rest = _row_tile(m - n_first, cap=math.gcd(n_first, m - n_first))
    return rows(res[1], 0, n_first, m - n_first, tm_rest, out)


def _mlp_kernel(x_ref, g_ref, wu_ref, wd_ref, gf_ref, o_ref, h_ref, *, final_norm):
    @pl.when(pl.program_id(1) == 0)
    def _():
        x = x_ref[...]
        h_ref[...] = _rms(x, g_ref[...]).astype(BF16)
        o_ref[...] = x

    u = jnp.maximum(jnp.dot(h_ref[...], wu_ref[...].astype(BF16), preferred_element_type=F32), 0.0)
    o_ref[...] += jnp.dot((u * u).astype(BF16), wd_ref[...].astype(BF16), preferred_element_type=F32)

    if final_norm:
        @pl.when(pl.program_id(1) == pl.num_programs(1) - 1)
        def _():
            o_ref[...] = _rms(o_ref[...], gf_ref[...])


def mlp_residual(x, g, w_up, w_down, *, layer, g_final=None, tf=512):
    m, d = x.shape
    f = w_up.shape[2]
    tm = _row_tile(m)
    assert f % tf == 0
    final_norm = g_final is not None
    gf = (g_final if final_norm else g).reshape(1, d)
    return pl.pallas_call(
        functools.partial(_mlp_kernel, final_norm=final_norm),
        grid=(m // tm, f // tf),
        in_specs=[
            pl.BlockSpec((tm, d), lambda i, j: (i, 0), **_ONE_BUFFER),
            pl.BlockSpec((1, d), lambda i, j: (0, 0)),
            pl.BlockSpec((None, d, tf), lambda i, j: (layer, 0, j)),
            pl.BlockSpec((None, tf, d), lambda i, j: (layer, j, 0)),
            pl.BlockSpec((1, d), lambda i, j: (0, 0)),
        ],
        out_specs=pl.BlockSpec((tm, d), lambda i, j: (i, 0), **_ONE_BUFFER),
        out_shape=jax.ShapeDtypeStruct((m, d), F32),
        scratch_shapes=[pltpu.VMEM((tm, d), BF16)],
        compiler_params=_params("parallel", "arbitrary"),
    )(x, g.reshape(1, d), w_up, w_down, gf)


def _attn_kernel(q_ref, qi_ref, kx_ref, k_ref, vt_ref, ki_ref, *rest, tq, tk, q_pos0, k_top, n_kb_max):
    o_ref, key_ref, m_ref, l_ref, acc_ref, kn_ref, p_ref = rest[-7:]
    @pl.when(pl.program_id(1) == 0)
    def _():
        def kn_body(j, best):
            kf = k_ref[0, pl.ds(pl.multiple_of(j * tk, tk), tk), :].astype(F32)
            sq = kf * kf
            for g in range(ATTN_KV_HEADS):
                n2 = jnp.sum(sq[:, g * ATTN_HEAD_DIM:(g + 1) * ATTN_HEAD_DIM], axis=1, keepdims=True)
                best = jnp.maximum(best, jnp.max(n2, axis=0, keepdims=True))
            return best

        kn_ref[...] = jnp.broadcast_to(lax.fori_loop(0, n_kb_max, kn_body, jnp.zeros((1, 1), F32)), kn_ref.shape)

    q_start = q_pos0 + pl.program_id(1) * tq
    n_adm = ((q_start + tq - 1) // CHUNK + 1) * CHUNK
    n_kb = jnp.minimum((n_adm + tk - 1) // tk, n_kb_max)

    k_row = lax.broadcasted_iota(jnp.int32, (tk, tq), 0)
    k_lim = ((q_start + lax.broadcasted_iota(jnp.int32, (tk, tq), 1)) // CHUNK + 1) * CHUNK

    w_t = kx_ref[0].T[IDX_HEAD_DIM:IDX_HEAD_DIM + IDX_HEADS, :] * (IDX_HEADS ** -0.5 * IDX_HEAD_DIM ** -0.5)
    qi = qi_ref[0].reshape(IDX_HEADS * tq, IDX_HEAD_DIM)

    def score_body(j, carry):
        kib = ki_ref[0, pl.ds(pl.multiple_of(j * tk, tk), tk), :]
        d = lax.dot_general(kib, qi, _NT, preferred_element_type=F32)
        sc = jnp.zeros((tk, tq), F32)
        for h in range(IDX_HEADS):
            sc = sc + w_t[h:h + 1, :] * jnp.maximum(d[:, h * tq:(h + 1) * tq], 0.0)
        bits = pltpu.bitcast(sc, jnp.int32)
        key = bits ^ ((bits >> 31) & jnp.int32(0x7FFFFFFF))
        key_ref[j] = jnp.where(k_row < k_lim - j * tk, key, INT_MIN)
        return carry

    lax.fori_loop(0, n_kb, score_body, 0)

    rows_acc = 8 * SUBLANES

    def count_where(pred):
        def body(j, cnt):
            blk = key_ref[j]
            hit = pred(blk, j)
            for r in range(tk // rows_acc):
                cnt = jnp.where(hit[r * rows_acc:(r + 1) * rows_acc], cnt + 1.0, cnt)
            return cnt
        cnt = lax.fori_loop(0, n_kb, body, jnp.zeros((rows_acc, tq), F32))
        return jnp.sum(cnt, axis=0, keepdims=True)

    def bit_body(p, state):
        thr, cnt_thr = state
        cand = thr + jnp.left_shift(jnp.int32(1), 31 - p)
        cnt = count_where(lambda blk, j: blk >= cand)
        keep = cnt >= float(k_top)
        return jnp.where(keep, cand, thr), jnp.where(keep, cnt, cnt_thr)

    def search_bits(lo, hi, state):
        return lax.fori_loop(lo, hi, bit_body, state)

    def settled(state):
        return jnp.min(jnp.where(state[1] == float(k_top), 1.0, 0.0)) > 0.0

    state = search_bits(0, 22, (jnp.full((1, tq), INT_MIN, jnp.int32), jnp.full((1, tq), 2.0 ** 30, F32)))
    for lo in range(22, 32, 2):
        state = lax.cond(settled(state), lambda s: s, functools.partial(search_bits, lo, lo + 2), state)
    thr_raw, cnt_thr = state
    thr = jnp.maximum(thr_raw, INT_MIN + 1)

    tied = jnp.logical_and(cnt_thr > float(k_top), thr_raw > INT_MIN)
    n_tied = jnp.max(jnp.where(tied, 1.0, 0.0))

    @pl.when(n_tied > 0.0)
    def _():
        need = float(k_top) - count_where(lambda blk, j: blk > thr)
        n_bits = max(1, int(n_kb_max * tk - 1).bit_length())

        def pos_body(b, last):
            step = jnp.left_shift(jnp.int32(1), n_bits - 1 - b)
            cand = last + step - 1
            got = count_where(lambda blk, j: jnp.logical_and(blk == thr, k_row + j * tk <= cand))
            return jnp.where(got < need, last + step, last)

        last = lax.fori_loop(0, n_bits, pos_body, jnp.zeros((1, tq), jnp.int32))

        def demote_body(j, carry):
            blk = key_ref[j]
            drop = jnp.logical_and(blk == thr, k_row + j * tk > last)
            key_ref[j] = jnp.where(drop, thr - 1, blk)
            return carry

        lax.fori_loop(0, n_kb, demote_body, 0)

    m_ref[...] = jnp.full(m_ref.shape, NEG_BIG, F32)
    l_ref[...] = jnp.zeros(l_ref.shape, F32)
    acc_ref[...] = jnp.zeros(acc_ref.shape, F32)
    c_exp = ATTN_HEAD_DIM ** -0.5 * LOG2_E
    cols = ATTN_GROUP * tq

    def logits_t(j, g):
        k0 = pl.multiple_of(j * tk, tk)
        kb = k_ref[0, pl.ds(k0, tk), g * ATTN_HEAD_DIM:(g + 1) * ATTN_HEAD_DIM]
        vtb = vt_ref[0, j, g * ATTN_HEAD_DIM:(g + 1) * ATTN_HEAD_DIM, :]
        qg = q_ref[0, g * ATTN_GROUP:(g + 1) * ATTN_GROUP].reshape(cols, ATTN_HEAD_DIM)
        return lax.dot_general(kb, qg, _NT, preferred_element_type=F32), vtb

    qf = q_ref[0].reshape(ATTN_HEADS * tq, ATTN_HEAD_DIM).astype(F32)
    qn2 = lax.dot_general(jnp.ones((SUBLANES, ATTN_HEAD_DIM), BF16), (qf * qf).astype(BF16), _NT,
                          preferred_element_type=F32)
    qn2_max = qn2[0:1, 0:tq]
    for h in range(1, ATTN_HEADS):
        qn2_max = jnp.maximum(qn2_max, qn2[0:1, h * tq:(h + 1) * tq])
    m_bound = jnp.sqrt(qn2_max * kn_ref[...]) * 1.02
    fixed_shift_ok = jnp.max(m_bound) * (2.0 * c_exp) < 100.0

    @pl.when(fixed_shift_ok)
    def _():
        shift = -c_exp * m_bound
        ones = jnp.ones((2 * SUBLANES, tk), BF16)

        def accumulate(j):
            for g in range(ATTN_KV_HEADS):
                vtb = vt_ref[0, j, g * ATTN_HEAD_DIM:(g + 1) * ATTN_HEAD_DIM, :]
                acc_ref[g] += jnp.dot(vtb, p_ref[g], preferred_element_type=F32)
                l_ref[g] += jnp.dot(ones, p_ref[g], preferred_element_type=F32)[0:1]

        def probabilities(j):
            bias = jnp.where(key_ref[j] >= thr, shift, NEG_BIG)
            bias = jnp.concatenate([bias] * ATTN_GROUP, axis=1)
            for g in range(ATTN_KV_HEADS):
                s, _ = logits_t(j, g)
                p_ref[g] = jnp.exp2(s * c_exp + bias).astype(BF16)

        probabilities(0)

        def body(j, carry):
            accumulate(j - 1)
            probabilities(j)
            return carry

        lax.fori_loop(1, n_kb, body, 0)
        accumulate(n_kb - 1)

    @pl.when(jnp.logical_not(fixed_shift_ok))
    def _():
        def body(j, carry):
            bias = jnp.where(key_ref[j] >= thr, 0.0, NEG_BIG)
            bias = jnp.concatenate([bias] * ATTN_GROUP, axis=1)
            for g in range(ATTN_KV_HEADS):
                s, vtb = logits_t(j, g)
                s = s + bias
                m_old = m_ref[g]
                m_new = jnp.maximum(m_old, jnp.max(s, axis=0, keepdims=True))
                alpha = jnp.exp2((m_old - m_new) * c_exp)
                p = jnp.exp2((s - m_new) * c_exp)
                l_ref[g] = alpha * l_ref[g] + jnp.sum(p, axis=0, keepdims=True)
                acc_ref[g] = alpha * acc_ref[g] + jnp.dot(vtb, p.astype(BF16), preferred_element_type=F32)
                m_ref[g] = m_new
            return carry

        lax.fori_loop(0, n_kb, body, 0)

    n_out = o_ref.shape[1]
    for g in range(ATTN_KV_HEADS):
        out_t = acc_ref[g] / l_ref[g]
        for hh in range(ATTN_GROUP):
            h = g * ATTN_GROUP + hh
            o_ref[0, :, h * ATTN_HEAD_DIM:(h + 1) * ATTN_HEAD_DIM] = (
                out_t[:, hh * tq:(hh + 1) * tq].T[:n_out].astype(o_ref.dtype))


def _rows_out_spec(width, n_steps, out_rows, into, dtype):
    b, rows = out_rows[0], out_rows[1]
    if into is None:
        spec = pl.BlockSpec((1, rows, width), lambda bi, i: (bi, i, 0))
        return spec, jax.ShapeDtypeStruct((b, out_rows[2], width), dtype), [], []
    buf, first_row, rows = into
    assert first_row % rows == 0 and buf.shape[0] == 1 and buf.dtype == dtype
    blk0 = first_row // rows
    spec = pl.BlockSpec((1, rows, width), lambda bi, i: (0, blk0 + bi * n_steps + i, 0))
    return spec, jax.ShapeDtypeStruct(buf.shape, dtype), [pl.BlockSpec(memory_space=pl.ANY)], [buf]


def dsa_attention(q, qi, kx, k, v, ki, *, n_q, tq, tk, q_pos0, k_top, v_is_transposed=False,
                  out_rows=None, into=None):
    b = q.shape[0]
    s = k.shape[1]
    assert n_q % tq == 0 and s % tk == 0 and tq == LANES
    n_kb_max = s // tk
    cols = ATTN_GROUP * tq
    vt = v if v_is_transposed else v.reshape(b, n_kb_max, tk, ATTN_KV_DIM).transpose(0, 1, 3, 2)
    assert vt.shape == (b, n_kb_max, ATTN_KV_DIM, tk)
    kern = functools.partial(_attn_kernel, tq=tq, tk=tk, q_pos0=q_pos0, k_top=k_top, n_kb_max=n_kb_max)
    resident = dict(pipeline_mode=pl.Buffered(1))
    out_spec, out_shape, buf_specs, bufs = _rows_out_spec(
        ATTN_Q_DIM, n_q // tq, (b, tq, out_rows or n_q), into, BF16)
    return pl.pallas_call(
        kern,
        grid=(b, n_q // tq),
        in_specs=[
            pl.BlockSpec((1, ATTN_HEADS, tq, ATTN_HEAD_DIM), lambda bi, i: (bi, 0, i, 0)),
            pl.BlockSpec((1, IDX_HEADS, tq, IDX_HEAD_DIM), lambda bi, i: (bi, 0, i, 0)),
            pl.BlockSpec((1, tq, LANES), lambda bi, i: (bi, i, 0)),
            pl.BlockSpec((1, s, ATTN_KV_DIM), lambda bi, i: (bi, 0, 0), **resident),
            pl.BlockSpec((1, n_kb_max, ATTN_KV_DIM, tk), lambda bi, i: (bi, 0, 0, 0), **resident),
            pl.BlockSpec((1, s, IDX_HEAD_DIM), lambda bi, i: (bi, 0, 0), **resident),
        ] + buf_specs,
        out_specs=out_spec,
        out_shape=out_shape,
        input_output_aliases={6: 0} if bufs else {},
        scratch_shapes=[
            pltpu.VMEM((n_kb_max, tk, tq), jnp.int32),
            pltpu.VMEM((ATTN_KV_HEADS, 1, cols), F32),
            pltpu.VMEM((ATTN_KV_HEADS, 1, cols), F32),
            pltpu.VMEM((ATTN_KV_HEADS, ATTN_HEAD_DIM, cols), F32),
            pltpu.VMEM((1, tq), F32),
            pltpu.VMEM((ATTN_KV_HEADS, tk, cols), BF16),
        ],
        compiler_params=_params("parallel", "arbitrary"),
    )(q, qi, kx, k, vt, ki, *bufs)


_LEVELS = (32, 16, 8, 4, 2, 1)


def _hgrn_tables():
    c = CHUNK
    t = np.arange(c)[:, None]
    u = np.arange(c)[None, :]
    mats = [(u <= t), (u > t)]
    masks = []
    for w in _LEVELS:
        r = (t // (2 * w)) * (2 * w) + w - 1
        upper = (t % (2 * w)) >= w
        mats.append((upper & (u > r) & (u <= t)) | ((~upper) & (u > t) & (u <= r)))
        s = np.arange(c)[None, :]
        masks.append(((t // (2 * w)) == (s // (2 * w))) & upper & ((s % (2 * w)) < w))
    masks.append(t == np.arange(c)[None, :])
    table = np.concatenate(mats, axis=0).astype(np.float32)
    return np.concatenate([table] * 3, axis=1), np.stack(masks).astype(np.float32)


def _hgrn_kernel(q_ref, f_ref, i_ref, gt_ref, lb_ref, gn_ref, tab_ref, msk_ref, s0_ref, *rest, layer):
    o_ref, sfin_ref, st_ref = rest[-3:]
    c = CHUNK
    dh = HGRN_HEAD_DIM
    step = pl.program_id(1)

    @pl.when(step == 0)
    def _():
        for h in range(HGRN_HEADS):
            st_ref[h] = s0_ref[0, h].T

    lb_all = lb_ref[...]
    e = jnp.exp(lb_all - jnp.max(lb_all, axis=0, keepdims=True))
    p_lb = e / jnp.sum(e, axis=0, keepdims=True)
    lower = jnp.sum(p_lb[:layer + 1], axis=0, keepdims=True) - p_lb[0:1]

    def chunk_operands(ci):
        rows = slice(ci * c, (ci + 1) * c)
        f = lower + (1.0 - lower) * jax.nn.sigmoid(f_ref[0, rows])
        kk = 1.0 - f
        lf = jnp.log2(f)
        lf_hi = lf.astype(BF16)
        r1 = lf - lf_hi.astype(F32)
        lf_mid = r1.astype(BF16)
        lf_lo = (r1 - lf_mid.astype(F32)).astype(BF16)
        lf3 = jnp.concatenate([lf_hi, lf_mid, lf_lo], axis=0)
        ex = jnp.exp2(jnp.dot(tab_ref[...], lf3, preferred_element_type=F32))
        qq = q_ref[0, rows]
        vv = i_ref[0, rows]
        ops = dict(q16=qq.astype(BF16), k16=kk.astype(BF16), v16=vv.astype(BF16), vv=vv)
        ops["q_lv"] = [(qq * ex[(2 + li) * c:(3 + li) * c]).astype(BF16) for li in range(len(_LEVELS))]
        ops["k_lv"] = [(kk * ex[(2 + li) * c:(3 + li) * c]).astype(BF16) for li in range(len(_LEVELS))]
        ops["q_in"] = (qq * ex[0:c]).astype(BF16)
        ops["k_out"] = (kk * ex[c:2 * c]).astype(BF16)
        ops["decay"] = ex[c - 1:c]
        ops["gate"] = jax.nn.sigmoid(gt_ref[0, rows])
        return ops

    def chunk_outputs(ci, ops):
        rows = slice(ci * c, (ci + 1) * c)
        scores = []
        for h in range(HGRN_HEADS):
            sl = slice(h * dh, (h + 1) * dh)
            s_h = msk_ref[len(_LEVELS)] * lax.dot_general(
                ops["q16"][:, sl], ops["k16"][:, sl], _NT, preferred_element_type=F32)
            for li in range(len(_LEVELS)):
                s_h = s_h + msk_ref[li] * lax.dot_general(
                    ops["q_lv"][li][:, sl], ops["k_lv"][li][:, sl], _NT, preferred_element_type=F32)
            scores.append(s_h.astype(BF16))
        for h in range(HGRN_HEADS):
            sl = slice(h * dh, (h + 1) * dh)
            st = st_ref[h]
            o_h = jnp.dot(scores[h], ops["v16"][:, sl], preferred_element_type=F32)
            o_h = o_h + lax.dot_general(ops["q_in"][:, sl], st.astype(BF16), _NT, preferred_element_type=F32)
            st_ref[h] = st * ops["decay"][:, sl] + jnp.dot(
                ops["vv"][:, sl].T.astype(BF16), ops["k_out"][:, sl], preferred_element_type=F32)
            o_ref[0, rows, sl] = (_rms(o_h, gn_ref[:, sl]) * ops["gate"][:, sl]).astype(o_ref.dtype)

    n_sub = q_ref.shape[1] // c
    ops = chunk_operands(0)
    for ci in range(n_sub):
        ops_next = chunk_operands(ci + 1) if ci + 1 < n_sub else None
        chunk_outputs(ci, ops)
        ops = ops_next

    @pl.when(step == pl.num_programs(1) - 1)
    def _():
        for h in range(HGRN_HEADS):
            sfin_ref[0, h] = st_ref[h].T


def hgrn_scan(p, lb, gnorm, s0, *, layer, n_rows=None, out_rows=None, into=None, states=None):
    b = p.shape[0]
    l = p.shape[1] if n_rows is None else n_rows
    d = D_MODEL
    c = max(r for r in (4 * CHUNK, 2 * CHUNK, CHUNK) if l % r == 0)
    tab, msk = _hgrn_tables()
    kern = functools.partial(_hgrn_kernel, layer=layer)
    col = lambda j: pl.BlockSpec((1, c, d), lambda bi, t, j=j: (bi, t, j))
    state_spec = pl.BlockSpec((1, HGRN_HEADS, HGRN_HEAD_DIM, HGRN_HEAD_DIM), lambda bi, t: (bi, 0, 0, 0))
    out_spec, out_shape, buf_specs, bufs = _rows_out_spec(d, l // c, (b, c, out_rows or l), into, BF16)
    aliases = {9: 0} if bufs else {}
    states_spec = pl.BlockSpec((None,) + state_spec.block_shape, lambda bi, t: (layer, bi, 0, 0, 0))
    if states is not None:
        aliases[9 + len(bufs)] = 1
        buf_specs = buf_specs + [pl.BlockSpec(memory_space=pl.ANY)]
        bufs = bufs + [states]
    return pl.pallas_call(
        kern,
        grid=(b, l // c),
        in_specs=[
            col(0), col(1), col(2), col(3),
            pl.BlockSpec(lb.shape, lambda bi, t: (0, 0)),
            pl.BlockSpec((1, d), lambda bi, t: (0, 0)),
            pl.BlockSpec(tab.shape, lambda bi, t: (0, 0)),
            pl.BlockSpec(msk.shape, lambda bi, t: (0, 0, 0)),
            state_spec,
        ] + buf_specs,
        out_specs=[out_spec, states_spec],
        out_shape=[out_shape, jax.ShapeDtypeStruct((lb.shape[0],) + s0.shape, F32)],
        input_output_aliases=aliases,
        scratch_shapes=[pltpu.VMEM((HGRN_HEADS, HGRN_HEAD_DIM, HGRN_HEAD_DIM), F32)],
        compiler_params=_params("parallel", "arbitrary"),
    )(p, p, p, p, lb, gnorm.reshape(1, d), jnp.asarray(tab, BF16), jnp.asarray(msk), s0, *bufs)


def _attn_layer(x, g, w_in, w_out, layer, new_kv, cache_k, cache_v, cache_kidx, n_p, b_s, t_s):
    q, qi, kx, new_kv, k16, vt, ki16 = attn_in_proj(x, g, w_in, layer=layer, n_prompt=n_p, new_kv=new_kv)
    k_new, v_new, ki_new = new_kv[3][layer], new_kv[4][layer], new_kv[5][layer]
    past = cache_k.shape[1]
    tq = LANES

    o_p = dsa_attention(
        q[None], qi[None], kx[None], k16[None], vt[None], ki16[None], v_is_transposed=True,
        n_q=n_p, tq=tq, tk=vt.shape[2], q_pos0=0, k_top=min(TOPK_MAX, n_p // 4),
        out_rows=n_p + b_s * t_s)

    tk_s = 256
    s_all = past + t_s
    n_adm_pad = ((past + tq - 1) // CHUNK + 1) * CHUNK
    s_pad = -(-max(s_all, n_adm_pad) // tk_s) * tk_s

    def with_cache(cache, new, width):
        full = jnp.zeros((b_s, s_pad, width), BF16)
        full = lax.dynamic_update_slice(full, cache.reshape(b_s, past, width).astype(BF16), (0, 0, 0))
        return lax.dynamic_update_slice(full, new.astype(BF16).reshape(b_s, t_s, width), (0, past, 0))

    def sample_heads(a, heads, dim):
        a = a[:, n_p:].reshape(heads, b_s, t_s, dim).transpose(1, 0, 2, 3)
        return jnp.pad(a, ((0, 0), (0, 0), (0, tq - t_s), (0, 0)))

    kx_s = jnp.pad(kx[n_p:].reshape(b_s, t_s, LANES), ((0, 0), (0, tq - t_s), (0, 0)))
    o_s = dsa_attention(
        sample_heads(q, ATTN_HEADS, ATTN_HEAD_DIM), sample_heads(qi, IDX_HEADS, IDX_HEAD_DIM), kx_s,
        with_cache(cache_k, k_new, ATTN_KV_DIM), with_cache(cache_v, v_new, ATTN_KV_DIM),
        with_cache(cache_kidx, ki_new, IDX_HEAD_DIM),
        n_q=tq, tq=tq, tk=tk_s, q_pos0=past, k_top=min(TOPK_MAX, s_all // 4),
        into=(o_p, n_p, t_s))

    x = matmul_residual(o_s[0], w_out, x, layer=layer)
    return x, new_kv


def _rec_layer(x, g, w_in, w_out, gnorm, rec_lb, state, layer, new_states, n_p, b_s, t_s):
    p = norm_matmul(x, g, w_in, layer=layer)
    s0_p = jnp.zeros((1,) + state.shape[1:], F32)
    o_p, s_p = hgrn_scan(p[None], rec_lb, gnorm, s0_p, layer=layer, n_rows=n_p, out_rows=x.shape[0],
                         states=new_states[0])
    o_s, s_s = hgrn_scan(p[n_p:].reshape(b_s, t_s, -1), rec_lb, gnorm, state, layer=layer,
                         into=(o_p, n_p, t_s),
                         states=new_states[1])
    x = matmul_residual(o_s[0], w_out, x, layer=layer)
    return x, (s_p, s_s)


def kernel(x_prompt, x_sample, cache_k, cache_v, cache_kidx, state_s, norm_mix, norm_mlp, norm_final,
           attn_w_in, attn_w_out, rec_w_in, rec_w_out, rec_gnorm, rec_lb, mlp_w_up, mlp_w_down):
    b_p, l_p, d = x_prompt.shape
    b_s, t_s, _ = x_sample.shape
    assert b_p == 1
    n_p = b_p * l_p
    x = (x_prompt.reshape(n_p, d), x_sample.reshape(b_s * t_s, d))

    new_states = (None, None)
    new_kv = None
    for layer in range(DEPTH):
        j = layer // N_MIXERS
        if layer % N_MIXERS == 0:
            x, new_kv = _attn_layer(
                x, norm_mix[layer], attn_w_in, attn_w_out, j, new_kv,
                cache_k[j], cache_v[j], cache_kidx[j], n_p, b_s, t_s)
        else:
            x, new_states = _rec_layer(
                x, norm_mix[layer], rec_w_in, rec_w_out, rec_gnorm[j], rec_lb, state_s[j],
                j, new_states, n_p, b_s, t_s)
        x = mlp_residual(x, norm_mlp[layer], mlp_w_up, mlp_w_down, layer=layer,
                         g_final=norm_final if layer == DEPTH - 1 else None)

    y = x
    kp, vp, kip, ks, vs, kis = new_kv
    n_l = kp.shape[0]
    heads = (ATTN_KV_HEADS, ATTN_HEAD_DIM)
    return (y[:n_p].reshape(b_p, l_p, d), y[n_p:].reshape(b_s, t_s, d),
            kp.reshape(n_l, b_p, l_p, *heads), vp.reshape(n_l, b_p, l_p, *heads),
            kip.reshape(n_l, b_p, l_p, IDX_HEAD_DIM), new_states[0],
            ks.reshape(n_l, b_s, t_s, *heads), vs.reshape(n_l, b_s, t_s, *heads),
            kis.reshape(n_l, b_s, t_s, IDX_HEAD_DIM), new_states[1])
```

```python
import functools
import math

import numpy as np
import jax
import jax.numpy as jnp
from jax import lax
from jax.experimental import pallas as pl
from jax.experimental.pallas import tpu as pltpu

D_MODEL = 2048
DEPTH = 4
CHUNK = 64
N_MIXERS = 2
ATTN_HEADS = 16
ATTN_KV_HEADS = 4
ATTN_HEAD_DIM = D_MODEL // ATTN_HEADS
ATTN_GROUP = ATTN_HEADS // ATTN_KV_HEADS
IDX_HEADS = 16
IDX_HEAD_DIM = 64
TOPK_MAX = 256
HGRN_HEAD_DIM = 128
HGRN_HEADS = D_MODEL // HGRN_HEAD_DIM
D_FF = 4 * D_MODEL
RMS_EPS = 1e-6
ATTN_Q_DIM = ATTN_HEADS * ATTN_HEAD_DIM
ATTN_KV_DIM = ATTN_KV_HEADS * ATTN_HEAD_DIM
IDX_Q_DIM = IDX_HEADS * IDX_HEAD_DIM

LANES = 128
SUBLANES = 8
VMEM_LIMIT = 56 * 1024 * 1024

F32 = jnp.float32
BF16 = jnp.bfloat16
INT_MIN = np.int32(-(2 ** 31))
NEG_BIG = -1e30
LOG2_E = 1.4426950408889634

_NT = (((1,), (1,)), ((), ()))


def _params(*sem):
    return pltpu.CompilerParams(dimension_semantics=sem, vmem_limit_bytes=VMEM_LIMIT)


def _rms(x, g):
    ms = jnp.mean(x * x, axis=-1, keepdims=True)
    return (x * lax.rsqrt(ms + RMS_EPS)) * g


def _norm_matmul_kernel(x_ref, g_ref, w_ref, o_ref, h_ref):
    @pl.when(pl.program_id(1) == 0)
    def _():
        h_ref[...] = _rms(x_ref[...], g_ref[...]).astype(BF16)

    o_ref[...] = jnp.dot(h_ref[...], w_ref[...].astype(BF16), preferred_element_type=F32)


def _row_tile(m, cap=1088):
    return max(t for t in range(16, cap + 1, 16) if m % t == 0)


_ONE_BUFFER = dict(pipeline_mode=pl.Buffered(1))


def norm_matmul(x, g, w, *, layer, tn=1024):
    m, d = x.shape
    n = w.shape[2]
    tm = _row_tile(m)
    assert n % tn == 0
    return pl.pallas_call(
        _norm_matmul_kernel,
        grid=(m // tm, n // tn),
        in_specs=[
            pl.BlockSpec((tm, d), lambda i, j: (i, 0), **_ONE_BUFFER),
            pl.BlockSpec((1, d), lambda i, j: (0, 0)),
            pl.BlockSpec((None, d, tn), lambda i, j: (layer, 0, j)),
        ],
        out_specs=pl.BlockSpec((tm, tn), lambda i, j: (i, j)),
        out_shape=jax.ShapeDtypeStruct((m, n), F32),
        scratch_shapes=[pltpu.VMEM((tm, d), BF16)],
        compiler_params=_params("parallel", "arbitrary"),
    )(x, g.reshape(1, d), w)


def _round_bf16_kernel(w_ref, o_ref):
    o_ref[...] = w_ref[...].astype(BF16)


def round_bf16(w_t, *, layer, n_rows, tn=512):
    d = w_t.shape[2]
    assert n_rows % tn == 0
    return pl.pallas_call(
        _round_bf16_kernel,
        grid=(n_rows // tn,),
        in_specs=[pl.BlockSpec((None, tn, d), lambda j: (layer, j, 0))],
        out_specs=pl.BlockSpec((tn, d), lambda j: (j, 0)),
        out_shape=jax.ShapeDtypeStruct((n_rows, d), BF16),
        compiler_params=_params("parallel"),
    )(w_t)


def _attn_in_proj_kernel(x_ref, g_ref, w_ref, wkx_ref, *rest, n_q, n_qi, prompt, tk):
    n_out = 9 if prompt else 6
    outs, h_ref = rest[-(n_out + 1):-1], rest[-1]
    q_ref, qi_ref, kx_ref, k_ref, v_ref, ki_ref = outs[:6]
    j = pl.program_id(1)

    @pl.when(j == 0)
    def _():
        h_ref[...] = _rms(x_ref[...], g_ref[...]).astype(BF16)

    def tile():
        return lax.dot_general(h_ref[...], w_ref[...], _NT, preferred_element_type=F32)

    @pl.when(j < n_q)
    def _():
        res = tile()
        for hh in range(q_ref.shape[0]):
            q_ref[hh] = res[:, hh * ATTN_HEAD_DIM:(hh + 1) * ATTN_HEAD_DIM].astype(q_ref.dtype)

    @pl.when(j == n_q)
    def _():
        res = tile()
        k_ref[...] = res
        if prompt:
            outs[6][...] = res.astype(BF16)

    @pl.when(j == n_q + 1)
    def _():
        res = tile()
        v_ref[...] = res
        if prompt:
            for t in range(outs[7].shape[0]):
                outs[7][t] = res[t * tk:(t + 1) * tk].T.astype(BF16)

    @pl.when(jnp.logical_and(j >= n_q + 2, j < n_q + 2 + n_qi))
    def _():
        res = tile()
        for hh in range(qi_ref.shape[0]):
            qi_ref[hh] = res[:, hh * IDX_HEAD_DIM:(hh + 1) * IDX_HEAD_DIM].astype(qi_ref.dtype)

    @pl.when(j == n_q + 2 + n_qi)
    def _():
        res = lax.dot_general(h_ref[...], wkx_ref[...].astype(BF16), _NT, preferred_element_type=F32)
        kx_ref[...] = res
        ki_ref[...] = res[:, :IDX_HEAD_DIM]
        if prompt:
            outs[8][...] = res[:, :IDX_HEAD_DIM].astype(BF16)


def attn_in_proj(x, g, w_in, *, layer, n_prompt, new_kv=None):
    x_parts = x if isinstance(x, tuple) else None
    m = sum(part.shape[0] for part in x_parts) if x_parts else x.shape[0]
    d = w_in.shape[1]
    tn = tk = ATTN_KV_DIM
    n_layers = w_in.shape[0]
    n_q, n_qi = ATTN_Q_DIM // tn, IDX_Q_DIM // tn
    n_main = n_q + 2 + n_qi
    tail = w_in.shape[2] - n_main * tn
    w_t = jnp.swapaxes(w_in, 1, 2)
    w_kx = jnp.pad(w_t[layer:layer + 1, n_main * tn:, :], ((0, 0), (0, LANES - tail), (0, 0)))
    w_main = round_bf16(w_t, layer=layer, n_rows=n_main * tn)
    clip = lambda v, hi: jnp.minimum(jnp.maximum(v, 0), hi)

    def project(row0, n_rows, tm, prompt, shared, slabs):
        assert row0 % tm == 0 and n_rows % tm == 0 and (tm % tk == 0 or not prompt)
        r0 = row0 // tm
        x_rows, x_r0 = (x, r0) if x_parts is None else (x_parts[0 if prompt else 1], 0)
        bufs = list(shared or ()) + list(slabs or ())
        aliases = {}
        if shared:
            aliases.update({4 + n: n for n in range(3)})
        if slabs:
            aliases.update({4 + len(shared or ()) + n: 3 + n for n in range(3)})
        slab = lambda width: jax.ShapeDtypeStruct((n_layers, n_rows, width), F32)
        out_specs = [
            pl.BlockSpec((tn // ATTN_HEAD_DIM, tm, ATTN_HEAD_DIM), lambda i, j: (clip(j, n_q - 1), r0 + i, 0)),
            pl.BlockSpec((tn // IDX_HEAD_DIM, tm, IDX_HEAD_DIM),
                         lambda i, j: (clip(j - n_q - 2, n_qi - 1), r0 + i, 0)),
            pl.BlockSpec((tm, LANES), lambda i, j: (r0 + i, 0)),
            pl.BlockSpec((None, tm, tn), lambda i, j: (layer, i, 0)),
            pl.BlockSpec((None, tm, tn), lambda i, j: (layer, i, 0)),
            pl.BlockSpec((None, tm, IDX_HEAD_DIM), lambda i, j: (layer, i, 0)),
        ]
        out_shape = [
            jax.ShapeDtypeStruct((ATTN_HEADS, m, ATTN_HEAD_DIM), BF16),
            jax.ShapeDtypeStruct((IDX_HEADS, m, IDX_HEAD_DIM), BF16),
            jax.ShapeDtypeStruct((m, LANES), F32),
            slab(tn), slab(tn), slab(IDX_HEAD_DIM),
        ]
        if prompt:
            out_specs += [
                pl.BlockSpec((tm, tn), lambda i, j: (i, 0)),
                pl.BlockSpec((tm // tk, tn, tk), lambda i, j: (i, 0, 0)),
                pl.BlockSpec((tm, IDX_HEAD_DIM), lambda i, j: (i, 0)),
            ]
            out_shape += [
                jax.ShapeDtypeStruct((n_rows, tn), BF16),
                jax.ShapeDtypeStruct((n_rows // tk, tn, tk), BF16),
                jax.ShapeDtypeStruct((n_rows, IDX_HEAD_DIM), BF16),
            ]
        return pl.pallas_call(
            functools.partial(_attn_in_proj_kernel, n_q=n_q, n_qi=n_qi, prompt=prompt, tk=tk),
            grid=(n_rows // tm, n_main + 1),
            in_specs=[
                pl.BlockSpec((tm, d), lambda i, j: (x_r0 + i, 0), **_ONE_BUFFER),
                pl.BlockSpec((1, d), lambda i, j: (0, 0)),
                pl.BlockSpec((tn, d), lambda i, j: (jnp.minimum(j, n_main - 1), 0)),
                pl.BlockSpec((None, LANES, d), lambda i, j: (0, 0, 0)),
            ] + [pl.BlockSpec(memory_space=pl.ANY)] * len(bufs),
            out_specs=out_specs,
            out_shape=out_shape,
            input_output_aliases=aliases,
            scratch_shapes=[pltpu.VMEM((tm, d), BF16)],
            compiler_params=_params("parallel", "arbitrary"),
        )(x_rows, g.reshape(1, d), w_main, w_kx, *bufs)

    n_sample = m - n_prompt
    tm_p = max(t for t in (2 * tk, tk) if n_prompt % t == 0)
    tm_s = _row_tile(n_sample, cap=tk)
    prev_p, prev_s = (None, None) if new_kv is None else (new_kv[:3], new_kv[3:])
    q, qi, kx, kp, vp, kip, k16, vt, ki16 = project(0, n_prompt, tm_p, True, None, prev_p)
    q, qi, kx, ks, vs, kis = project(n_prompt, n_sample, tm_s, False, (q, qi, kx), prev_s)
    return q, qi, kx, (kp, vp, kip, ks, vs, kis), k16, vt, ki16


def _matmul_res_kernel(a_ref, w_ref, r_ref, *rest):
    o_ref = rest[-1]
    o_ref[...] = r_ref[...] + jnp.dot(a_ref[...], w_ref[...].astype(BF16), preferred_element_type=F32)


def matmul_residual(a, w, res, *, layer, tn=1024):
    m, k = a.shape
    n = w.shape[2]
    assert n % tn == 0

    def rows(res_rows, res_r0, row0, n_rows, tm, out_buf):
        assert row0 % tm == 0 and n_rows % tm == 0
        r0 = row0 // tm
        bufs = [] if out_buf is None else [out_buf]
        return pl.pallas_call(
            _matmul_res_kernel,
            grid=(n_rows // tm, n // tn),
            in_specs=[
                pl.BlockSpec((tm, k), lambda i, j: (r0 + i, 0)),
                pl.BlockSpec((None, k, tn), lambda i, j: (layer, 0, j)),
                pl.BlockSpec((tm, tn), lambda i, j: (res_r0 + i, j)),
            ] + [pl.BlockSpec(memory_space=pl.ANY)] * len(bufs),
            out_specs=pl.BlockSpec((tm, tn), lambda i, j: (r0 + i, j)),
            out_shape=jax.ShapeDtypeStruct((m, n), F32),
            input_output_aliases={3: 0} if bufs else {},
            compiler_params=_params("parallel", "parallel"),
        )(a, w, res_rows, *bufs)

    if not isinstance(res, tuple):
        return rows(res, 0, 0, m, _row_tile(m), None)
    n_first = res[0].shape[0]
    out = rows(res[0], 0, 0, n_first, _row_tile(n_first), None)
    tm_rest = _row_tile(m - n_first, cap=math.gcd(n_first, m - n_first))
    return rows(res[1], 0, n_first, m - n_first, tm_rest, out)


def _mlp_kernel(x_ref, g_ref, wu_ref, wd_ref, gf_ref, o_ref, *rest, final_norm, tail):
    h_ref = rest[-1]

    @pl.when(pl.program_id(1) == 0)
    def _():
        x = x_ref[...]
        h_ref[...] = _rms(x, g_ref[...]).astype(BF16)
        o_ref[...] = x

    u = jnp.maximum(jnp.dot(h_ref[...], wu_ref[...].astype(BF16), preferred_element_type=F32), 0.0)
    o_ref[...] += jnp.dot((u * u).astype(BF16), wd_ref[...].astype(BF16), preferred_element_type=F32)

    if final_norm:
        @pl.when(pl.program_id(1) == pl.num_programs(1) - 1)
        def _():
            o_ref[...] = _rms(o_ref[...], gf_ref[...])

    if tail is not None:
        tail_ref = rest[0]
        tail_tile, tail_row = tail

        @pl.when(jnp.logical_and(pl.program_id(0) == tail_tile, pl.program_id(1) == pl.num_programs(1) - 1))
        def _():
            tail_ref[...] = o_ref[tail_row:tail_row + tail_ref.shape[0]]


def mlp_residual(x, g, w_up, w_down, *, layer, g_final=None, split=None, tf=512):
    m, d = x.shape
    f = w_up.shape[2]
    tm = _row_tile(m)
    assert f % tf == 0
    final_norm = g_final is not None
    gf = (g_final if final_norm else g).reshape(1, d)
    out_spec = pl.BlockSpec((tm, d), lambda i, j: (i, 0), **_ONE_BUFFER)
    if split is None:
        tail, out_specs, out_shape = None, out_spec, jax.ShapeDtypeStruct((m, d), F32)
    else:
        tail = (split // tm, split % tm)
        assert m - split == tm - tail[1] and tail[1] % SUBLANES == 0
        out_specs = [out_spec, pl.BlockSpec((m - split, d), lambda i, j: (0, 0))]
        out_shape = [jax.ShapeDtypeStruct((split, d), F32), jax.ShapeDtypeStruct((m - split, d), F32)]
    return pl.pallas_call(
        functools.partial(_mlp_kernel, final_norm=final_norm, tail=tail),
        grid=(m // tm, f // tf),
        in_specs=[
            pl.BlockSpec((tm, d), lambda i, j: (i, 0), **_ONE_BUFFER),
            pl.BlockSpec((1, d), lambda i, j: (0, 0)),
            pl.BlockSpec((None, d, tf), lambda i, j: (layer, 0, j)),
            pl.BlockSpec((None, tf, d), lambda i, j: (layer, j, 0)),
            pl.BlockSpec((1, d), lambda i, j: (0, 0)),
        ],
        out_specs=out_specs,
        out_shape=out_shape,
        scratch_shapes=[pltpu.VMEM((tm, d), BF16)],
        compiler_params=_params("parallel", "arbitrary"),
    )(x, g.reshape(1, d), w_up, w_down, gf)


def _attn_kernel(q_ref, qi_ref, kx_ref, k_ref, vt_ref, ki_ref, *rest, tq, tk, q_pos0, k_top, n_kb_max):
    o_ref, key_ref, m_ref, l_ref, acc_ref, kn_ref, p_ref = rest[-7:]
    @pl.when(pl.program_id(1) == 0)
    def _():
        def kn_body(j, best):
            kf = k_ref[0, pl.ds(pl.multiple_of(j * tk, tk), tk), :].astype(F32)
            sq = kf * kf
            for g in range(ATTN_KV_HEADS):
                n2 = jnp.sum(sq[:, g * ATTN_HEAD_DIM:(g + 1) * ATTN_HEAD_DIM], axis=1, keepdims=True)
                best = jnp.maximum(best, jnp.max(n2, axis=0, keepdims=True))
            return best

        kn_ref[...] = jnp.broadcast_to(lax.fori_loop(0, n_kb_max, kn_body, jnp.zeros((1, 1), F32)), kn_ref.shape)

    q_start = q_pos0 + pl.program_id(1) * tq
    n_adm = ((q_start + tq - 1) // CHUNK + 1) * CHUNK
    n_kb = jnp.minimum((n_adm + tk - 1) // tk, n_kb_max)

    k_row = lax.broadcasted_iota(jnp.int32, (tk, tq), 0)
    k_lim = ((q_start + lax.broadcasted_iota(jnp.int32, (tk, tq), 1)) // CHUNK + 1) * CHUNK

    w_t = kx_ref[0].T[IDX_HEAD_DIM:IDX_HEAD_DIM + IDX_HEADS, :] * (IDX_HEADS ** -0.5 * IDX_HEAD_DIM ** -0.5)
    qi = qi_ref[0].reshape(IDX_HEADS * tq, IDX_HEAD_DIM)

    def score_body(j, carry):
        kib = ki_ref[0, pl.ds(pl.multiple_of(j * tk, tk), tk), :]
        d = lax.dot_general(kib, qi, _NT, preferred_element_type=F32)
        sc = jnp.zeros((tk, tq), F32)
        for h in range(IDX_HEADS):
            sc = sc + w_t[h:h + 1, :] * jnp.maximum(d[:, h * tq:(h + 1) * tq], 0.0)
        bits = pltpu.bitcast(sc, jnp.int32)
        key = bits ^ ((bits >> 31) & jnp.int32(0x7FFFFFFF))
        key_ref[j] = jnp.where(k_row < k_lim - j * tk, key, INT_MIN)
        return carry

    lax.fori_loop(0, n_kb, score_body, 0)

    rows_acc = 8 * SUBLANES

    def count_where(pred):
        def body(j, cnt):
            blk = key_ref[j]
            hit = pred(blk, j)
            for r in range(tk // rows_acc):
                cnt = jnp.where(hit[r * rows_acc:(r + 1) * rows_acc], cnt + 1.0, cnt)
            return cnt
        cnt = lax.fori_loop(0, n_kb, body, jnp.zeros((rows_acc, tq), F32))
        return jnp.sum(cnt, axis=0, keepdims=True)

    def bit_body(p, state):
        thr, cnt_thr = state
        cand = thr + jnp.left_shift(jnp.int32(1), 31 - p)
        cnt = count_where(lambda blk, j: blk >= cand)
        keep = cnt >= float(k_top)
        return jnp.where(keep, cand, thr), jnp.where(keep, cnt, cnt_thr)

    def search_bits(lo, hi, state):
        return lax.fori_loop(lo, hi, bit_body, state)

    def settled(state):
        return jnp.min(jnp.where(state[1] == float(k_top), 1.0, 0.0)) > 0.0

    state = search_bits(0, 22, (jnp.full((1, tq), INT_MIN, jnp.int32), jnp.full((1, tq), 2.0 ** 30, F32)))
    for lo in range(22, 32, 2):
        state = lax.cond(settled(state), lambda s: s, functools.partial(search_bits, lo, lo + 2), state)
    thr_raw, cnt_thr = state
    thr = jnp.maximum(thr_raw, INT_MIN + 1)

    tied = jnp.logical_and(cnt_thr > float(k_top), thr_raw > INT_MIN)
    n_tied = jnp.max(jnp.where(tied, 1.0, 0.0))

    @pl.when(n_tied > 0.0)
    def _():
        need = float(k_top) - count_where(lambda blk, j: blk > thr)
        n_bits = max(1, int(n_kb_max * tk - 1).bit_length())

        def pos_body(b, last):
            step = jnp.left_shift(jnp.int32(1), n_bits - 1 - b)
            cand = last + step - 1
            got = count_where(lambda blk, j: jnp.logical_and(blk == thr, k_row + j * tk <= cand))
            return jnp.where(got < need, last + step, last)

        last = lax.fori_loop(0, n_bits, pos_body, jnp.zeros((1, tq), jnp.int32))

        def demote_body(j, carry):
            blk = key_ref[j]
            drop = jnp.logical_and(blk == thr, k_row + j * tk > last)
            key_ref[j] = jnp.where(drop, thr - 1, blk)
            return carry

        lax.fori_loop(0, n_kb, demote_body, 0)

    m_ref[...] = jnp.full(m_ref.shape, NEG_BIG, F32)
    l_ref[...] = jnp.zeros(l_ref.shape, F32)
    acc_ref[...] = jnp.zeros(acc_ref.shape, F32)
    c_exp = ATTN_HEAD_DIM ** -0.5 * LOG2_E
    cols = ATTN_GROUP * tq

    def logits_t(j, g):
        k0 = pl.multiple_of(j * tk, tk)
        kb = k_ref[0, pl.ds(k0, tk), g * ATTN_HEAD_DIM:(g + 1) * ATTN_HEAD_DIM]
        vtb = vt_ref[0, j, g * ATTN_HEAD_DIM:(g + 1) * ATTN_HEAD_DIM, :]
        qg = q_ref[0, g * ATTN_GROUP:(g + 1) * ATTN_GROUP].reshape(cols, ATTN_HEAD_DIM)
        return lax.dot_general(kb, qg, _NT, preferred_element_type=F32), vtb

    qf = q_ref[0].reshape(ATTN_HEADS * tq, ATTN_HEAD_DIM).astype(F32)
    qn2 = lax.dot_general(jnp.ones((SUBLANES, ATTN_HEAD_DIM), BF16), (qf * qf).astype(BF16), _NT,
                          preferred_element_type=F32)
    qn2_max = qn2[0:1, 0:tq]
    for h in range(1, ATTN_HEADS):
        qn2_max = jnp.maximum(qn2_max, qn2[0:1, h * tq:(h + 1) * tq])
    m_bound = jnp.sqrt(qn2_max * kn_ref[...]) * 1.02
    fixed_shift_ok = jnp.max(m_bound) * (2.0 * c_exp) < 100.0

    @pl.when(fixed_shift_ok)
    def _():
        shift = -c_exp * m_bound
        ones = jnp.ones((2 * SUBLANES, tk), BF16)

        def accumulate(j):
            for g in range(ATTN_KV_HEADS):
                vtb = vt_ref[0, j, g * ATTN_HEAD_DIM:(g + 1) * ATTN_HEAD_DIM, :]
                acc_ref[g] += jnp.dot(vtb, p_ref[g], preferred_element_type=F32)
                l_ref[g] += jnp.dot(ones, p_ref[g], preferred_element_type=F32)[0:1]

        def probabilities(j):
            bias = jnp.where(key_ref[j] >= thr, shift, NEG_BIG)
            bias = jnp.concatenate([bias] * ATTN_GROUP, axis=1)
            for g in range(ATTN_KV_HEADS):
                s, _ = logits_t(j, g)
                p_ref[g] = jnp.exp2(s * c_exp + bias).astype(BF16)

        probabilities(0)

        def body(j, carry):
            accumulate(j - 1)
            probabilities(j)
            return carry

        lax.fori_loop(1, n_kb, body, 0)
        accumulate(n_kb - 1)

    @pl.when(jnp.logical_not(fixed_shift_ok))
    def _():
        def body(j, carry):
            bias = jnp.where(key_ref[j] >= thr, 0.0, NEG_BIG)
            bias = jnp.concatenate([bias] * ATTN_GROUP, axis=1)
            for g in range(ATTN_KV_HEADS):
                s, vtb = logits_t(j, g)
                s = s + bias
                m_old = m_ref[g]
                m_new = jnp.maximum(m_old, jnp.max(s, axis=0, keepdims=True))
                alpha = jnp.exp2((m_old - m_new) * c_exp)
                p = jnp.exp2((s - m_new) * c_exp)
                l_ref[g] = alpha * l_ref[g] + jnp.sum(p, axis=0, keepdims=True)
                acc_ref[g] = alpha * acc_ref[g] + jnp.dot(vtb, p.astype(BF16), preferred_element_type=F32)
                m_ref[g] = m_new
            return carry

        lax.fori_loop(0, n_kb, body, 0)

    n_out = o_ref.shape[1]
    for g in range(ATTN_KV_HEADS):
        out_t = acc_ref[g] / l_ref[g]
        for hh in range(ATTN_GROUP):
            h = g * ATTN_GROUP + hh
            o_ref[0, :, h * ATTN_HEAD_DIM:(h + 1) * ATTN_HEAD_DIM] = (
                out_t[:, hh * tq:(hh + 1) * tq].T[:n_out].astype(o_ref.dtype))


def _rows_out_spec(width, n_steps, out_rows, into, dtype):
    b, rows = out_rows[0], out_rows[1]
    if into is None:
        spec = pl.BlockSpec((1, rows, width), lambda bi, i: (bi, i, 0))
        return spec, jax.ShapeDtypeStruct((b, out_rows[2], width), dtype), [], []
    buf, first_row, rows = into
    assert first_row % rows == 0 and buf.shape[0] == 1 and buf.dtype == dtype
    blk0 = first_row // rows
    spec = pl.BlockSpec((1, rows, width), lambda bi, i: (0, blk0 + bi * n_steps + i, 0))
    return spec, jax.ShapeDtypeStruct(buf.shape, dtype), [pl.BlockSpec(memory_space=pl.ANY)], [buf]


def dsa_attention(q, qi, kx, k, v, ki, *, n_q, tq, tk, q_pos0, k_top, v_is_transposed=False,
                  out_rows=None, into=None):
    b = q.shape[0]
    s = k.shape[1]
    assert n_q % tq == 0 and s % tk == 0 and tq == LANES
    n_kb_max = s // tk
    cols = ATTN_GROUP * tq
    vt = v if v_is_transposed else v.reshape(b, n_kb_max, tk, ATTN_KV_DIM).transpose(0, 1, 3, 2)
    assert vt.shape == (b, n_kb_max, ATTN_KV_DIM, tk)
    kern = functools.partial(_attn_kernel, tq=tq, tk=tk, q_pos0=q_pos0, k_top=k_top, n_kb_max=n_kb_max)
    resident = dict(pipeline_mode=pl.Buffered(1))
    out_spec, out_shape, buf_specs, bufs = _rows_out_spec(
        ATTN_Q_DIM, n_q // tq, (b, tq, out_rows or n_q), into, BF16)
    return pl.pallas_call(
        kern,
        grid=(b, n_q // tq),
        in_specs=[
            pl.BlockSpec((1, ATTN_HEADS, tq, ATTN_HEAD_DIM), lambda bi, i: (bi, 0, i, 0)),
            pl.BlockSpec((1, IDX_HEADS, tq, IDX_HEAD_DIM), lambda bi, i: (bi, 0, i, 0)),
            pl.BlockSpec((1, tq, LANES), lambda bi, i: (bi, i, 0)),
            pl.BlockSpec((1, s, ATTN_KV_DIM), lambda bi, i: (bi, 0, 0), **resident),
            pl.BlockSpec((1, n_kb_max, ATTN_KV_DIM, tk), lambda bi, i: (bi, 0, 0, 0), **resident),
            pl.BlockSpec((1, s, IDX_HEAD_DIM), lambda bi, i: (bi, 0, 0), **resident),
        ] + buf_specs,
        out_specs=out_spec,
        out_shape=out_shape,
        input_output_aliases={6: 0} if bufs else {},
        scratch_shapes=[
            pltpu.VMEM((n_kb_max, tk, tq), jnp.int32),
            pltpu.VMEM((ATTN_KV_HEADS, 1, cols), F32),
            pltpu.VMEM((ATTN_KV_HEADS, 1, cols), F32),
            pltpu.VMEM((ATTN_KV_HEADS, ATTN_HEAD_DIM, cols), F32),
            pltpu.VMEM((1, tq), F32),
            pltpu.VMEM((ATTN_KV_HEADS, tk, cols), BF16),
        ],
        compiler_params=_params("parallel", "arbitrary"),
    )(q, qi, kx, k, vt, ki, *bufs)


_LEVELS = (32, 16, 8, 4, 2, 1)


def _hgrn_tables():
    c = CHUNK
    t = np.arange(c)[:, None]
    u = np.arange(c)[None, :]
    mats = [(u <= t), (u > t)]
    masks = []
    for w in _LEVELS:
        r = (t // (2 * w)) * (2 * w) + w - 1
        upper = (t % (2 * w)) >= w
        mats.append((upper & (u > r) & (u <= t)) | ((~upper) & (u > t) & (u <= r)))
        s = np.arange(c)[None, :]
        masks.append(((t // (2 * w)) == (s // (2 * w))) & upper & ((s % (2 * w)) < w))
    masks.append(t == np.arange(c)[None, :])
    table = np.concatenate(mats, axis=0).astype(np.float32)
    return np.concatenate([table] * 3, axis=1), np.stack(masks).astype(np.float32)


def _hgrn_kernel(q_ref, f_ref, i_ref, gt_ref, lb_ref, gn_ref, tab_ref, msk_ref, s0_ref, *rest, layer):
    o_ref, sfin_ref, st_ref = rest[-3:]
    c = CHUNK
    dh = HGRN_HEAD_DIM
    step = pl.program_id(1)

    @pl.when(step == 0)
    def _():
        for h in range(HGRN_HEADS):
            st_ref[h] = s0_ref[0, h].T

    lb_all = lb_ref[...]
    e = jnp.exp(lb_all - jnp.max(lb_all, axis=0, keepdims=True))
    p_lb = e / jnp.sum(e, axis=0, keepdims=True)
    lower = jnp.sum(p_lb[:layer + 1], axis=0, keepdims=True) - p_lb[0:1]

    def chunk_operands(ci):
        rows = slice(ci * c, (ci + 1) * c)
        f = lower + (1.0 - lower) * jax.nn.sigmoid(f_ref[0, rows])
        kk = 1.0 - f
        lf = jnp.log2(f)
        lf_hi = lf.astype(BF16)
        r1 = lf - lf_hi.astype(F32)
        lf_mid = r1.astype(BF16)
        lf_lo = (r1 - lf_mid.astype(F32)).astype(BF16)
        lf3 = jnp.concatenate([lf_hi, lf_mid, lf_lo], axis=0)
        ex = jnp.exp2(jnp.dot(tab_ref[...], lf3, preferred_element_type=F32))
        qq = q_ref[0, rows]
        vv = i_ref[0, rows]
        ops = dict(q16=qq.astype(BF16), k16=kk.astype(BF16), v16=vv.astype(BF16), vv=vv)
        ops["q_lv"] = [(qq * ex[(2 + li) * c:(3 + li) * c]).astype(BF16) for li in range(len(_LEVELS))]
        ops["k_lv"] = [(kk * ex[(2 + li) * c:(3 + li) * c]).astype(BF16) for li in range(len(_LEVELS))]
        ops["q_in"] = (qq * ex[0:c]).astype(BF16)
        ops["k_out"] = (kk * ex[c:2 * c]).astype(BF16)
        ops["decay"] = ex[c - 1:c]
        ops["gate"] = jax.nn.sigmoid(gt_ref[0, rows])
        return ops

    def chunk_outputs(ci, ops):
        rows = slice(ci * c, (ci + 1) * c)
        scores = []
        for h in range(HGRN_HEADS):
            sl = slice(h * dh, (h + 1) * dh)
            s_h = msk_ref[len(_LEVELS)] * lax.dot_general(
                ops["q16"][:, sl], ops["k16"][:, sl], _NT, preferred_element_type=F32)
            for li in range(len(_LEVELS)):
                s_h = s_h + msk_ref[li] * lax.dot_general(
                    ops["q_lv"][li][:, sl], ops["k_lv"][li][:, sl], _NT, preferred_element_type=F32)
            scores.append(s_h.astype(BF16))
        for h in range(HGRN_HEADS):
            sl = slice(h * dh, (h + 1) * dh)
            st = st_ref[h]
            o_h = jnp.dot(scores[h], ops["v16"][:, sl], preferred_element_type=F32)
            o_h = o_h + lax.dot_general(ops["q_in"][:, sl], st.astype(BF16), _NT, preferred_element_type=F32)
            st_ref[h] = st * ops["decay"][:, sl] + jnp.dot(
                ops["vv"][:, sl].T.astype(BF16), ops["k_out"][:, sl], preferred_element_type=F32)
            o_ref[0, rows, sl] = (_rms(o_h, gn_ref[:, sl]) * ops["gate"][:, sl]).astype(o_ref.dtype)

    n_sub = q_ref.shape[1] // c
    ops = chunk_operands(0)
    for ci in range(n_sub):
        ops_next = chunk_operands(ci + 1) if ci + 1 < n_sub else None
        chunk_outputs(ci, ops)
        ops = ops_next

    @pl.when(step == pl.num_programs(1) - 1)
    def _():
        for h in range(HGRN_HEADS):
            sfin_ref[0, h] = st_ref[h].T


def hgrn_scan(p, lb, gnorm, s0, *, layer, n_rows=None, out_rows=None, into=None, states=None):
    b = p.shape[0]
    l = p.shape[1] if n_rows is None else n_rows
    d = D_MODEL
    c = max(r for r in (4 * CHUNK, 2 * CHUNK, CHUNK) if l % r == 0)
    tab, msk = _hgrn_tables()
    kern = functools.partial(_hgrn_kernel, layer=layer)
    col = lambda j: pl.BlockSpec((1, c, d), lambda bi, t, j=j: (bi, t, j))
    state_spec = pl.BlockSpec((1, HGRN_HEADS, HGRN_HEAD_DIM, HGRN_HEAD_DIM), lambda bi, t: (bi, 0, 0, 0))
    out_spec, out_shape, buf_specs, bufs = _rows_out_spec(d, l // c, (b, c, out_rows or l), into, BF16)
    aliases = {9: 0} if bufs else {}
    states_spec = pl.BlockSpec((None,) + state_spec.block_shape, lambda bi, t: (layer, bi, 0, 0, 0))
    if states is not None:
        aliases[9 + len(bufs)] = 1
        buf_specs = buf_specs + [pl.BlockSpec(memory_space=pl.ANY)]
        bufs = bufs + [states]
    return pl.pallas_call(
        kern,
        grid=(b, l // c),
        in_specs=[
            col(0), col(1), col(2), col(3),
            pl.BlockSpec(lb.shape, lambda bi, t: (0, 0)),
            pl.BlockSpec((1, d), lambda bi, t: (0, 0)),
            pl.BlockSpec(tab.shape, lambda bi, t: (0, 0)),
            pl.BlockSpec(msk.shape, lambda bi, t: (0, 0, 0)),
            state_spec,
        ] + buf_specs,
        out_specs=[out_spec, states_spec],
        out_shape=[out_shape, jax.ShapeDtypeStruct((lb.shape[0],) + s0.shape, F32)],
        input_output_aliases=aliases,
        scratch_shapes=[pltpu.VMEM((HGRN_HEADS, HGRN_HEAD_DIM, HGRN_HEAD_DIM), F32)],
        compiler_params=_params("parallel", "arbitrary"),
    )(p, p, p, p, lb, gnorm.reshape(1, d), jnp.asarray(tab, BF16), jnp.asarray(msk), s0, *bufs)


def _attn_layer(x, g, w_in, w_out, layer, new_kv, cache_k, cache_v, cache_kidx, n_p, b_s, t_s):
    q, qi, kx, new_kv, k16, vt, ki16 = attn_in_proj(x, g, w_in, layer=layer, n_prompt=n_p, new_kv=new_kv)
    k_new, v_new, ki_new = new_kv[3][layer], new_kv[4][layer], new_kv[5][layer]
    past = cache_k.shape[1]
    tq = LANES

    o_p = dsa_attention(
        q[None], qi[None], kx[None], k16[None], vt[None], ki16[None], v_is_transposed=True,
        n_q=n_p, tq=tq, tk=vt.shape[2], q_pos0=0, k_top=min(TOPK_MAX, n_p // 4),
        out_rows=n_p + b_s * t_s)

    tk_s = 256
    s_all = past + t_s
    n_adm_pad = ((past + tq - 1) // CHUNK + 1) * CHUNK
    s_pad = -(-max(s_all, n_adm_pad) // tk_s) * tk_s

    def with_cache(cache, new, width):
        full = jnp.zeros((b_s, s_pad, width), BF16)
        full = lax.dynamic_update_slice(full, cache.reshape(b_s, past, width).astype(BF16), (0, 0, 0))
        return lax.dynamic_update_slice(full, new.astype(BF16).reshape(b_s, t_s, width), (0, past, 0))

    def sample_heads(a, heads, dim):
        a = a[:, n_p:].reshape(heads, b_s, t_s, dim).transpose(1, 0, 2, 3)
        return jnp.pad(a, ((0, 0), (0, 0), (0, tq - t_s), (0, 0)))

    kx_s = jnp.pad(kx[n_p:].reshape(b_s, t_s, LANES), ((0, 0), (0, tq - t_s), (0, 0)))
    o_s = dsa_attention(
        sample_heads(q, ATTN_HEADS, ATTN_HEAD_DIM), sample_heads(qi, IDX_HEADS, IDX_HEAD_DIM), kx_s,
        with_cache(cache_k, k_new, ATTN_KV_DIM), with_cache(cache_v, v_new, ATTN_KV_DIM),
        with_cache(cache_kidx, ki_new, IDX_HEAD_DIM),
        n_q=tq, tq=tq, tk=tk_s, q_pos0=past, k_top=min(TOPK_MAX, s_all // 4),
        into=(o_p, n_p, t_s))

    x = matmul_residual(o_s[0], w_out, x, layer=layer)
    return x, new_kv


def _rec_layer(x, g, w_in, w_out, gnorm, rec_lb, state, layer, new_states, n_p, b_s, t_s):
    p = norm_matmul(x, g, w_in, layer=layer)
    s0_p = jnp.zeros((1,) + state.shape[1:], F32)
    o_p, s_p = hgrn_scan(p[None], rec_lb, gnorm, s0_p, layer=layer, n_rows=n_p, out_rows=x.shape[0],
                         states=new_states[0])
    o_s, s_s = hgrn_scan(p[n_p:].reshape(b_s, t_s, -1), rec_lb, gnorm, state, layer=layer,
                         into=(o_p, n_p, t_s),
                         states=new_states[1])
    x = matmul_residual(o_s[0], w_out, x, layer=layer)
    return x, (s_p, s_s)


def kernel(x_prompt, x_sample, cache_k, cache_v, cache_kidx, state_s, norm_mix, norm_mlp, norm_final,
           attn_w_in, attn_w_out, rec_w_in, rec_w_out, rec_gnorm, rec_lb, mlp_w_up, mlp_w_down):
    b_p, l_p, d = x_prompt.shape
    b_s, t_s, _ = x_sample.shape
    assert b_p == 1
    n_p = b_p * l_p
    x = (x_prompt.reshape(n_p, d), x_sample.reshape(b_s * t_s, d))

    new_states = (None, None)
    new_kv = None
    for layer in range(DEPTH):
        j = layer // N_MIXERS
        if layer % N_MIXERS == 0:
            x, new_kv = _attn_layer(
                x, norm_mix[layer], attn_w_in, attn_w_out, j, new_kv,
                cache_k[j], cache_v[j], cache_kidx[j], n_p, b_s, t_s)
        else:
            x, new_states = _rec_layer(
                x, norm_mix[layer], rec_w_in, rec_w_out, rec_gnorm[j], rec_lb, state_s[j],
                j, new_states, n_p, b_s, t_s)
        last = layer == DEPTH - 1
        x = mlp_residual(x, norm_mlp[layer], mlp_w_up, mlp_w_down, layer=layer,
                         g_final=norm_final if last else None, split=n_p if last else None)

    y_p, y_s = x
    kp, vp, kip, ks, vs, kis = new_kv
    n_l = kp.shape[0]
    heads = (ATTN_KV_HEADS, ATTN_HEAD_DIM)
    return (y_p.reshape(b_p, l_p, d), y_s.reshape(b_s, t_s, d),
            kp.reshape(n_l, b_p, l_p, *heads), vp.reshape(n_l, b_p, l_p, *heads),
            kip.reshape(n_l, b_p, l_p, IDX_HEAD_DIM), new_states[0],
            ks.reshape(n_l, b_s, t_s, *heads), vs.reshape(n_l, b_s, t_s, *heads),
            kis.reshape(n_l, b_s, t_s, IDX_HEAD_DIM), new_states[1])
```

```python
import functools
import math

import numpy as np
import jax
import jax.numpy as jnp
from jax import lax
from jax.experimental import pallas as pl
from jax.experimental.pallas import tpu as pltpu

D_MODEL = 2048
DEPTH = 4
CHUNK = 64
N_MIXERS = 2
ATTN_HEADS = 16
ATTN_KV_HEADS = 4
ATTN_HEAD_DIM = D_MODEL // ATTN_HEADS
ATTN_GROUP = ATTN_HEADS // ATTN_KV_HEADS
IDX_HEADS = 16
IDX_HEAD_DIM = 64
TOPK_MAX = 256
HGRN_HEAD_DIM = 128
HGRN_HEADS = D_MODEL // HGRN_HEAD_DIM
D_FF = 4 * D_MODEL
RMS_EPS = 1e-6
ATTN_Q_DIM = ATTN_HEADS * ATTN_HEAD_DIM
ATTN_KV_DIM = ATTN_KV_HEADS * ATTN_HEAD_DIM
IDX_Q_DIM = IDX_HEADS * IDX_HEAD_DIM

LANES = 128
SUBLANES = 8
VMEM_LIMIT = 56 * 1024 * 1024

F32 = jnp.float32
BF16 = jnp.bfloat16
INT_MIN = np.int32(-(2 ** 31))
NEG_BIG = -1e30
LOG2_E = 1.4426950408889634

_NT = (((1,), (1,)), ((), ()))


def _params(*sem):
    return pltpu.CompilerParams(dimension_semantics=sem, vmem_limit_bytes=VMEM_LIMIT)


def _rms(x, g):
    ms = jnp.mean(x * x, axis=-1, keepdims=True)
    return (x * lax.rsqrt(ms + RMS_EPS)) * g


def _norm_matmul_kernel(x_ref, g_ref, w_ref, o_ref, h_ref):
    @pl.when(pl.program_id(1) == 0)
    def _():
        h_ref[...] = _rms(x_ref[...], g_ref[...]).astype(BF16)

    o_ref[...] = jnp.dot(h_ref[...], w_ref[...].astype(BF16), preferred_element_type=F32)


def _row_tile(m, cap=1088):
    return max(t for t in range(16, cap + 1, 16) if m % t == 0)


_ONE_BUFFER = dict(pipeline_mode=pl.Buffered(1))


def norm_matmul(x, g, w, *, layer, tn=1024):
    m, d = x.shape
    n = w.shape[2]
    tm = _row_tile(m)
    assert n % tn == 0
    return pl.pallas_call(
        _norm_matmul_kernel,
        grid=(m // tm, n // tn),
        in_specs=[
            pl.BlockSpec((tm, d), lambda i, j: (i, 0), **_ONE_BUFFER),
            pl.BlockSpec((1, d), lambda i, j: (0, 0)),
            pl.BlockSpec((None, d, tn), lambda i, j: (layer, 0, j)),
        ],
        out_specs=pl.BlockSpec((tm, tn), lambda i, j: (i, j)),
        out_shape=jax.ShapeDtypeStruct((m, n), F32),
        scratch_shapes=[pltpu.VMEM((tm, d), BF16)],
        compiler_params=_params("parallel", "arbitrary"),
    )(x, g.reshape(1, d), w)


def _round_bf16_kernel(w_ref, o_ref):
    o_ref[...] = w_ref[...].astype(BF16)


def round_bf16(w_t, *, layer, n_rows, tn=512):
    d = w_t.shape[2]
    assert n_rows % tn == 0
    return pl.pallas_call(
        _round_bf16_kernel,
        grid=(n_rows // tn,),
        in_specs=[pl.BlockSpec((None, tn, d), lambda j: (layer, j, 0))],
        out_specs=pl.BlockSpec((tn, d), lambda j: (j, 0)),
        out_shape=jax.ShapeDtypeStruct((n_rows, d), BF16),
        compiler_params=_params("parallel"),
    )(w_t)


def _attn_in_proj_kernel(x_ref, g_ref, w_ref, wkx_ref, *rest, n_q, n_qi, prompt, tk):
    n_out = 9 if prompt else 6
    outs, h_ref = rest[-(n_out + 1):-1], rest[-1]
    q_ref, qi_ref, kx_ref, k_ref, v_ref, ki_ref = outs[:6]
    j = pl.program_id(1)

    @pl.when(j == 0)
    def _():
        h_ref[...] = _rms(x_ref[...], g_ref[...]).astype(BF16)

    def tile():
        return lax.dot_general(h_ref[...], w_ref[...], _NT, preferred_element_type=F32)

    @pl.when(j < n_q)
    def _():
        res = tile()
        for hh in range(q_ref.shape[0]):
            q_ref[hh] = res[:, hh * ATTN_HEAD_DIM:(hh + 1) * ATTN_HEAD_DIM].astype(q_ref.dtype)

    @pl.when(j == n_q)
    def _():
        res = tile()
        k_ref[...] = res
        if prompt:
            outs[6][...] = res.astype(BF16)

    @pl.when(j == n_q + 1)
    def _():
        res = tile()
        v_ref[...] = res
        if prompt:
            for t in range(outs[7].shape[0]):
                outs[7][t] = res[t * tk:(t + 1) * tk].T.astype(BF16)

    @pl.when(jnp.logical_and(j >= n_q + 2, j < n_q + 2 + n_qi))
    def _():
        res = tile()
        for hh in range(qi_ref.shape[0]):
            qi_ref[hh] = res[:, hh * IDX_HEAD_DIM:(hh + 1) * IDX_HEAD_DIM].astype(qi_ref.dtype)

    @pl.when(j == n_q + 2 + n_qi)
    def _():
        res = lax.dot_general(h_ref[...], wkx_ref[...].astype(BF16), _NT, preferred_element_type=F32)
        kx_ref[...] = res
        ki_ref[...] = res[:, :IDX_HEAD_DIM]
        if prompt:
            outs[8][...] = res[:, :IDX_HEAD_DIM].astype(BF16)


def attn_in_proj(x, g, w_in, *, layer, n_prompt, new_kv=None):
    x_parts = x if isinstance(x, tuple) else None
    m = sum(part.shape[0] for part in x_parts) if x_parts else x.shape[0]
    d = w_in.shape[1]
    tn = tk = ATTN_KV_DIM
    n_layers = w_in.shape[0]
    n_q, n_qi = ATTN_Q_DIM // tn, IDX_Q_DIM // tn
    n_main = n_q + 2 + n_qi
    tail = w_in.shape[2] - n_main * tn
    w_t = jnp.swapaxes(w_in, 1, 2)
    w_kx = jnp.pad(w_t[layer:layer + 1, n_main * tn:, :], ((0, 0), (0, LANES - tail), (0, 0)))
    w_main = round_bf16(w_t, layer=layer, n_rows=n_main * tn)
    clip = lambda v, hi: jnp.minimum(jnp.maximum(v, 0), hi)

    def project(row0, n_rows, tm, prompt, shared, slabs):
        assert row0 % tm == 0 and n_rows % tm == 0 and (tm % tk == 0 or not prompt)
        r0 = row0 // tm
        x_rows, x_r0 = (x, r0) if x_parts is None else (x_parts[0 if prompt else 1], 0)
        bufs = list(shared or ()) + list(slabs or ())
        aliases = {}
        if shared:
            aliases.update({4 + n: n for n in range(3)})
        if slabs:
            aliases.update({4 + len(shared or ()) + n: 3 + n for n in range(3)})
        slab = lambda width: jax.ShapeDtypeStruct((n_layers, n_rows, width), F32)
        out_specs = [
            pl.BlockSpec((tn // ATTN_HEAD_DIM, tm, ATTN_HEAD_DIM), lambda i, j: (clip(j, n_q - 1), r0 + i, 0)),
            pl.BlockSpec((tn // IDX_HEAD_DIM, tm, IDX_HEAD_DIM),
                         lambda i, j: (clip(j - n_q - 2, n_qi - 1), r0 + i, 0)),
            pl.BlockSpec((tm, LANES), lambda i, j: (r0 + i, 0)),
            pl.BlockSpec((None, tm, tn), lambda i, j: (layer, i, 0)),
            pl.BlockSpec((None, tm, tn), lambda i, j: (layer, i, 0)),
            pl.BlockSpec((None, tm, IDX_HEAD_DIM), lambda i, j: (layer, i, 0)),
        ]
        out_shape = [
            jax.ShapeDtypeStruct((ATTN_HEADS, m, ATTN_HEAD_DIM), BF16),
            jax.ShapeDtypeStruct((IDX_HEADS, m, IDX_HEAD_DIM), BF16),
            jax.ShapeDtypeStruct((m, LANES), F32),
            slab(tn), slab(tn), slab(IDX_HEAD_DIM),
        ]
        if prompt:
            out_specs += [
                pl.BlockSpec((tm, tn), lambda i, j: (i, 0)),
                pl.BlockSpec((tm // tk, tn, tk), lambda i, j: (i, 0, 0)),
                pl.BlockSpec((tm, IDX_HEAD_DIM), lambda i, j: (i, 0)),
            ]
            out_shape += [
                jax.ShapeDtypeStruct((n_rows, tn), BF16),
                jax.ShapeDtypeStruct((n_rows // tk, tn, tk), BF16),
                jax.ShapeDtypeStruct((n_rows, IDX_HEAD_DIM), BF16),
            ]
        return pl.pallas_call(
            functools.partial(_attn_in_proj_kernel, n_q=n_q, n_qi=n_qi, prompt=prompt, tk=tk),
            grid=(n_rows // tm, n_main + 1),
            in_specs=[
                pl.BlockSpec((tm, d), lambda i, j: (x_r0 + i, 0), **_ONE_BUFFER),
                pl.BlockSpec((1, d), lambda i, j: (0, 0)),
                pl.BlockSpec((tn, d), lambda i, j: (jnp.minimum(j, n_main - 1), 0)),
                pl.BlockSpec((None, LANES, d), lambda i, j: (0, 0, 0)),
            ] + [pl.BlockSpec(memory_space=pl.ANY)] * len(bufs),
            out_specs=out_specs,
            out_shape=out_shape,
            input_output_aliases=aliases,
            scratch_shapes=[pltpu.VMEM((tm, d), BF16)],
            compiler_params=_params("parallel", "arbitrary"),
        )(x_rows, g.reshape(1, d), w_main, w_kx, *bufs)

    n_sample = m - n_prompt
    tm_p = max(t for t in (2 * tk, tk) if n_prompt % t == 0)
    tm_s = _row_tile(n_sample, cap=tk)
    prev_p, prev_s = (None, None) if new_kv is None else (new_kv[:3], new_kv[3:])
    q, qi, kx, kp, vp, kip, k16, vt, ki16 = project(0, n_prompt, tm_p, True, None, prev_p)
    q, qi, kx, ks, vs, kis = project(n_prompt, n_sample, tm_s, False, (q, qi, kx), prev_s)
    return q, qi, kx, (kp, vp, kip, ks, vs, kis), k16, vt, ki16


def _matmul_res_kernel(a_ref, w_ref, r_ref, *rest):
    o_ref = rest[-1]
    o_ref[...] = r_ref[...] + jnp.dot(a_ref[...], w_ref[...].astype(BF16), preferred_element_type=F32)


def matmul_residual(a, w, res, *, layer, tn=1024):
    m, k = a.shape
    n = w.shape[2]
    assert n % tn == 0

    def rows(res_rows, res_r0, row0, n_rows, tm, out_buf):
        assert row0 % tm == 0 and n_rows % tm == 0
        r0 = row0 // tm
        bufs = [] if out_buf is None else [out_buf]
        return pl.pallas_call(
            _matmul_res_kernel,
            grid=(n_rows // tm, n // tn),
            in_specs=[
                pl.BlockSpec((tm, k), lambda i, j: (r0 + i, 0)),
                pl.BlockSpec((None, k, tn), lambda i, j: (layer, 0, j)),
                pl.BlockSpec((tm, tn), lambda i, j: (res_r0 + i, j)),
            ] + [pl.BlockSpec(memory_space=pl.ANY)] * len(bufs),
            out_specs=pl.BlockSpec((tm, tn), lambda i, j: (r0 + i, j)),
            out_shape=jax.ShapeDtypeStruct((m, n), F32),
            input_output_aliases={3: 0} if bufs else {},
            compiler_params=_params("parallel", "parallel"),
        )(a, w, res_rows, *bufs)

    if not isinstance(res, tuple):
        return rows(res, 0, 0, m, _row_tile(m), None)
    n_first = res[0].shape[0]
    out = rows(res[0], 0, 0, n_first, _row_tile(n_first), None)
    tm_rest = _row_tile(m - n_first, cap=math.gcd(n_first, m - n_first))
    return rows(res[1], 0, n_first, m - n_first, tm_rest, out)


def _mlp_kernel(x_ref, g_ref, wu_ref, wd_ref, gf_ref, o_ref, *rest, final_norm, tail):
    h_ref = rest[-1]

    @pl.when(pl.program_id(1) == 0)
    def _():
        x = x_ref[...]
        h_ref[...] = _rms(x, g_ref[...]).astype(BF16)
        o_ref[...] = x

    u = jnp.maximum(jnp.dot(h_ref[...], wu_ref[...].astype(BF16), preferred_element_type=F32), 0.0)
    o_ref[...] += jnp.dot((u * u).astype(BF16), wd_ref[...].astype(BF16), preferred_element_type=F32)

    if final_norm:
        @pl.when(pl.program_id(1) == pl.num_programs(1) - 1)
        def _():
            o_ref[...] = _rms(o_ref[...], gf_ref[...])

    if tail is not None:
        tail_ref = rest[0]
        tail_tile, tail_row = tail

        @pl.when(jnp.logical_and(pl.program_id(0) == tail_tile, pl.program_id(1) == pl.num_programs(1) - 1))
        def _():
            tail_ref[...] = o_ref[tail_row:tail_row + tail_ref.shape[0]]


def mlp_residual(x, g, w_up, w_down, *, layer, g_final=None, split=None, tf=512):
    m, d = x.shape
    f = w_up.shape[2]
    tm = _row_tile(m)
    assert f % tf == 0
    final_norm = g_final is not None
    gf = (g_final if final_norm else g).reshape(1, d)
    out_spec = pl.BlockSpec((tm, d), lambda i, j: (i, 0), **_ONE_BUFFER)
    if split is None:
        tail, out_specs, out_shape = None, out_spec, jax.ShapeDtypeStruct((m, d), F32)
    else:
        tail = (split // tm, split % tm)
        assert m - split == tm - tail[1] and tail[1] % SUBLANES == 0
        out_specs = [out_spec, pl.BlockSpec((m - split, d), lambda i, j: (0, 0))]
        out_shape = [jax.ShapeDtypeStruct((split, d), F32), jax.ShapeDtypeStruct((m - split, d), F32)]
    return pl.pallas_call(
        functools.partial(_mlp_kernel, final_norm=final_norm, tail=tail),
        grid=(m // tm, f // tf),
        in_specs=[
            pl.BlockSpec((tm, d), lambda i, j: (i, 0), **_ONE_BUFFER),
            pl.BlockSpec((1, d), lambda i, j: (0, 0)),
            pl.BlockSpec((None, d, tf), lambda i, j: (layer, 0, j)),
            pl.BlockSpec((None, tf, d), lambda i, j: (layer, j, 0)),
            pl.BlockSpec((1, d), lambda i, j: (0, 0)),
        ],
        out_specs=out_specs,
        out_shape=out_shape,
        scratch_shapes=[pltpu.VMEM((tm, d), BF16)],
        compiler_params=_params("parallel", "arbitrary"),
    )(x, g.reshape(1, d), w_up, w_down, gf)


def _attn_kernel(q_ref, qi_ref, kx_ref, k_ref, vt_ref, ki_ref, *rest, tq, tk, q_pos0, k_top, n_kb_max):
    o_ref, key_ref, m_ref, l_ref, acc_ref, kn_ref, p_ref = rest[-7:]
    @pl.when(pl.program_id(1) == 0)
    def _():
        def kn_body(j, best):
            kf = k_ref[0, pl.ds(pl.multiple_of(j * tk, tk), tk), :].astype(F32)
            sq = kf * kf
            for g in range(ATTN_KV_HEADS):
                n2 = jnp.sum(sq[:, g * ATTN_HEAD_DIM:(g + 1) * ATTN_HEAD_DIM], axis=1, keepdims=True)
                best = jnp.maximum(best, jnp.max(n2, axis=0, keepdims=True))
            return best

        kn_ref[...] = jnp.broadcast_to(lax.fori_loop(0, n_kb_max, kn_body, jnp.zeros((1, 1), F32)), kn_ref.shape)

    q_start = q_pos0 + pl.program_id(1) * tq
    n_adm = ((q_start + tq - 1) // CHUNK + 1) * CHUNK
    n_kb = jnp.minimum((n_adm + tk - 1) // tk, n_kb_max)

    k_row = lax.broadcasted_iota(jnp.int32, (tk, tq), 0)
    k_lim = ((q_start + lax.broadcasted_iota(jnp.int32, (tk, tq), 1)) // CHUNK + 1) * CHUNK

    w_t = kx_ref[0].T[IDX_HEAD_DIM:IDX_HEAD_DIM + IDX_HEADS, :] * (IDX_HEADS ** -0.5 * IDX_HEAD_DIM ** -0.5)
    qi = qi_ref[0].reshape(IDX_HEADS * tq, IDX_HEAD_DIM)

    def score_body(j, carry):
        kib = ki_ref[0, pl.ds(pl.multiple_of(j * tk, tk), tk), :]
        d = lax.dot_general(kib, qi, _NT, preferred_element_type=F32)
        sc = jnp.zeros((tk, tq), F32)
        for h in range(IDX_HEADS):
            sc = sc + w_t[h:h + 1, :] * jnp.maximum(d[:, h * tq:(h + 1) * tq], 0.0)
        bits = pltpu.bitcast(sc, jnp.int32)
        key = bits ^ ((bits >> 31) & jnp.int32(0x7FFFFFFF))
        key_ref[j] = jnp.where(k_row < k_lim - j * tk, key, INT_MIN)
        return carry

    lax.fori_loop(0, n_kb, score_body, 0)

    rows_acc = 8 * SUBLANES

    def count_where(pred):
        def body(j, cnt):
            blk = key_ref[j]
            hit = pred(blk, j)
            for r in range(tk // rows_acc):
                cnt = jnp.where(hit[r * rows_acc:(r + 1) * rows_acc], cnt + 1.0, cnt)
            return cnt
        cnt = lax.fori_loop(0, n_kb, body, jnp.zeros((rows_acc, tq), F32))
        return jnp.sum(cnt, axis=0, keepdims=True)

    def bit_body(p, state):
        thr, cnt_thr = state
        cand = thr + jnp.left_shift(jnp.int32(1), 31 - p)
        cnt = count_where(lambda blk, j: blk >= cand)
        keep = cnt >= float(k_top)
        return jnp.where(keep, cand, thr), jnp.where(keep, cnt, cnt_thr)

    def search_bits(lo, hi, state):
        return lax.fori_loop(lo, hi, bit_body, state)

    def settled(state):
        return jnp.min(jnp.where(state[1] == float(k_top), 1.0, 0.0)) > 0.0

    state = search_bits(0, 24, (jnp.full((1, tq), INT_MIN, jnp.int32), jnp.full((1, tq), 2.0 ** 30, F32)))
    for lo in range(24, 32, 2):
        state = lax.cond(settled(state), lambda s: s, functools.partial(search_bits, lo, lo + 2), state)
    thr_raw, cnt_thr = state
    thr = jnp.maximum(thr_raw, INT_MIN + 1)

    tied = jnp.logical_and(cnt_thr > float(k_top), thr_raw > INT_MIN)
    n_tied = jnp.max(jnp.where(tied, 1.0, 0.0))

    @pl.when(n_tied > 0.0)
    def _():
        need = float(k_top) - count_where(lambda blk, j: blk > thr)
        n_bits = max(1, int(n_kb_max * tk - 1).bit_length())

        def pos_body(b, last):
            step = jnp.left_shift(jnp.int32(1), n_bits - 1 - b)
            cand = last + step - 1
            got = count_where(lambda blk, j: jnp.logical_and(blk == thr, k_row + j * tk <= cand))
            return jnp.where(got < need, last + step, last)

        last = lax.fori_loop(0, n_bits, pos_body, jnp.zeros((1, tq), jnp.int32))

        def demote_body(j, carry):
            blk = key_ref[j]
            drop = jnp.logical_and(blk == thr, k_row + j * tk > last)
            key_ref[j] = jnp.where(drop, thr - 1, blk)
            return carry

        lax.fori_loop(0, n_kb, demote_body, 0)

    m_ref[...] = jnp.full(m_ref.shape, NEG_BIG, F32)
    l_ref[...] = jnp.zeros(l_ref.shape, F32)
    acc_ref[...] = jnp.zeros(acc_ref.shape, F32)
    c_exp = ATTN_HEAD_DIM ** -0.5 * LOG2_E
    cols = ATTN_GROUP * tq

    def logits_t(j, g):
        k0 = pl.multiple_of(j * tk, tk)
        kb = k_ref[0, pl.ds(k0, tk), g * ATTN_HEAD_DIM:(g + 1) * ATTN_HEAD_DIM]
        vtb = vt_ref[0, j, g * ATTN_HEAD_DIM:(g + 1) * ATTN_HEAD_DIM, :]
        qg = q_ref[0, g * ATTN_GROUP:(g + 1) * ATTN_GROUP].reshape(cols, ATTN_HEAD_DIM)
        return lax.dot_general(kb, qg, _NT, preferred_element_type=F32), vtb

    qf = q_ref[0].reshape(ATTN_HEADS * tq, ATTN_HEAD_DIM).astype(F32)
    qn2 = lax.dot_general(jnp.ones((SUBLANES, ATTN_HEAD_DIM), BF16), (qf * qf).astype(BF16), _NT,
                          preferred_element_type=F32)
    qn2_max = qn2[0:1, 0:tq]
    for h in range(1, ATTN_HEADS):
        qn2_max = jnp.maximum(qn2_max, qn2[0:1, h * tq:(h + 1) * tq])
    m_bound = jnp.sqrt(qn2_max * kn_ref[...]) * 1.02
    fixed_shift_ok = jnp.max(m_bound) * (2.0 * c_exp) < 100.0

    @pl.when(fixed_shift_ok)
    def _():
        shift = -c_exp * m_bound
        ones = jnp.ones((2 * SUBLANES, tk), BF16)

        def accumulate(j):
            for g in range(ATTN_KV_HEADS):
                vtb = vt_ref[0, j, g * ATTN_HEAD_DIM:(g + 1) * ATTN_HEAD_DIM, :]
                acc_ref[g] += jnp.dot(vtb, p_ref[g], preferred_element_type=F32)
                l_ref[g] += jnp.dot(ones, p_ref[g], preferred_element_type=F32)[0:1]

        def probabilities(j):
            bias = jnp.where(key_ref[j] >= thr, shift, NEG_BIG)
            bias = jnp.concatenate([bias] * ATTN_GROUP, axis=1)
            for g in range(ATTN_KV_HEADS):
                s, _ = logits_t(j, g)
                p_ref[g] = jnp.exp2(s * c_exp + bias).astype(BF16)

        probabilities(0)

        def body(j, carry):
            accumulate(j - 1)
            probabilities(j)
            return carry

        lax.fori_loop(1, n_kb, body, 0)
        accumulate(n_kb - 1)

    @pl.when(jnp.logical_not(fixed_shift_ok))
    def _():
        def body(j, carry):
            bias = jnp.where(key_ref[j] >= thr, 0.0, NEG_BIG)
            bias = jnp.concatenate([bias] * ATTN_GROUP, axis=1)
            for g in range(ATTN_KV_HEADS):
                s, vtb = logits_t(j, g)
                s = s + bias
                m_old = m_ref[g]
                m_new = jnp.maximum(m_old, jnp.max(s, axis=0, keepdims=True))
                alpha = jnp.exp2((m_old - m_new) * c_exp)
                p = jnp.exp2((s - m_new) * c_exp)
                l_ref[g] = alpha * l_ref[g] + jnp.sum(p, axis=0, keepdims=True)
                acc_ref[g] = alpha * acc_ref[g] + jnp.dot(vtb, p.astype(BF16), preferred_element_type=F32)
                m_ref[g] = m_new
            return carry

        lax.fori_loop(0, n_kb, body, 0)

    n_out = o_ref.shape[1]
    for g in range(ATTN_KV_HEADS):
        out_t = acc_ref[g] / l_ref[g]
        for hh in range(ATTN_GROUP):
            h = g * ATTN_GROUP + hh
            o_ref[0, :, h * ATTN_HEAD_DIM:(h + 1) * ATTN_HEAD_DIM] = (
                out_t[:, hh * tq:(hh + 1) * tq].T[:n_out].astype(o_ref.dtype))


def _rows_out_spec(width, n_steps, out_rows, into, dtype):
    b, rows = out_rows[0], out_rows[1]
    if into is None:
        spec = pl.BlockSpec((1, rows, width), lambda bi, i: (bi, i, 0))
        return spec, jax.ShapeDtypeStruct((b, out_rows[2], width), dtype), [], []
    buf, first_row, rows = into
    assert first_row % rows == 0 and buf.shape[0] == 1 and buf.dtype == dtype
    blk0 = first_row // rows
    spec = pl.BlockSpec((1, rows, width), lambda bi, i: (0, blk0 + bi * n_steps + i, 0))
    return spec, jax.ShapeDtypeStruct(buf.shape, dtype), [pl.BlockSpec(memory_space=pl.ANY)], [buf]


def dsa_attention(q, qi, kx, k, v, ki, *, n_q, tq, tk, q_pos0, k_top, v_is_transposed=False,
                  out_rows=None, into=None):
    b = q.shape[0]
    s = k.shape[1]
    assert n_q % tq == 0 and s % tk == 0 and tq == LANES
    n_kb_max = s // tk
    cols = ATTN_GROUP * tq
    vt = v if v_is_transposed else v.reshape(b, n_kb_max, tk, ATTN_KV_DIM).transpose(0, 1, 3, 2)
    assert vt.shape == (b, n_kb_max, ATTN_KV_DIM, tk)
    kern = functools.partial(_attn_kernel, tq=tq, tk=tk, q_pos0=q_pos0, k_top=k_top, n_kb_max=n_kb_max)
    resident = dict(pipeline_mode=pl.Buffered(1))
    out_spec, out_shape, buf_specs, bufs = _rows_out_spec(
        ATTN_Q_DIM, n_q // tq, (b, tq, out_rows or n_q), into, BF16)
    return pl.pallas_call(
        kern,
        grid=(b, n_q // tq),
        in_specs=[
            pl.BlockSpec((1, ATTN_HEADS, tq, ATTN_HEAD_DIM), lambda bi, i: (bi, 0, i, 0)),
            pl.BlockSpec((1, IDX_HEADS, tq, IDX_HEAD_DIM), lambda bi, i: (bi, 0, i, 0)),
            pl.BlockSpec((1, tq, LANES), lambda bi, i: (bi, i, 0)),
            pl.BlockSpec((1, s, ATTN_KV_DIM), lambda bi, i: (bi, 0, 0), **resident),
            pl.BlockSpec((1, n_kb_max, ATTN_KV_DIM, tk), lambda bi, i: (bi, 0, 0, 0), **resident),
            pl.BlockSpec((1, s, IDX_HEAD_DIM), lambda bi, i: (bi, 0, 0), **resident),
        ] + buf_specs,
        out_specs=out_spec,
        out_shape=out_shape,
        input_output_aliases={6: 0} if bufs else {},
        scratch_shapes=[
            pltpu.VMEM((n_kb_max, tk, tq), jnp.int32),
            pltpu.VMEM((ATTN_KV_HEADS, 1, cols), F32),
            pltpu.VMEM((ATTN_KV_HEADS, 1, cols), F32),
            pltpu.VMEM((ATTN_KV_HEADS, ATTN_HEAD_DIM, cols), F32),
            pltpu.VMEM((1, tq), F32),
            pltpu.VMEM((ATTN_KV_HEADS, tk, cols), BF16),
        ],
        compiler_params=_params("parallel", "arbitrary"),
    )(q, qi, kx, k, vt, ki, *bufs)


_LEVELS = (32, 16, 8, 4, 2, 1)


def _hgrn_tables():
    c = CHUNK
    t = np.arange(c)[:, None]
    u = np.arange(c)[None, :]
    mats = [(u <= t), (u > t)]
    masks = []
    for w in _LEVELS:
        r = (t // (2 * w)) * (2 * w) + w - 1
        upper = (t % (2 * w)) >= w
        mats.append((upper & (u > r) & (u <= t)) | ((~upper) & (u > t) & (u <= r)))
        s = np.arange(c)[None, :]
        masks.append(((t // (2 * w)) == (s // (2 * w))) & upper & ((s % (2 * w)) < w))
    masks.append(t == np.arange(c)[None, :])
    table = np.concatenate(mats, axis=0).astype(np.float32)
    return np.concatenate([table] * 3, axis=1), np.stack(masks).astype(np.float32)


def _hgrn_kernel(q_ref, f_ref, i_ref, gt_ref, lb_ref, gn_ref, tab_ref, msk_ref, s0_ref, *rest, layer):
    o_ref, sfin_ref, st_ref = rest[-3:]
    c = CHUNK
    dh = HGRN_HEAD_DIM
    step = pl.program_id(1)

    @pl.when(step == 0)
    def _():
        for h in range(HGRN_HEADS):
            st_ref[h] = s0_ref[0, h].T

    lb_all = lb_ref[...]
    e = jnp.exp(lb_all - jnp.max(lb_all, axis=0, keepdims=True))
    p_lb = e / jnp.sum(e, axis=0, keepdims=True)
    lower = jnp.sum(p_lb[:layer + 1], axis=0, keepdims=True) - p_lb[0:1]

    def chunk_operands(ci):
        rows = slice(ci * c, (ci + 1) * c)
        f = lower + (1.0 - lower) * jax.nn.sigmoid(f_ref[0, rows])
        kk = 1.0 - f
        lf = jnp.log2(f)
        lf_hi = lf.astype(BF16)
        r1 = lf - lf_hi.astype(F32)
        lf_mid = r1.astype(BF16)
        lf_lo = (r1 - lf_mid.astype(F32)).astype(BF16)
        lf3 = jnp.concatenate([lf_hi, lf_mid, lf_lo], axis=0)
        ex = jnp.exp2(jnp.dot(tab_ref[...], lf3, preferred_element_type=F32))
        qq = q_ref[0, rows]
        vv = i_ref[0, rows]
        ops = dict(q16=qq.astype(BF16), k16=kk.astype(BF16), v16=vv.astype(BF16), vv=vv)
        ops["q_lv"] = [(qq * ex[(2 + li) * c:(3 + li) * c]).astype(BF16) for li in range(len(_LEVELS))]
        ops["k_lv"] = [(kk * ex[(2 + li) * c:(3 + li) * c]).astype(BF16) for li in range(len(_LEVELS))]
        ops["q_in"] = (qq * ex[0:c]).astype(BF16)
        ops["k_out"] = (kk * ex[c:2 * c]).astype(BF16)
        ops["decay"] = ex[c - 1:c]
        ops["gate"] = jax.nn.sigmoid(gt_ref[0, rows])
        return ops

    def chunk_outputs(ci, ops):
        rows = slice(ci * c, (ci + 1) * c)
        scores = []
        for h in range(HGRN_HEADS):
            sl = slice(h * dh, (h + 1) * dh)
            s_h = msk_ref[len(_LEVELS)] * lax.dot_general(
                ops["q16"][:, sl], ops["k16"][:, sl], _NT, preferred_element_type=F32)
            for li in range(len(_LEVELS)):
                s_h = s_h + msk_ref[li] * lax.dot_general(
                    ops["q_lv"][li][:, sl], ops["k_lv"][li][:, sl], _NT, preferred_element_type=F32)
            scores.append(s_h.astype(BF16))
        for h in range(HGRN_HEADS):
            sl = slice(h * dh, (h + 1) * dh)
            st = st_ref[h]
            o_h = jnp.dot(scores[h], ops["v16"][:, sl], preferred_element_type=F32)
            o_h = o_h + lax.dot_general(ops["q_in"][:, sl], st.astype(BF16), _NT, preferred_element_type=F32)
            st_ref[h] = st * ops["decay"][:, sl] + jnp.dot(
                ops["vv"][:, sl].T.astype(BF16), ops["k_out"][:, sl], preferred_element_type=F32)
            o_ref[0, rows, sl] = (_rms(o_h, gn_ref[:, sl]) * ops["gate"][:, sl]).astype(o_ref.dtype)

    n_sub = q_ref.shape[1] // c
    ops = chunk_operands(0)
    for ci in range(n_sub):
        ops_next = chunk_operands(ci + 1) if ci + 1 < n_sub else None
        chunk_outputs(ci, ops)
        ops = ops_next

    @pl.when(step == pl.num_programs(1) - 1)
    def _():
        for h in range(HGRN_HEADS):
            sfin_ref[0, h] = st_ref[h].T


def hgrn_scan(p, lb, gnorm, s0, *, layer, n_rows=None, out_rows=None, into=None, states=None):
    b = p.shape[0]
    l = p.shape[1] if n_rows is None else n_rows
    d = D_MODEL
    c = max(r for r in (4 * CHUNK, 2 * CHUNK, CHUNK) if l % r == 0)
    tab, msk = _hgrn_tables()
    kern = functools.partial(_hgrn_kernel, layer=layer)
    col = lambda j: pl.BlockSpec((1, c, d), lambda bi, t, j=j: (bi, t, j))
    state_spec = pl.BlockSpec((1, HGRN_HEADS, HGRN_HEAD_DIM, HGRN_HEAD_DIM), lambda bi, t: (bi, 0, 0, 0))
    out_spec, out_shape, buf_specs, bufs = _rows_out_spec(d, l // c, (b, c, out_rows or l), into, BF16)
    aliases = {9: 0} if bufs else {}
    states_spec = pl.BlockSpec((None,) + state_spec.block_shape, lambda bi, t: (layer, bi, 0, 0, 0))
    if states is not None:
        aliases[9 + len(bufs)] = 1
        buf_specs = buf_specs + [pl.BlockSpec(memory_space=pl.ANY)]
        bufs = bufs + [states]
    return pl.pallas_call(
        kern,
        grid=(b, l // c),
        in_specs=[
            col(0), col(1), col(2), col(3),
            pl.BlockSpec(lb.shape, lambda bi, t: (0, 0)),
            pl.BlockSpec((1, d), lambda bi, t: (0, 0)),
            pl.BlockSpec(tab.shape, lambda bi, t: (0, 0)),
            pl.BlockSpec(msk.shape, lambda bi, t: (0, 0, 0)),
            state_spec,
        ] + buf_specs,
        out_specs=[out_spec, states_spec],
        out_shape=[out_shape, jax.ShapeDtypeStruct((lb.shape[0],) + s0.shape, F32)],
        input_output_aliases=aliases,
        scratch_shapes=[pltpu.VMEM((HGRN_HEADS, HGRN_HEAD_DIM, HGRN_HEAD_DIM), F32)],
        compiler_params=_params("parallel", "arbitrary"),
    )(p, p, p, p, lb, gnorm.reshape(1, d), jnp.asarray(tab, BF16), jnp.asarray(msk), s0, *bufs)


def _attn_layer(x, g, w_in, w_out, layer, new_kv, cache_k, cache_v, cache_kidx, n_p, b_s, t_s):
    q, qi, kx, new_kv, k16, vt, ki16 = attn_in_proj(x, g, w_in, layer=layer, n_prompt=n_p, new_kv=new_kv)
    k_new, v_new, ki_new = new_kv[3][layer], new_kv[4][layer], new_kv[5][layer]
    past = cache_k.shape[1]
    tq = LANES

    o_p = dsa_attention(
        q[None], qi[None], kx[None], k16[None], vt[None], ki16[None], v_is_transposed=True,
        n_q=n_p, tq=tq, tk=vt.shape[2], q_pos0=0, k_top=min(TOPK_MAX, n_p // 4),
        out_rows=n_p + b_s * t_s)

    s_all = past + t_s
    n_adm_pad = ((past + tq - 1) // CHUNK + 1) * CHUNK
    s_pad = -(-max(s_all, n_adm_pad) // LANES) * LANES
    tk_s = s_pad

    def with_cache(cache, new, width):
        full = jnp.zeros((b_s, s_pad, width), BF16)
        full = lax.dynamic_update_slice(full, cache.reshape(b_s, past, width).astype(BF16), (0, 0, 0))
        return lax.dynamic_update_slice(full, new.astype(BF16).reshape(b_s, t_s, width), (0, past, 0))

    def sample_heads(a, heads, dim):
        a = a[:, n_p:].reshape(heads, b_s, t_s, dim).transpose(1, 0, 2, 3)
        return jnp.pad(a, ((0, 0), (0, 0), (0, tq - t_s), (0, 0)))

    kx_s = jnp.pad(kx[n_p:].reshape(b_s, t_s, LANES), ((0, 0), (0, tq - t_s), (0, 0)))
    o_s = dsa_attention(
        sample_heads(q, ATTN_HEADS, ATTN_HEAD_DIM), sample_heads(qi, IDX_HEADS, IDX_HEAD_DIM), kx_s,
        with_cache(cache_k, k_new, ATTN_KV_DIM), with_cache(cache_v, v_new, ATTN_KV_DIM),
        with_cache(cache_kidx, ki_new, IDX_HEAD_DIM),
        n_q=tq, tq=tq, tk=tk_s, q_pos0=past, k_top=min(TOPK_MAX, s_all // 4),
        into=(o_p, n_p, t_s))

    x = matmul_residual(o_s[0], w_out, x, layer=layer)
    return x, new_kv


def _rec_layer(x, g, w_in, w_out, gnorm, rec_lb, state, layer, new_states, n_p, b_s, t_s):
    p = norm_matmul(x, g, w_in, layer=layer)
    s0_p = jnp.zeros((1,) + state.shape[1:], F32)
    o_p, s_p = hgrn_scan(p[None], rec_lb, gnorm, s0_p, layer=layer, n_rows=n_p, out_rows=x.shape[0],
                         states=new_states[0])
    o_s, s_s = hgrn_scan(p[n_p:].reshape(b_s, t_s, -1), rec_lb, gnorm, state, layer=layer,
                         into=(o_p, n_p, t_s),
                         states=new_states[1])
    x = matmul_residual(o_s[0], w_out, x, layer=layer)
    return x, (s_p, s_s)


def kernel(x_prompt, x_sample, cache_k, cache_v, cache_kidx, state_s, norm_mix, norm_mlp, norm_final,
           attn_w_in, attn_w_out, rec_w_in, rec_w_out, rec_gnorm, rec_lb, mlp_w_up, mlp_w_down):
    b_p, l_p, d = x_prompt.shape
    b_s, t_s, _ = x_sample.shape
    assert b_p == 1
    n_p = b_p * l_p
    x = (x_prompt.reshape(n_p, d), x_sample.reshape(b_s * t_s, d))

    new_states = (None, None)
    new_kv = None
    for layer in range(DEPTH):
        j = layer // N_MIXERS
        if layer % N_MIXERS == 0:
            x, new_kv = _attn_layer(
                x, norm_mix[layer], attn_w_in, attn_w_out, j, new_kv,
                cache_k[j], cache_v[j], cache_kidx[j], n_p, b_s, t_s)
        else:
            x, new_states = _rec_layer(
                x, norm_mix[layer], rec_w_in, rec_w_out, rec_gnorm[j], rec_lb, state_s[j],
                j, new_states, n_p, b_s, t_s)
        last = layer == DEPTH - 1
        x = mlp_residual(x, norm_mlp[layer], mlp_w_up, mlp_w_down, layer=layer,
                         g_final=norm_final if last else None, split=n_p if last else None)

    y_p, y_s = x
    kp, vp, kip, ks, vs, kis = new_kv
    n_l = kp.shape[0]
    heads = (ATTN_KV_HEADS, ATTN_HEAD_DIM)
    return (y_p.reshape(b_p, l_p, d), y_s.reshape(b_s, t_s, d),
            kp.reshape(n_l, b_p, l_p, *heads), vp.reshape(n_l, b_p, l_p, *heads),
            kip.reshape(n_l, b_p, l_p, IDX_HEAD_DIM), new_states[0],
            ks.reshape(n_l, b_s, t_s, *heads), vs.reshape(n_l, b_s, t_s, *heads),
            kis.reshape(n_l, b_s, t_s, IDX_HEAD_DIM), new_states[1])
```

```python
import functools
import math

import numpy as np
import jax
import jax.numpy as jnp
from jax import lax
from jax.experimental import pallas as pl
from jax.experimental.pallas import tpu as pltpu

D_MODEL = 2048
DEPTH = 4
CHUNK = 64
N_MIXERS = 2
ATTN_HEADS = 16
ATTN_KV_HEADS = 4
ATTN_HEAD_DIM = D_MODEL // ATTN_HEADS
ATTN_GROUP = ATTN_HEADS // ATTN_KV_HEADS
IDX_HEADS = 16
IDX_HEAD_DIM = 64
TOPK_MAX = 256
HGRN_HEAD_DIM = 128
HGRN_HEADS = D_MODEL // HGRN_HEAD_DIM
D_FF = 4 * D_MODEL
RMS_EPS = 1e-6
ATTN_Q_DIM = ATTN_HEADS * ATTN_HEAD_DIM
ATTN_KV_DIM = ATTN_KV_HEADS * ATTN_HEAD_DIM
IDX_Q_DIM = IDX_HEADS * IDX_HEAD_DIM

LANES = 128
SUBLANES = 8
VMEM_LIMIT = 56 * 1024 * 1024

F32 = jnp.float32
BF16 = jnp.bfloat16
INT_MIN = np.int32(-(2 ** 31))
NEG_BIG = -1e30
LOG2_E = 1.4426950408889634

_NT = (((1,), (1,)), ((), ()))


def _params(*sem):
    return pltpu.CompilerParams(dimension_semantics=sem, vmem_limit_bytes=VMEM_LIMIT)


def _rms(x, g):
    ms = jnp.mean(x * x, axis=-1, keepdims=True)
    return (x * lax.rsqrt(ms + RMS_EPS)) * g


def _norm_matmul_kernel(x_ref, g_ref, w_ref, o_ref, h_ref):
    @pl.when(pl.program_id(1) == 0)
    def _():
        h_ref[...] = _rms(x_ref[...], g_ref[...]).astype(BF16)

    o_ref[...] = jnp.dot(h_ref[...], w_ref[...].astype(BF16), preferred_element_type=F32)


def _row_tile(m, cap=1088):
    return max(t for t in range(16, cap + 1, 16) if m % t == 0)


_ONE_BUFFER = dict(pipeline_mode=pl.Buffered(1))


def norm_matmul(x, g, w, *, layer, tn=1024):
    m, d = x.shape
    n = w.shape[2]
    tm = _row_tile(m)
    assert n % tn == 0
    return pl.pallas_call(
        _norm_matmul_kernel,
        grid=(m // tm, n // tn),
        in_specs=[
            pl.BlockSpec((tm, d), lambda i, j: (i, 0), **_ONE_BUFFER),
            pl.BlockSpec((1, d), lambda i, j: (0, 0)),
            pl.BlockSpec((None, d, tn), lambda i, j: (layer, 0, j)),
        ],
        out_specs=pl.BlockSpec((tm, tn), lambda i, j: (i, j)),
        out_shape=jax.ShapeDtypeStruct((m, n), F32),
        scratch_shapes=[pltpu.VMEM((tm, d), BF16)],
        compiler_params=_params("parallel", "arbitrary"),
    )(x, g.reshape(1, d), w)


def _round_bf16_kernel(w_ref, o_ref):
    o_ref[...] = w_ref[...].astype(BF16)


def round_bf16(w_t, *, layer, n_rows, tn=512):
    d = w_t.shape[2]
    assert n_rows % tn == 0
    return pl.pallas_call(
        _round_bf16_kernel,
        grid=(n_rows // tn,),
        in_specs=[pl.BlockSpec((None, tn, d), lambda j: (layer, j, 0))],
        out_specs=pl.BlockSpec((tn, d), lambda j: (j, 0)),
        out_shape=jax.ShapeDtypeStruct((n_rows, d), BF16),
        compiler_params=_params("parallel"),
    )(w_t)


def _attn_in_proj_kernel(x_ref, g_ref, w_ref, wkx_ref, *rest, n_q, n_qi, prompt, tk):
    n_out = 9 if prompt else 6
    outs, h_ref = rest[-(n_out + 1):-1], rest[-1]
    q_ref, qi_ref, kx_ref, k_ref, v_ref, ki_ref = outs[:6]
    j = pl.program_id(1)

    @pl.when(j == 0)
    def _():
        h_ref[...] = _rms(x_ref[...], g_ref[...]).astype(BF16)

    def tile():
        return lax.dot_general(h_ref[...], w_ref[...], _NT, preferred_element_type=F32)

    @pl.when(j < n_q)
    def _():
        res = tile()
        for hh in range(q_ref.shape[0]):
            q_ref[hh] = res[:, hh * ATTN_HEAD_DIM:(hh + 1) * ATTN_HEAD_DIM].astype(q_ref.dtype)

    @pl.when(j == n_q)
    def _():
        res = tile()
        k_ref[...] = res
        if prompt:
            outs[6][...] = res.astype(BF16)

    @pl.when(j == n_q + 1)
    def _():
        res = tile()
        v_ref[...] = res
        if prompt:
            for t in range(outs[7].shape[0]):
                outs[7][t] = res[t * tk:(t + 1) * tk].T.astype(BF16)

    @pl.when(jnp.logical_and(j >= n_q + 2, j < n_q + 2 + n_qi))
    def _():
        res = tile()
        for hh in range(qi_ref.shape[0]):
            qi_ref[hh] = res[:, hh * IDX_HEAD_DIM:(hh + 1) * IDX_HEAD_DIM].astype(qi_ref.dtype)

    @pl.when(j == n_q + 2 + n_qi)
    def _():
        res = lax.dot_general(h_ref[...], wkx_ref[...].astype(BF16), _NT, preferred_element_type=F32)
        kx_ref[...] = res
        ki_ref[...] = res[:, :IDX_HEAD_DIM]
        if prompt:
            outs[8][...] = res[:, :IDX_HEAD_DIM].astype(BF16)


def attn_in_proj(x, g, w_in, *, layer, n_prompt, new_kv=None):
    x_parts = x if isinstance(x, tuple) else None
    m = sum(part.shape[0] for part in x_parts) if x_parts else x.shape[0]
    d = w_in.shape[1]
    tn = tk = ATTN_KV_DIM
    n_layers = w_in.shape[0]
    n_q, n_qi = ATTN_Q_DIM // tn, IDX_Q_DIM // tn
    n_main = n_q + 2 + n_qi
    tail = w_in.shape[2] - n_main * tn
    w_t = jnp.swapaxes(w_in, 1, 2)
    w_kx = jnp.pad(w_t[layer:layer + 1, n_main * tn:, :], ((0, 0), (0, LANES - tail), (0, 0)))
    w_main = round_bf16(w_t, layer=layer, n_rows=n_main * tn)
    clip = lambda v, hi: jnp.minimum(jnp.maximum(v, 0), hi)

    def project(row0, n_rows, tm, prompt, shared, slabs):
        assert row0 % tm == 0 and n_rows % tm == 0 and (tm % tk == 0 or not prompt)
        r0 = row0 // tm
        x_rows, x_r0 = (x, r0) if x_parts is None else (x_parts[0 if prompt else 1], 0)
        bufs = list(shared or ()) + list(slabs or ())
        aliases = {}
        if shared:
            aliases.update({4 + n: n for n in range(3)})
        if slabs:
            aliases.update({4 + len(shared or ()) + n: 3 + n for n in range(3)})
        slab = lambda width: jax.ShapeDtypeStruct((n_layers, n_rows, width), F32)
        out_specs = [
            pl.BlockSpec((tn // ATTN_HEAD_DIM, tm, ATTN_HEAD_DIM), lambda i, j: (clip(j, n_q - 1), r0 + i, 0)),
            pl.BlockSpec((tn // IDX_HEAD_DIM, tm, IDX_HEAD_DIM),
                         lambda i, j: (clip(j - n_q - 2, n_qi - 1), r0 + i, 0)),
            pl.BlockSpec((tm, LANES), lambda i, j: (r0 + i, 0)),
            pl.BlockSpec((None, tm, tn), lambda i, j: (layer, i, 0)),
            pl.BlockSpec((None, tm, tn), lambda i, j: (layer, i, 0)),
            pl.BlockSpec((None, tm, IDX_HEAD_DIM), lambda i, j: (layer, i, 0)),
        ]
        out_shape = [
            jax.ShapeDtypeStruct((ATTN_HEADS, m, ATTN_HEAD_DIM), BF16),
            jax.ShapeDtypeStruct((IDX_HEADS, m, IDX_HEAD_DIM), BF16),
            jax.ShapeDtypeStruct((m, LANES), F32),
            slab(tn), slab(tn), slab(IDX_HEAD_DIM),
        ]
        if prompt:
            out_specs += [
                pl.BlockSpec((tm, tn), lambda i, j: (i, 0)),
                pl.BlockSpec((tm // tk, tn, tk), lambda i, j: (i, 0, 0)),
                pl.BlockSpec((tm, IDX_HEAD_DIM), lambda i, j: (i, 0)),
            ]
            out_shape += [
                jax.ShapeDtypeStruct((n_rows, tn), BF16),
                jax.ShapeDtypeStruct((n_rows // tk, tn, tk), BF16),
                jax.ShapeDtypeStruct((n_rows, IDX_HEAD_DIM), BF16),
            ]
        return pl.pallas_call(
            functools.partial(_attn_in_proj_kernel, n_q=n_q, n_qi=n_qi, prompt=prompt, tk=tk),
            grid=(n_rows // tm, n_main + 1),
            in_specs=[
                pl.BlockSpec((tm, d), lambda i, j: (x_r0 + i, 0), **_ONE_BUFFER),
                pl.BlockSpec((1, d), lambda i, j: (0, 0)),
                pl.BlockSpec((tn, d), lambda i, j: (jnp.minimum(j, n_main - 1), 0)),
                pl.BlockSpec((None, LANES, d), lambda i, j: (0, 0, 0)),
            ] + [pl.BlockSpec(memory_space=pl.ANY)] * len(bufs),
            out_specs=out_specs,
            out_shape=out_shape,
            input_output_aliases=aliases,
            scratch_shapes=[pltpu.VMEM((tm, d), BF16)],
            compiler_params=_params("parallel", "arbitrary"),
        )(x_rows, g.reshape(1, d), w_main, w_kx, *bufs)

    n_sample = m - n_prompt
    tm_p = max(t for t in (2 * tk, tk) if n_prompt % t == 0)
    tm_s = _row_tile(n_sample, cap=tk)
    prev_p, prev_s = (None, None) if new_kv is None else (new_kv[:3], new_kv[3:])
    q, qi, kx, kp, vp, kip, k16, vt, ki16 = project(0, n_prompt, tm_p, True, None, prev_p)
    q, qi, kx, ks, vs, kis = project(n_prompt, n_sample, tm_s, False, (q, qi, kx), prev_s)
    return q, qi, kx, (kp, vp, kip, ks, vs, kis), k16, vt, ki16


def _matmul_res_kernel(a_ref, w_ref, r_ref, *rest):
    o_ref = rest[-1]
    o_ref[...] = r_ref[...] + jnp.dot(a_ref[...], w_ref[...].astype(BF16), preferred_element_type=F32)


def matmul_residual(a, w, res, *, layer, tn=1024):
    m, k = a.shape
    n = w.shape[2]
    assert n % tn == 0

    def rows(res_rows, res_r0, row0, n_rows, tm, out_buf):
        assert row0 % tm == 0 and n_rows % tm == 0
        r0 = row0 // tm
        bufs = [] if out_buf is None else [out_buf]
        return pl.pallas_call(
            _matmul_res_kernel,
            grid=(n_rows // tm, n // tn),
            in_specs=[
                pl.BlockSpec((tm, k), lambda i, j: (r0 + i, 0)),
                pl.BlockSpec((None, k, tn), lambda i, j: (layer, 0, j)),
                pl.BlockSpec((tm, tn), lambda i, j: (res_r0 + i, j)),
            ] + [pl.BlockSpec(memory_space=pl.ANY)] * len(bufs),
            out_specs=pl.BlockSpec((tm, tn), lambda i, j: (r0 + i, j)),
            out_shape=jax.ShapeDtypeStruct((m, n), F32),
            input_output_aliases={3: 0} if bufs else {},
            compiler_params=_params("parallel", "parallel"),
        )(a, w, res_rows, *bufs)

    if not isinstance(res, tuple):
        return rows(res, 0, 0, m, _row_tile(m), None)
    n_first = res[0].shape[0]
    out = rows(res[0], 0, 0, n_first, _row_tile(n_first), None)
    tm_rest = _row_tile(m - n_first, cap=math.gcd(n_first, m - n_first))
    return rows(res[1], 0, n_first, m - n_first, tm_rest, out)


def _mlp_kernel(x_ref, g_ref, wu_ref, wd_ref, gf_ref, o_ref, *rest, final_norm, tail):
    h_ref = rest[-1]

    @pl.when(pl.program_id(1) == 0)
    def _():
        x = x_ref[...]
        h_ref[...] = _rms(x, g_ref[...]).astype(BF16)
        o_ref[...] = x

    u = jnp.maximum(jnp.dot(h_ref[...], wu_ref[...].astype(BF16), preferred_element_type=F32), 0.0)
    o_ref[...] += jnp.dot((u * u).astype(BF16), wd_ref[...].astype(BF16), preferred_element_type=F32)

    if final_norm:
        @pl.when(pl.program_id(1) == pl.num_programs(1) - 1)
        def _():
            o_ref[...] = _rms(o_ref[...], gf_ref[...])

    if tail is not None:
        tail_ref = rest[0]
        tail_tile, tail_row = tail

        @pl.when(jnp.logical_and(pl.program_id(0) == tail_tile, pl.program_id(1) == pl.num_programs(1) - 1))
        def _():
            tail_ref[...] = o_ref[tail_row:tail_row + tail_ref.shape[0]]


def mlp_residual(x, g, w_up, w_down, *, layer, g_final=None, split=None, tf=512):
    m, d = x.shape
    f = w_up.shape[2]
    tm = _row_tile(m)
    assert f % tf == 0
    final_norm = g_final is not None
    gf = (g_final if final_norm else g).reshape(1, d)
    out_spec = pl.BlockSpec((tm, d), lambda i, j: (i, 0), **_ONE_BUFFER)
    if split is None:
        tail, out_specs, out_shape = None, out_spec, jax.ShapeDtypeStruct((m, d), F32)
    else:
        tail = (split // tm, split % tm)
        assert m - split == tm - tail[1] and tail[1] % SUBLANES == 0
        out_specs = [out_spec, pl.BlockSpec((m - split, d), lambda i, j: (0, 0))]
        out_shape = [jax.ShapeDtypeStruct((split, d), F32), jax.ShapeDtypeStruct((m - split, d), F32)]
    return pl.pallas_call(
        functools.partial(_mlp_kernel, final_norm=final_norm, tail=tail),
        grid=(m // tm, f // tf),
        in_specs=[
            pl.BlockSpec((tm, d), lambda i, j: (i, 0), **_ONE_BUFFER),
            pl.BlockSpec((1, d), lambda i, j: (0, 0)),
            pl.BlockSpec((None, d, tf), lambda i, j: (layer, 0, j)),
            pl.BlockSpec((None, tf, d), lambda i, j: (layer, j, 0)),
            pl.BlockSpec((1, d), lambda i, j: (0, 0)),
        ],
        out_specs=out_specs,
        out_shape=out_shape,
        scratch_shapes=[pltpu.VMEM((tm, d), BF16)],
        compiler_params=_params("parallel", "arbitrary"),
    )(x, g.reshape(1, d), w_up, w_down, gf)


def _attn_kernel(q_ref, qi_ref, kx_ref, k_ref, vt_ref, ki_ref, *rest, tq, tk, q_pos0, k_top, n_kb_max):
    o_ref, key_ref, m_ref, l_ref, acc_ref, kn_ref, p_ref = rest[-7:]
    @pl.when(pl.program_id(1) == 0)
    def _():
        def kn_body(j, best):
            kf = k_ref[0, pl.ds(pl.multiple_of(j * tk, tk), tk), :].astype(F32)
            sq = kf * kf
            for g in range(ATTN_KV_HEADS):
                n2 = jnp.sum(sq[:, g * ATTN_HEAD_DIM:(g + 1) * ATTN_HEAD_DIM], axis=1, keepdims=True)
                best = jnp.maximum(best, jnp.max(n2, axis=0, keepdims=True))
            return best

        kn_ref[...] = jnp.broadcast_to(lax.fori_loop(0, n_kb_max, kn_body, jnp.zeros((1, 1), F32)), kn_ref.shape)

    q_start = q_pos0 + pl.program_id(1) * tq
    n_adm = ((q_start + tq - 1) // CHUNK + 1) * CHUNK
    n_kb = jnp.minimum((n_adm + tk - 1) // tk, n_kb_max)

    k_row = lax.broadcasted_iota(jnp.int32, (tk, tq), 0)
    k_lim = ((q_start + lax.broadcasted_iota(jnp.int32, (tk, tq), 1)) // CHUNK + 1) * CHUNK

    w_t = kx_ref[0].T[IDX_HEAD_DIM:IDX_HEAD_DIM + IDX_HEADS, :] * (IDX_HEADS ** -0.5 * IDX_HEAD_DIM ** -0.5)
    qi = qi_ref[0].reshape(IDX_HEADS * tq, IDX_HEAD_DIM)

    def score_body(j, carry):
        kib = ki_ref[0, pl.ds(pl.multiple_of(j * tk, tk), tk), :]
        d = lax.dot_general(kib, qi, _NT, preferred_element_type=F32)
        sc = jnp.zeros((tk, tq), F32)
        for h in range(IDX_HEADS):
            sc = sc + w_t[h:h + 1, :] * jnp.maximum(d[:, h * tq:(h + 1) * tq], 0.0)
        bits = pltpu.bitcast(sc, jnp.int32)
        key = bits ^ ((bits >> 31) & jnp.int32(0x7FFFFFFF))
        key_ref[j] = jnp.where(k_row < k_lim - j * tk, key, INT_MIN)
        return carry

    lax.fori_loop(0, n_kb, score_body, 0)

    rows_acc = 8 * SUBLANES

    def count_where(pred):
        def body(j, cnt):
            blk = key_ref[j]
            hit = pred(blk, j)
            for r in range(tk // rows_acc):
                cnt = jnp.where(hit[r * rows_acc:(r + 1) * rows_acc], cnt + 1.0, cnt)
            return cnt
        cnt = lax.fori_loop(0, n_kb, body, jnp.zeros((rows_acc, tq), F32))
        return jnp.sum(cnt, axis=0, keepdims=True)

    def bit_body(p, state):
        thr, cnt_thr = state
        cand = thr + jnp.left_shift(jnp.int32(1), 31 - p)
        cnt = count_where(lambda blk, j: blk >= cand)
        keep = cnt >= float(k_top)
        return jnp.where(keep, cand, thr), jnp.where(keep, cnt, cnt_thr)

    def search_bits(lo, hi, state):
        return lax.fori_loop(lo, hi, bit_body, state)

    def settled(state):
        return jnp.min(jnp.where(state[1] == float(k_top), 1.0, 0.0)) > 0.0

    state = search_bits(0, 24, (jnp.full((1, tq), INT_MIN, jnp.int32), jnp.full((1, tq), 2.0 ** 30, F32)))
    for lo in range(24, 32, 2):
        state = lax.cond(settled(state), lambda s: s, functools.partial(search_bits, lo, lo + 2), state)
    thr_raw, cnt_thr = state
    thr = jnp.maximum(thr_raw, INT_MIN + 1)

    tied = jnp.logical_and(cnt_thr > float(k_top), thr_raw > INT_MIN)
    n_tied = jnp.max(jnp.where(tied, 1.0, 0.0))

    @pl.when(n_tied > 0.0)
    def _():
        need = float(k_top) - count_where(lambda blk, j: blk > thr)
        n_bits = max(1, int(n_kb_max * tk - 1).bit_length())

        def pos_body(b, last):
            step = jnp.left_shift(jnp.int32(1), n_bits - 1 - b)
            cand = last + step - 1
            got = count_where(lambda blk, j: jnp.logical_and(blk == thr, k_row + j * tk <= cand))
            return jnp.where(got < need, last + step, last)

        last = lax.fori_loop(0, n_bits, pos_body, jnp.zeros((1, tq), jnp.int32))

        def demote_body(j, carry):
            blk = key_ref[j]
            drop = jnp.logical_and(blk == thr, k_row + j * tk > last)
            key_ref[j] = jnp.where(drop, thr - 1, blk)
            return carry

        lax.fori_loop(0, n_kb, demote_body, 0)

    m_ref[...] = jnp.full(m_ref.shape, NEG_BIG, F32)
    l_ref[...] = jnp.zeros(l_ref.shape, F32)
    acc_ref[...] = jnp.zeros(acc_ref.shape, F32)
    c_exp = ATTN_HEAD_DIM ** -0.5 * LOG2_E
    cols = ATTN_GROUP * tq

    def logits_t(j, g):
        k0 = pl.multiple_of(j * tk, tk)
        kb = k_ref[0, pl.ds(k0, tk), g * ATTN_HEAD_DIM:(g + 1) * ATTN_HEAD_DIM]
        vtb = vt_ref[0, j, g * ATTN_HEAD_DIM:(g + 1) * ATTN_HEAD_DIM, :]
        qg = q_ref[0, g * ATTN_GROUP:(g + 1) * ATTN_GROUP].reshape(cols, ATTN_HEAD_DIM)
        return lax.dot_general(kb, qg, _NT, preferred_element_type=F32), vtb

    qf = q_ref[0].reshape(ATTN_HEADS * tq, ATTN_HEAD_DIM).astype(F32)
    qn2 = lax.dot_general(jnp.ones((SUBLANES, ATTN_HEAD_DIM), BF16), (qf * qf).astype(BF16), _NT,
                          preferred_element_type=F32)
    qn2_max = qn2[0:1, 0:tq]
    for h in range(1, ATTN_HEADS):
        qn2_max = jnp.maximum(qn2_max, qn2[0:1, h * tq:(h + 1) * tq])
    m_bound = jnp.sqrt(qn2_max * kn_ref[...]) * 1.02
    fixed_shift_ok = jnp.max(m_bound) * (2.0 * c_exp) < 100.0

    @pl.when(fixed_shift_ok)
    def _():
        shift = -c_exp * m_bound

        def accumulate(j):
            for g in range(ATTN_KV_HEADS):
                vtb = vt_ref[0, j, g * ATTN_HEAD_DIM:(g + 1) * ATTN_HEAD_DIM, :]
                acc_ref[g] += jnp.dot(vtb, p_ref[g], preferred_element_type=F32)

        def probabilities(j):
            bias = jnp.where(key_ref[j] >= thr, shift, NEG_BIG)
            bias = jnp.concatenate([bias] * ATTN_GROUP, axis=1)
            for g in range(ATTN_KV_HEADS):
                s, _ = logits_t(j, g)
                p = jnp.exp2(s * c_exp + bias)
                l_ref[g] += jnp.sum(p, axis=0, keepdims=True)
                p_ref[g] = p.astype(BF16)

        probabilities(0)

        def body(j, carry):
            accumulate(j - 1)
            probabilities(j)
            return carry

        lax.fori_loop(1, n_kb, body, 0)
        accumulate(n_kb - 1)

    @pl.when(jnp.logical_not(fixed_shift_ok))
    def _():
        def body(j, carry):
            bias = jnp.where(key_ref[j] >= thr, 0.0, NEG_BIG)
            bias = jnp.concatenate([bias] * ATTN_GROUP, axis=1)
            for g in range(ATTN_KV_HEADS):
                s, vtb = logits_t(j, g)
                s = s + bias
                m_old = m_ref[g]
                m_new = jnp.maximum(m_old, jnp.max(s, axis=0, keepdims=True))
                alpha = jnp.exp2((m_old - m_new) * c_exp)
                p = jnp.exp2((s - m_new) * c_exp)
                l_ref[g] = alpha * l_ref[g] + jnp.sum(p, axis=0, keepdims=True)
                acc_ref[g] = alpha * acc_ref[g] + jnp.dot(vtb, p.astype(BF16), preferred_element_type=F32)
                m_ref[g] = m_new
            return carry

        lax.fori_loop(0, n_kb, body, 0)

    n_out = o_ref.shape[1]
    for g in range(ATTN_KV_HEADS):
        out_t = acc_ref[g] / l_ref[g]
        for hh in range(ATTN_GROUP):
            h = g * ATTN_GROUP + hh
            o_ref[0, :, h * ATTN_HEAD_DIM:(h + 1) * ATTN_HEAD_DIM] = (
                out_t[:, hh * tq:(hh + 1) * tq].T[:n_out].astype(o_ref.dtype))


def _rows_out_spec(width, n_steps, out_rows, into, dtype):
    b, rows = out_rows[0], out_rows[1]
    if into is None:
        spec = pl.BlockSpec((1, rows, width), lambda bi, i: (bi, i, 0))
        return spec, jax.ShapeDtypeStruct((b, out_rows[2], width), dtype), [], []
    buf, first_row, rows = into
    assert first_row % rows == 0 and buf.shape[0] == 1 and buf.dtype == dtype
    blk0 = first_row // rows
    spec = pl.BlockSpec((1, rows, width), lambda bi, i: (0, blk0 + bi * n_steps + i, 0))
    return spec, jax.ShapeDtypeStruct(buf.shape, dtype), [pl.BlockSpec(memory_space=pl.ANY)], [buf]


def dsa_attention(q, qi, kx, k, v, ki, *, n_q, tq, tk, q_pos0, k_top, v_is_transposed=False,
                  out_rows=None, into=None):
    b = q.shape[0]
    s = k.shape[1]
    assert n_q % tq == 0 and s % tk == 0 and tq == LANES
    n_kb_max = s // tk
    cols = ATTN_GROUP * tq
    vt = v if v_is_transposed else v.reshape(b, n_kb_max, tk, ATTN_KV_DIM).transpose(0, 1, 3, 2)
    assert vt.shape == (b, n_kb_max, ATTN_KV_DIM, tk)
    kern = functools.partial(_attn_kernel, tq=tq, tk=tk, q_pos0=q_pos0, k_top=k_top, n_kb_max=n_kb_max)
    resident = dict(pipeline_mode=pl.Buffered(1))
    out_spec, out_shape, buf_specs, bufs = _rows_out_spec(
        ATTN_Q_DIM, n_q // tq, (b, tq, out_rows or n_q), into, BF16)
    return pl.pallas_call(
        kern,
        grid=(b, n_q // tq),
        in_specs=[
            pl.BlockSpec((1, ATTN_HEADS, tq, ATTN_HEAD_DIM), lambda bi, i: (bi, 0, i, 0)),
            pl.BlockSpec((1, IDX_HEADS, tq, IDX_HEAD_DIM), lambda bi, i: (bi, 0, i, 0)),
            pl.BlockSpec((1, tq, LANES), lambda bi, i: (bi, i, 0)),
            pl.BlockSpec((1, s, ATTN_KV_DIM), lambda bi, i: (bi, 0, 0), **resident),
            pl.BlockSpec((1, n_kb_max, ATTN_KV_DIM, tk), lambda bi, i: (bi, 0, 0, 0), **resident),
            pl.BlockSpec((1, s, IDX_HEAD_DIM), lambda bi, i: (bi, 0, 0), **resident),
        ] + buf_specs,
        out_specs=out_spec,
        out_shape=out_shape,
        input_output_aliases={6: 0} if bufs else {},
        scratch_shapes=[
            pltpu.VMEM((n_kb_max, tk, tq), jnp.int32),
            pltpu.VMEM((ATTN_KV_HEADS, 1, cols), F32),
            pltpu.VMEM((ATTN_KV_HEADS, 1, cols), F32),
            pltpu.VMEM((ATTN_KV_HEADS, ATTN_HEAD_DIM, cols), F32),
            pltpu.VMEM((1, tq), F32),
            pltpu.VMEM((ATTN_KV_HEADS, tk, cols), BF16),
        ],
        compiler_params=_params("parallel", "arbitrary"),
    )(q, qi, kx, k, vt, ki, *bufs)


_LEVELS = (32, 16, 8, 4, 2, 1)


def _hgrn_tables():
    c = CHUNK
    t = np.arange(c)[:, None]
    u = np.arange(c)[None, :]
    mats = [(u <= t), (u > t)]
    masks = []
    for w in _LEVELS:
        r = (t // (2 * w)) * (2 * w) + w - 1
        upper = (t % (2 * w)) >= w
        mats.append((upper & (u > r) & (u <= t)) | ((~upper) & (u > t) & (u <= r)))
        s = np.arange(c)[None, :]
        masks.append(((t // (2 * w)) == (s // (2 * w))) & upper & ((s % (2 * w)) < w))
    masks.append(t == np.arange(c)[None, :])
    table = np.concatenate(mats, axis=0).astype(np.float32)
    return np.concatenate([table] * 3, axis=1), np.stack(masks).astype(np.float32)


def _hgrn_kernel(q_ref, f_ref, i_ref, gt_ref, lb_ref, gn_ref, tab_ref, msk_ref, s0_ref, *rest, layer):
    o_ref, sfin_ref, st_ref = rest[-3:]
    c = CHUNK
    dh = HGRN_HEAD_DIM
    step = pl.program_id(1)

    @pl.when(step == 0)
    def _():
        for h in range(HGRN_HEADS):
            st_ref[h] = s0_ref[0, h].T

    lb_all = lb_ref[...]
    e = jnp.exp(lb_all - jnp.max(lb_all, axis=0, keepdims=True))
    p_lb = e / jnp.sum(e, axis=0, keepdims=True)
    lower = jnp.sum(p_lb[:layer + 1], axis=0, keepdims=True) - p_lb[0:1]

    def chunk_operands(ci):
        rows = slice(ci * c, (ci + 1) * c)
        f = lower + (1.0 - lower) * jax.nn.sigmoid(f_ref[0, rows])
        kk = 1.0 - f
        lf = jnp.log2(f)
        lf_hi = lf.astype(BF16)
        r1 = lf - lf_hi.astype(F32)
        lf_mid = r1.astype(BF16)
        lf_lo = (r1 - lf_mid.astype(F32)).astype(BF16)
        lf3 = jnp.concatenate([lf_hi, lf_mid, lf_lo], axis=0)
        ex = jnp.exp2(jnp.dot(tab_ref[...], lf3, preferred_element_type=F32))
        qq = q_ref[0, rows]
        vv = i_ref[0, rows]
        ops = dict(q16=qq.astype(BF16), k16=kk.astype(BF16), v16=vv.astype(BF16), vv=vv)
        ops["q_lv"] = [(qq * ex[(2 + li) * c:(3 + li) * c]).astype(BF16) for li in range(len(_LEVELS))]
        ops["k_lv"] = [(kk * ex[(2 + li) * c:(3 + li) * c]).astype(BF16) for li in range(len(_LEVELS))]
        ops["q_in"] = (qq * ex[0:c]).astype(BF16)
        ops["k_out"] = (kk * ex[c:2 * c]).astype(BF16)
        ops["decay"] = ex[c - 1:c]
        ops["gate"] = jax.nn.sigmoid(gt_ref[0, rows])
        return ops

    def chunk_outputs(ci, ops):
        rows = slice(ci * c, (ci + 1) * c)
        scores = []
        for h in range(HGRN_HEADS):
            sl = slice(h * dh, (h + 1) * dh)
            s_h = msk_ref[len(_LEVELS)] * lax.dot_general(
                ops["q16"][:, sl], ops["k16"][:, sl], _NT, preferred_element_type=F32)
            for li in range(len(_LEVELS)):
                s_h = s_h + msk_ref[li] * lax.dot_general(
                    ops["q_lv"][li][:, sl], ops["k_lv"][li][:, sl], _NT, preferred_element_type=F32)
            scores.append(s_h.astype(BF16))
        for h in range(HGRN_HEADS):
            sl = slice(h * dh, (h + 1) * dh)
            st = st_ref[h]
            o_h = jnp.dot(scores[h], ops["v16"][:, sl], preferred_element_type=F32)
            o_h = o_h + lax.dot_general(ops["q_in"][:, sl], st.astype(BF16), _NT, preferred_element_type=F32)
            st_ref[h] = st * ops["decay"][:, sl] + jnp.dot(
                ops["vv"][:, sl].T.astype(BF16), ops["k_out"][:, sl], preferred_element_type=F32)
            o_ref[0, rows, sl] = (_rms(o_h, gn_ref[:, sl]) * ops["gate"][:, sl]).astype(o_ref.dtype)

    n_sub = q_ref.shape[1] // c
    ops = chunk_operands(0)
    for ci in range(n_sub):
        ops_next = chunk_operands(ci + 1) if ci + 1 < n_sub else None
        chunk_outputs(ci, ops)
        ops = ops_next

    @pl.when(step == pl.num_programs(1) - 1)
    def _():
        for h in range(HGRN_HEADS):
            sfin_ref[0, h] = st_ref[h].T


def hgrn_scan(p, lb, gnorm, s0, *, layer, n_rows=None, out_rows=None, into=None, states=None):
    b = p.shape[0]
    l = p.shape[1] if n_rows is None else n_rows
    d = D_MODEL
    c = max(r for r in (4 * CHUNK, 2 * CHUNK, CHUNK) if l % r == 0)
    tab, msk = _hgrn_tables()
    kern = functools.partial(_hgrn_kernel, layer=layer)
    col = lambda j: pl.BlockSpec((1, c, d), lambda bi, t, j=j: (bi, t, j))
    state_spec = pl.BlockSpec((1, HGRN_HEADS, HGRN_HEAD_DIM, HGRN_HEAD_DIM), lambda bi, t: (bi, 0, 0, 0))
    out_spec, out_shape, buf_specs, bufs = _rows_out_spec(d, l // c, (b, c, out_rows or l), into, BF16)
    aliases = {9: 0} if bufs else {}
    states_spec = pl.BlockSpec((None,) + state_spec.block_shape, lambda bi, t: (layer, bi, 0, 0, 0))
    if states is not None:
        aliases[9 + len(bufs)] = 1
        buf_specs = buf_specs + [pl.BlockSpec(memory_space=pl.ANY)]
        bufs = bufs + [states]
    return pl.pallas_call(
        kern,
        grid=(b, l // c),
        in_specs=[
            col(0), col(1), col(2), col(3),
            pl.BlockSpec(lb.shape, lambda bi, t: (0, 0)),
            pl.BlockSpec((1, d), lambda bi, t: (0, 0)),
            pl.BlockSpec(tab.shape, lambda bi, t: (0, 0)),
            pl.BlockSpec(msk.shape, lambda bi, t: (0, 0, 0)),
            state_spec,
        ] + buf_specs,
        out_specs=[out_spec, states_spec],
        out_shape=[out_shape, jax.ShapeDtypeStruct((lb.shape[0],) + s0.shape, F32)],
        input_output_aliases=aliases,
        scratch_shapes=[pltpu.VMEM((HGRN_HEADS, HGRN_HEAD_DIM, HGRN_HEAD_DIM), F32)],
        compiler_params=_params("parallel", "arbitrary"),
    )(p, p, p, p, lb, gnorm.reshape(1, d), jnp.asarray(tab, BF16), jnp.asarray(msk), s0, *bufs)


def _attn_layer(x, g, w_in, w_out, layer, new_kv, cache_k, cache_v, cache_kidx, n_p, b_s, t_s):
    q, qi, kx, new_kv, k16, vt, ki16 = attn_in_proj(x, g, w_in, layer=layer, n_prompt=n_p, new_kv=new_kv)
    k_new, v_new, ki_new = new_kv[3][layer], new_kv[4][layer], new_kv[5][layer]
    past = cache_k.shape[1]
    tq = LANES

    o_p = dsa_attention(
        q[None], qi[None], kx[None], k16[None], vt[None], ki16[None], v_is_transposed=True,
        n_q=n_p, tq=tq, tk=vt.shape[2], q_pos0=0, k_top=min(TOPK_MAX, n_p // 4),
        out_rows=n_p + b_s * t_s)

    s_all = past + t_s
    n_adm_pad = ((past + tq - 1) // CHUNK + 1) * CHUNK
    s_pad = -(-max(s_all, n_adm_pad) // LANES) * LANES
    tk_s = s_pad

    def with_cache(cache, new, width):
        full = jnp.zeros((b_s, s_pad, width), BF16)
        full = lax.dynamic_update_slice(full, cache.reshape(b_s, past, width).astype(BF16), (0, 0, 0))
        return lax.dynamic_update_slice(full, new.astype(BF16).reshape(b_s, t_s, width), (0, past, 0))

    def sample_heads(a, heads, dim):
        a = a[:, n_p:].reshape(heads, b_s, t_s, dim).transpose(1, 0, 2, 3)
        return jnp.pad(a, ((0, 0), (0, 0), (0, tq - t_s), (0, 0)))

    kx_s = jnp.pad(kx[n_p:].reshape(b_s, t_s, LANES), ((0, 0), (0, tq - t_s), (0, 0)))
    o_s = dsa_attention(
        sample_heads(q, ATTN_HEADS, ATTN_HEAD_DIM), sample_heads(qi, IDX_HEADS, IDX_HEAD_DIM), kx_s,
        with_cache(cache_k, k_new, ATTN_KV_DIM), with_cache(cache_v, v_new, ATTN_KV_DIM),
        with_cache(cache_kidx, ki_new, IDX_HEAD_DIM),
        n_q=tq, tq=tq, tk=tk_s, q_pos0=past, k_top=min(TOPK_MAX, s_all // 4),
        into=(o_p, n_p, t_s))

    x = matmul_residual(o_s[0], w_out, x, layer=layer)
    return x, new_kv


def _rec_layer(x, g, w_in, w_out, gnorm, rec_lb, state, layer, new_states, n_p, b_s, t_s):
    p = norm_matmul(x, g, w_in, layer=layer)
    s0_p = jnp.zeros((1,) + state.shape[1:], F32)
    o_p, s_p = hgrn_scan(p[None], rec_lb, gnorm, s0_p, layer=layer, n_rows=n_p, out_rows=x.shape[0],
                         states=new_states[0])
    o_s, s_s = hgrn_scan(p[n_p:].reshape(b_s, t_s, -1), rec_lb, gnorm, state, layer=layer,
                         into=(o_p, n_p, t_s),
                         states=new_states[1])
    x = matmul_residual(o_s[0], w_out, x, layer=layer)
    return x, (s_p, s_s)


def kernel(x_prompt, x_sample, cache_k, cache_v, cache_kidx, state_s, norm_mix, norm_mlp, norm_final,
           attn_w_in, attn_w_out, rec_w_in, rec_w_out, rec_gnorm, rec_lb, mlp_w_up, mlp_w_down):
    b_p, l_p, d = x_prompt.shape
    b_s, t_s, _ = x_sample.shape
    assert b_p == 1
    n_p = b_p * l_p
    x = (x_prompt.reshape(n_p, d), x_sample.reshape(b_s * t_s, d))

    new_states = (None, None)
    new_kv = None
    for layer in range(DEPTH):
        j = layer // N_MIXERS
        if layer % N_MIXERS == 0:
            x, new_kv = _attn_layer(
                x, norm_mix[layer], attn_w_in, attn_w_out, j, new_kv,
                cache_k[j], cache_v[j], cache_kidx[j], n_p, b_s, t_s)
        else:
            x, new_states = _rec_layer(
                x, norm_mix[layer], rec_w_in, rec_w_out, rec_gnorm[j], rec_lb, state_s[j],
                j, new_states, n_p, b_s, t_s)
        last = layer == DEPTH - 1
        x = mlp_residual(x, norm_mlp[layer], mlp_w_up, mlp_w_down, layer=layer,
                         g_final=norm_final if last else None, split=n_p if last else None)

    y_p, y_s = x
    kp, vp, kip, ks, vs, kis = new_kv
    n_l = kp.shape[0]
    heads = (ATTN_KV_HEADS, ATTN_HEAD_DIM)
    return (y_p.reshape(b_p, l_p, d), y_s.reshape(b_s, t_s, d),
            kp.reshape(n_l, b_p, l_p, *heads), vp.reshape(n_l, b_p, l_p, *heads),
            kip.reshape(n_l, b_p, l_p, IDX_HEAD_DIM), new_states[0],
            ks.reshape(n_l, b_s, t_s, *heads), vs.reshape(n_l, b_s, t_s, *heads),
            kis.reshape(n_l, b_s, t_s, IDX_HEAD_DIM), new_states[1])
```

```python
import functools
import math

import numpy as np
import jax
import jax.numpy as jnp
from jax import lax
from jax.experimental import pallas as pl
from jax.experimental.pallas import tpu as pltpu

D_MODEL = 2048
DEPTH = 4
CHUNK = 64
N_MIXERS = 2
ATTN_HEADS = 16
ATTN_KV_HEADS = 4
ATTN_HEAD_DIM = D_MODEL // ATTN_HEADS
ATTN_GROUP = ATTN_HEADS // ATTN_KV_HEADS
IDX_HEADS = 16
IDX_HEAD_DIM = 64
TOPK_MAX = 256
HGRN_HEAD_DIM = 128
HGRN_HEADS = D_MODEL // HGRN_HEAD_DIM
D_FF = 4 * D_MODEL
RMS_EPS = 1e-6
ATTN_Q_DIM = ATTN_HEADS * ATTN_HEAD_DIM
ATTN_KV_DIM = ATTN_KV_HEADS * ATTN_HEAD_DIM
IDX_Q_DIM = IDX_HEADS * IDX_HEAD_DIM

LANES = 128
SUBLANES = 8
VMEM_LIMIT = 56 * 1024 * 1024

F32 = jnp.float32
BF16 = jnp.bfloat16
INT_MIN = np.int32(-(2 ** 31))
NEG_BIG = -1e30
LOG2_E = 1.4426950408889634

_NT = (((1,), (1,)), ((), ()))


def _params(*sem):
    return pltpu.CompilerParams(dimension_semantics=sem, vmem_limit_bytes=VMEM_LIMIT)


def _rms(x, g):
    ms = jnp.mean(x * x, axis=-1, keepdims=True)
    return (x * lax.rsqrt(ms + RMS_EPS)) * g


def _norm_matmul_kernel(x_ref, g_ref, w_ref, o_ref, h_ref):
    @pl.when(pl.program_id(1) == 0)
    def _():
        h_ref[...] = _rms(x_ref[...], g_ref[...]).astype(BF16)

    o_ref[...] = jnp.dot(h_ref[...], w_ref[...].astype(BF16), preferred_element_type=F32)


def _row_tile(m, cap=1088):
    return max(t for t in range(16, cap + 1, 16) if m % t == 0)


_ONE_BUFFER = dict(pipeline_mode=pl.Buffered(1))


def norm_matmul(x, g, w, *, layer, tn=1024):
    m, d = x.shape
    n = w.shape[2]
    tm = _row_tile(m)
    assert n % tn == 0
    return pl.pallas_call(
        _norm_matmul_kernel,
        grid=(m // tm, n // tn),
        in_specs=[
            pl.BlockSpec((tm, d), lambda i, j: (i, 0), **_ONE_BUFFER),
            pl.BlockSpec((1, d), lambda i, j: (0, 0)),
            pl.BlockSpec((None, d, tn), lambda i, j: (layer, 0, j)),
        ],
        out_specs=pl.BlockSpec((tm, tn), lambda i, j: (i, j)),
        out_shape=jax.ShapeDtypeStruct((m, n), F32),
        scratch_shapes=[pltpu.VMEM((tm, d), BF16)],
        compiler_params=_params("parallel", "arbitrary"),
    )(x, g.reshape(1, d), w)


def _round_bf16_kernel(w_ref, o_ref):
    o_ref[...] = w_ref[...].astype(BF16)


def round_bf16(w_t, *, layer, n_rows, tn=512):
    d = w_t.shape[2]
    assert n_rows % tn == 0
    return pl.pallas_call(
        _round_bf16_kernel,
        grid=(n_rows // tn,),
        in_specs=[pl.BlockSpec((None, tn, d), lambda j: (layer, j, 0))],
        out_specs=pl.BlockSpec((tn, d), lambda j: (j, 0)),
        out_shape=jax.ShapeDtypeStruct((n_rows, d), BF16),
        compiler_params=_params("parallel"),
    )(w_t)


def _attn_in_proj_kernel(x_ref, g_ref, w_ref, wkx_ref, *rest, n_q, n_qi, prompt, tk):
    n_out = 9 if prompt else 6
    outs, h_ref = rest[-(n_out + 1):-1], rest[-1]
    q_ref, qi_ref, kx_ref, k_ref, v_ref, ki_ref = outs[:6]
    j = pl.program_id(1)

    @pl.when(j == 0)
    def _():
        h_ref[...] = _rms(x_ref[...], g_ref[...]).astype(BF16)

    def tile():
        return lax.dot_general(h_ref[...], w_ref[...], _NT, preferred_element_type=F32)

    @pl.when(j < n_q)
    def _():
        res = tile()
        for hh in range(q_ref.shape[0]):
            q_ref[hh] = res[:, hh * ATTN_HEAD_DIM:(hh + 1) * ATTN_HEAD_DIM].astype(q_ref.dtype)

    @pl.when(j == n_q)
    def _():
        res = tile()
        k_ref[...] = res
        if prompt:
            outs[6][...] = res.astype(BF16)

    @pl.when(j == n_q + 1)
    def _():
        res = tile()
        v_ref[...] = res
        if prompt:
            for t in range(outs[7].shape[0]):
                outs[7][t] = res[t * tk:(t + 1) * tk].T.astype(BF16)

    @pl.when(jnp.logical_and(j >= n_q + 2, j < n_q + 2 + n_qi))
    def _():
        res = tile()
        for hh in range(qi_ref.shape[0]):
            qi_ref[hh] = res[:, hh * IDX_HEAD_DIM:(hh + 1) * IDX_HEAD_DIM].astype(qi_ref.dtype)

    @pl.when(j == n_q + 2 + n_qi)
    def _():
        res = lax.dot_general(h_ref[...], wkx_ref[...].astype(BF16), _NT, preferred_element_type=F32)
        kx_ref[...] = res
        ki_ref[...] = res[:, :IDX_HEAD_DIM]
        if prompt:
            outs[8][...] = res[:, :IDX_HEAD_DIM].astype(BF16)


def attn_in_proj(x, g, w_in, *, layer, n_prompt, new_kv=None):
    x_parts = x if isinstance(x, tuple) else None
    m = sum(part.shape[0] for part in x_parts) if x_parts else x.shape[0]
    d = w_in.shape[1]
    tn = ATTN_KV_DIM
    tk = 2 * ATTN_KV_DIM
    n_layers = w_in.shape[0]
    n_q, n_qi = ATTN_Q_DIM // tn, IDX_Q_DIM // tn
    n_main = n_q + 2 + n_qi
    tail = w_in.shape[2] - n_main * tn
    w_t = jnp.swapaxes(w_in, 1, 2)
    w_kx = jnp.pad(w_t[layer:layer + 1, n_main * tn:, :], ((0, 0), (0, LANES - tail), (0, 0)))
    w_main = round_bf16(w_t, layer=layer, n_rows=n_main * tn)
    clip = lambda v, hi: jnp.minimum(jnp.maximum(v, 0), hi)

    def project(row0, n_rows, tm, prompt, shared, slabs):
        assert row0 % tm == 0 and n_rows % tm == 0 and (tm % tk == 0 or not prompt)
        r0 = row0 // tm
        x_rows, x_r0 = (x, r0) if x_parts is None else (x_parts[0 if prompt else 1], 0)
        bufs = list(shared or ()) + list(slabs or ())
        aliases = {}
        if shared:
            aliases.update({4 + n: n for n in range(3)})
        if slabs:
            aliases.update({4 + len(shared or ()) + n: 3 + n for n in range(3)})
        slab = lambda width: jax.ShapeDtypeStruct((n_layers, n_rows, width), F32)
        out_specs = [
            pl.BlockSpec((tn // ATTN_HEAD_DIM, tm, ATTN_HEAD_DIM), lambda i, j: (clip(j, n_q - 1), r0 + i, 0)),
            pl.BlockSpec((tn // IDX_HEAD_DIM, tm, IDX_HEAD_DIM),
                         lambda i, j: (clip(j - n_q - 2, n_qi - 1), r0 + i, 0)),
            pl.BlockSpec((tm, LANES), lambda i, j: (r0 + i, 0)),
            pl.BlockSpec((None, tm, tn), lambda i, j: (layer, i, 0)),
            pl.BlockSpec((None, tm, tn), lambda i, j: (layer, i, 0)),
            pl.BlockSpec((None, tm, IDX_HEAD_DIM), lambda i, j: (layer, i, 0)),
        ]
        out_shape = [
            jax.ShapeDtypeStruct((ATTN_HEADS, m, ATTN_HEAD_DIM), BF16),
            jax.ShapeDtypeStruct((IDX_HEADS, m, IDX_HEAD_DIM), BF16),
            jax.ShapeDtypeStruct((m, LANES), F32),
            slab(tn), slab(tn), slab(IDX_HEAD_DIM),
        ]
        if prompt:
            out_specs += [
                pl.BlockSpec((tm, tn), lambda i, j: (i, 0)),
                pl.BlockSpec((tm // tk, tn, tk), lambda i, j: (i, 0, 0)),
                pl.BlockSpec((tm, IDX_HEAD_DIM), lambda i, j: (i, 0)),
            ]
            out_shape += [
                jax.ShapeDtypeStruct((n_rows, tn), BF16),
                jax.ShapeDtypeStruct((n_rows // tk, tn, tk), BF16),
                jax.ShapeDtypeStruct((n_rows, IDX_HEAD_DIM), BF16),
            ]
        return pl.pallas_call(
            functools.partial(_attn_in_proj_kernel, n_q=n_q, n_qi=n_qi, prompt=prompt, tk=tk),
            grid=(n_rows // tm, n_main + 1),
            in_specs=[
                pl.BlockSpec((tm, d), lambda i, j: (x_r0 + i, 0), **_ONE_BUFFER),
                pl.BlockSpec((1, d), lambda i, j: (0, 0)),
                pl.BlockSpec((tn, d), lambda i, j: (jnp.minimum(j, n_main - 1), 0)),
                pl.BlockSpec((None, LANES, d), lambda i, j: (0, 0, 0)),
            ] + [pl.BlockSpec(memory_space=pl.ANY)] * len(bufs),
            out_specs=out_specs,
            out_shape=out_shape,
            input_output_aliases=aliases,
            scratch_shapes=[pltpu.VMEM((tm, d), BF16)],
            compiler_params=_params("parallel", "arbitrary"),
        )(x_rows, g.reshape(1, d), w_main, w_kx, *bufs)

    n_sample = m - n_prompt
    tm_p = tk
    tm_s = _row_tile(n_sample, cap=tn)
    prev_p, prev_s = (None, None) if new_kv is None else (new_kv[:3], new_kv[3:])
    q, qi, kx, kp, vp, kip, k16, vt, ki16 = project(0, n_prompt, tm_p, True, None, prev_p)
    q, qi, kx, ks, vs, kis = project(n_prompt, n_sample, tm_s, False, (q, qi, kx), prev_s)
    return q, qi, kx, (kp, vp, kip, ks, vs, kis), k16, vt, ki16


def _matmul_res_kernel(a_ref, w_ref, r_ref, *rest):
    o_ref = rest[-1]
    o_ref[...] = r_ref[...] + jnp.dot(a_ref[...], w_ref[...].astype(BF16), preferred_element_type=F32)


def matmul_residual(a, w, res, *, layer, tn=1024):
    m, k = a.shape
    n = w.shape[2]
    assert n % tn == 0

    def rows(res_rows, res_r0, row0, n_rows, tm, out_buf):
        assert row0 % tm == 0 and n_rows % tm == 0
        r0 = row0 // tm
        bufs = [] if out_buf is None else [out_buf]
        return pl.pallas_call(
            _matmul_res_kernel,
            grid=(n_rows // tm, n // tn),
            in_specs=[
                pl.BlockSpec((tm, k), lambda i, j: (r0 + i, 0)),
                pl.BlockSpec((None, k, tn), lambda i, j: (layer, 0, j)),
                pl.BlockSpec((tm, tn), lambda i, j: (res_r0 + i, j)),
            ] + [pl.BlockSpec(memory_space=pl.ANY)] * len(bufs),
            out_specs=pl.BlockSpec((tm, tn), lambda i, j: (r0 + i, j)),
            out_shape=jax.ShapeDtypeStruct((m, n), F32),
            input_output_aliases={3: 0} if bufs else {},
            compiler_params=_params("parallel", "parallel"),
        )(a, w, res_rows, *bufs)

    if not isinstance(res, tuple):
        return rows(res, 0, 0, m, _row_tile(m), None)
    n_first = res[0].shape[0]
    out = rows(res[0], 0, 0, n_first, _row_tile(n_first), None)
    tm_rest = _row_tile(m - n_first, cap=math.gcd(n_first, m - n_first))
    return rows(res[1], 0, n_first, m - n_first, tm_rest, out)


def _mlp_kernel(x_ref, g_ref, wu_ref, wd_ref, gf_ref, o_ref, *rest, final_norm, tail):
    h_ref = rest[-1]

    @pl.when(pl.program_id(1) == 0)
    def _():
        x = x_ref[...]
        h_ref[...] = _rms(x, g_ref[...]).astype(BF16)
        o_ref[...] = x

    u = jnp.maximum(jnp.dot(h_ref[...], wu_ref[...].astype(BF16), preferred_element_type=F32), 0.0)
    o_ref[...] += jnp.dot((u * u).astype(BF16), wd_ref[...].astype(BF16), preferred_element_type=F32)

    if final_norm:
        @pl.when(pl.program_id(1) == pl.num_programs(1) - 1)
        def _():
            o_ref[...] = _rms(o_ref[...], gf_ref[...])

    if tail is not None:
        tail_ref = rest[0]
        tail_tile, tail_row = tail

        @pl.when(jnp.logical_and(pl.program_id(0) == tail_tile, pl.program_id(1) == pl.num_programs(1) - 1))
        def _():
            tail_ref[...] = o_ref[tail_row:tail_row + tail_ref.shape[0]]


def mlp_residual(x, g, w_up, w_down, *, layer, g_final=None, split=None, tf=512):
    m, d = x.shape
    f = w_up.shape[2]
    tm = _row_tile(m)
    assert f % tf == 0
    final_norm = g_final is not None
    gf = (g_final if final_norm else g).reshape(1, d)
    out_spec = pl.BlockSpec((tm, d), lambda i, j: (i, 0), **_ONE_BUFFER)
    if split is None:
        tail, out_specs, out_shape = None, out_spec, jax.ShapeDtypeStruct((m, d), F32)
    else:
        tail = (split // tm, split % tm)
        assert m - split == tm - tail[1] and tail[1] % SUBLANES == 0
        out_specs = [out_spec, pl.BlockSpec((m - split, d), lambda i, j: (0, 0))]
        out_shape = [jax.ShapeDtypeStruct((split, d), F32), jax.ShapeDtypeStruct((m - split, d), F32)]
    return pl.pallas_call(
        functools.partial(_mlp_kernel, final_norm=final_norm, tail=tail),
        grid=(m // tm, f // tf),
        in_specs=[
            pl.BlockSpec((tm, d), lambda i, j: (i, 0), **_ONE_BUFFER),
            pl.BlockSpec((1, d), lambda i, j: (0, 0)),
            pl.BlockSpec((None, d, tf), lambda i, j: (layer, 0, j)),
            pl.BlockSpec((None, tf, d), lambda i, j: (layer, j, 0)),
            pl.BlockSpec((1, d), lambda i, j: (0, 0)),
        ],
        out_specs=out_specs,
        out_shape=out_shape,
        scratch_shapes=[pltpu.VMEM((tm, d), BF16)],
        compiler_params=_params("parallel", "arbitrary"),
    )(x, g.reshape(1, d), w_up, w_down, gf)


def _attn_kernel(q_ref, qi_ref, kx_ref, k_ref, vt_ref, ki_ref, *rest, tq, tk, q_pos0, k_top, n_kb_max):
    o_ref, key_ref, m_ref, l_ref, acc_ref, kn_ref, p_ref = rest[-7:]
    @pl.when(pl.program_id(1) == 0)
    def _():
        def kn_body(j, best):
            kf = k_ref[0, pl.ds(pl.multiple_of(j * tk, tk), tk), :].astype(F32)
            sq = kf * kf
            for g in range(ATTN_KV_HEADS):
                n2 = jnp.sum(sq[:, g * ATTN_HEAD_DIM:(g + 1) * ATTN_HEAD_DIM], axis=1, keepdims=True)
                best = jnp.maximum(best, jnp.max(n2, axis=0, keepdims=True))
            return best

        kn_ref[...] = jnp.broadcast_to(lax.fori_loop(0, n_kb_max, kn_body, jnp.zeros((1, 1), F32)), kn_ref.shape)

    q_start = q_pos0 + pl.program_id(1) * tq
    n_adm = ((q_start + tq - 1) // CHUNK + 1) * CHUNK
    n_kb = jnp.minimum((n_adm + tk - 1) // tk, n_kb_max)

    k_row = lax.broadcasted_iota(jnp.int32, (tk, tq), 0)
    k_lim = ((q_start + lax.broadcasted_iota(jnp.int32, (tk, tq), 1)) // CHUNK + 1) * CHUNK

    w_t = kx_ref[0].T[IDX_HEAD_DIM:IDX_HEAD_DIM + IDX_HEADS, :] * (IDX_HEADS ** -0.5 * IDX_HEAD_DIM ** -0.5)
    qi = qi_ref[0].reshape(IDX_HEADS * tq, IDX_HEAD_DIM)

    def score_body(j, carry):
        kib = ki_ref[0, pl.ds(pl.multiple_of(j * tk, tk), tk), :]
        d = lax.dot_general(kib, qi, _NT, preferred_element_type=F32)
        sc = jnp.zeros((tk, tq), F32)
        for h in range(IDX_HEADS):
            sc = sc + w_t[h:h + 1, :] * jnp.maximum(d[:, h * tq:(h + 1) * tq], 0.0)
        bits = pltpu.bitcast(sc, jnp.int32)
        key = bits ^ ((bits >> 31) & jnp.int32(0x7FFFFFFF))
        key_ref[j] = jnp.where(k_row < k_lim - j * tk, key, INT_MIN)
        return carry

    lax.fori_loop(0, n_kb, score_body, 0)

    rows_acc = 8 * SUBLANES

    def count_where(pred):
        def body(j, cnt):
            blk = key_ref[j]
            hit = pred(blk, j)
            for r in range(tk // rows_acc):
                cnt = jnp.where(hit[r * rows_acc:(r + 1) * rows_acc], cnt + 1.0, cnt)
            return cnt
        cnt = lax.fori_loop(0, n_kb, body, jnp.zeros((rows_acc, tq), F32))
        return jnp.sum(cnt, axis=0, keepdims=True)

    def bit_body(p, state):
        thr, cnt_thr = state
        cand = thr + jnp.left_shift(jnp.int32(1), 31 - p)
        cnt = count_where(lambda blk, j: blk >= cand)
        keep = cnt >= float(k_top)
        return jnp.where(keep, cand, thr), jnp.where(keep, cnt, cnt_thr)

    def search_bits(lo, hi, state):
        return lax.fori_loop(lo, hi, bit_body, state)

    def settled(state):
        return jnp.min(jnp.where(state[1] == float(k_top), 1.0, 0.0)) > 0.0

    state = search_bits(0, 24, (jnp.full((1, tq), INT_MIN, jnp.int32), jnp.full((1, tq), 2.0 ** 30, F32)))
    for lo in range(24, 32, 2):
        state = lax.cond(settled(state), lambda s: s, functools.partial(search_bits, lo, lo + 2), state)
    thr_raw, cnt_thr = state
    thr = jnp.maximum(thr_raw, INT_MIN + 1)

    tied = jnp.logical_and(cnt_thr > float(k_top), thr_raw > INT_MIN)
    n_tied = jnp.max(jnp.where(tied, 1.0, 0.0))

    @pl.when(n_tied > 0.0)
    def _():
        need = float(k_top) - count_where(lambda blk, j: blk > thr)
        n_bits = max(1, int(n_kb_max * tk - 1).bit_length())

        def pos_body(b, last):
            step = jnp.left_shift(jnp.int32(1), n_bits - 1 - b)
            cand = last + step - 1
            got = count_where(lambda blk, j: jnp.logical_and(blk == thr, k_row + j * tk <= cand))
            return jnp.where(got < need, last + step, last)

        last = lax.fori_loop(0, n_bits, pos_body, jnp.zeros((1, tq), jnp.int32))

        def demote_body(j, carry):
            blk = key_ref[j]
            drop = jnp.logical_and(blk == thr, k_row + j * tk > last)
            key_ref[j] = jnp.where(drop, thr - 1, blk)
            return carry

        lax.fori_loop(0, n_kb, demote_body, 0)

    m_ref[...] = jnp.full(m_ref.shape, NEG_BIG, F32)
    l_ref[...] = jnp.zeros(l_ref.shape, F32)
    acc_ref[...] = jnp.zeros(acc_ref.shape, F32)
    c_exp = ATTN_HEAD_DIM ** -0.5 * LOG2_E
    cols = ATTN_GROUP * tq

    def logits_t(j, g):
        k0 = pl.multiple_of(j * tk, tk)
        kb = k_ref[0, pl.ds(k0, tk), g * ATTN_HEAD_DIM:(g + 1) * ATTN_HEAD_DIM]
        vtb = vt_ref[0, j, g * ATTN_HEAD_DIM:(g + 1) * ATTN_HEAD_DIM, :]
        qg = q_ref[0, g * ATTN_GROUP:(g + 1) * ATTN_GROUP].reshape(cols, ATTN_HEAD_DIM)
        return lax.dot_general(kb, qg, _NT, preferred_element_type=F32), vtb

    qf = q_ref[0].reshape(ATTN_HEADS * tq, ATTN_HEAD_DIM).astype(F32)
    qn2 = lax.dot_general(jnp.ones((SUBLANES, ATTN_HEAD_DIM), BF16), (qf * qf).astype(BF16), _NT,
                          preferred_element_type=F32)
    qn2_max = qn2[0:1, 0:tq]
    for h in range(1, ATTN_HEADS):
        qn2_max = jnp.maximum(qn2_max, qn2[0:1, h * tq:(h + 1) * tq])
    m_bound = jnp.sqrt(qn2_max * kn_ref[...]) * 1.02
    fixed_shift_ok = jnp.max(m_bound) * (2.0 * c_exp) < 100.0

    @pl.when(fixed_shift_ok)
    def _():
        shift = -c_exp * m_bound

        def accumulate(j):
            for g in range(ATTN_KV_HEADS):
                vtb = vt_ref[0, j, g * ATTN_HEAD_DIM:(g + 1) * ATTN_HEAD_DIM, :]
                acc_ref[g] += jnp.dot(vtb, p_ref[g], preferred_element_type=F32)

        def probabilities(j):
            bias = jnp.where(key_ref[j] >= thr, shift, NEG_BIG)
            bias = jnp.concatenate([bias] * ATTN_GROUP, axis=1)
            for g in range(ATTN_KV_HEADS):
                s, _ = logits_t(j, g)
                p = jnp.exp2(s * c_exp + bias)
                l_ref[g] += jnp.sum(p, axis=0, keepdims=True)
                p_ref[g] = p.astype(BF16)

        probabilities(0)

        def body(j, carry):
            accumulate(j - 1)
            probabilities(j)
            return carry

        lax.fori_loop(1, n_kb, body, 0)
        accumulate(n_kb - 1)

    @pl.when(jnp.logical_not(fixed_shift_ok))
    def _():
        def body(j, carry):
            bias = jnp.where(key_ref[j] >= thr, 0.0, NEG_BIG)
            bias = jnp.concatenate([bias] * ATTN_GROUP, axis=1)
            for g in range(ATTN_KV_HEADS):
                s, vtb = logits_t(j, g)
                s = s + bias
                m_old = m_ref[g]
                m_new = jnp.maximum(m_old, jnp.max(s, axis=0, keepdims=True))
                alpha = jnp.exp2((m_old - m_new) * c_exp)
                p = jnp.exp2((s - m_new) * c_exp)
                l_ref[g] = alpha * l_ref[g] + jnp.sum(p, axis=0, keepdims=True)
                acc_ref[g] = alpha * acc_ref[g] + jnp.dot(vtb, p.astype(BF16), preferred_element_type=F32)
                m_ref[g] = m_new
            return carry

        lax.fori_loop(0, n_kb, body, 0)

    n_out = o_ref.shape[1]
    for g in range(ATTN_KV_HEADS):
        out_t = acc_ref[g] / l_ref[g]
        for hh in range(ATTN_GROUP):
            h = g * ATTN_GROUP + hh
            o_ref[0, :, h * ATTN_HEAD_DIM:(h + 1) * ATTN_HEAD_DIM] = (
                out_t[:, hh * tq:(hh + 1) * tq].T[:n_out].astype(o_ref.dtype))


def _rows_out_spec(width, n_steps, out_rows, into, dtype):
    b, rows = out_rows[0], out_rows[1]
    if into is None:
        spec = pl.BlockSpec((1, rows, width), lambda bi, i: (bi, i, 0))
        return spec, jax.ShapeDtypeStruct((b, out_rows[2], width), dtype), [], []
    buf, first_row, rows = into
    assert first_row % rows == 0 and buf.shape[0] == 1 and buf.dtype == dtype
    blk0 = first_row // rows
    spec = pl.BlockSpec((1, rows, width), lambda bi, i: (0, blk0 + bi * n_steps + i, 0))
    return spec, jax.ShapeDtypeStruct(buf.shape, dtype), [pl.BlockSpec(memory_space=pl.ANY)], [buf]


def dsa_attention(q, qi, kx, k, v, ki, *, n_q, tq, tk, q_pos0, k_top, v_is_transposed=False,
                  out_rows=None, into=None):
    b = q.shape[0]
    s = k.shape[1]
    assert n_q % tq == 0 and s % tk == 0 and tq == LANES
    n_kb_max = s // tk
    cols = ATTN_GROUP * tq
    vt = v if v_is_transposed else v.reshape(b, n_kb_max, tk, ATTN_KV_DIM).transpose(0, 1, 3, 2)
    assert vt.shape == (b, n_kb_max, ATTN_KV_DIM, tk)
    kern = functools.partial(_attn_kernel, tq=tq, tk=tk, q_pos0=q_pos0, k_top=k_top, n_kb_max=n_kb_max)
    resident = dict(pipeline_mode=pl.Buffered(1))
    out_spec, out_shape, buf_specs, bufs = _rows_out_spec(
        ATTN_Q_DIM, n_q // tq, (b, tq, out_rows or n_q), into, BF16)
    return pl.pallas_call(
        kern,
        grid=(b, n_q // tq),
        in_specs=[
            pl.BlockSpec((1, ATTN_HEADS, tq, ATTN_HEAD_DIM), lambda bi, i: (bi, 0, i, 0)),
            pl.BlockSpec((1, IDX_HEADS, tq, IDX_HEAD_DIM), lambda bi, i: (bi, 0, i, 0)),
            pl.BlockSpec((1, tq, LANES), lambda bi, i: (bi, i, 0)),
            pl.BlockSpec((1, s, ATTN_KV_DIM), lambda bi, i: (bi, 0, 0), **resident),
            pl.BlockSpec((1, n_kb_max, ATTN_KV_DIM, tk), lambda bi, i: (bi, 0, 0, 0), **resident),
            pl.BlockSpec((1, s, IDX_HEAD_DIM), lambda bi, i: (bi, 0, 0), **resident),
        ] + buf_specs,
        out_specs=out_spec,
        out_shape=out_shape,
        input_output_aliases={6: 0} if bufs else {},
        scratch_shapes=[
            pltpu.VMEM((n_kb_max, tk, tq), jnp.int32),
            pltpu.VMEM((ATTN_KV_HEADS, 1, cols), F32),
            pltpu.VMEM((ATTN_KV_HEADS, 1, cols), F32),
            pltpu.VMEM((ATTN_KV_HEADS, ATTN_HEAD_DIM, cols), F32),
            pltpu.VMEM((1, tq), F32),
            pltpu.VMEM((ATTN_KV_HEADS, tk, cols), BF16),
        ],
        compiler_params=_params("parallel", "arbitrary"),
    )(q, qi, kx, k, vt, ki, *bufs)


_LEVELS = (32, 16, 8, 4, 2, 1)


def _hgrn_tables():
    c = CHUNK
    t = np.arange(c)[:, None]
    u = np.arange(c)[None, :]
    mats = [(u <= t), (u > t)]
    masks = []
    for w in _LEVELS:
        r = (t // (2 * w)) * (2 * w) + w - 1
        upper = (t % (2 * w)) >= w
        mats.append((upper & (u > r) & (u <= t)) | ((~upper) & (u > t) & (u <= r)))
        s = np.arange(c)[None, :]
        masks.append(((t // (2 * w)) == (s // (2 * w))) & upper & ((s % (2 * w)) < w))
    masks.append(t == np.arange(c)[None, :])
    table = np.concatenate(mats, axis=0).astype(np.float32)
    return np.concatenate([table] * 3, axis=1), np.stack(masks).astype(np.float32)


def _hgrn_kernel(q_ref, f_ref, i_ref, gt_ref, lb_ref, gn_ref, tab_ref, msk_ref, s0_ref, *rest, layer):
    o_ref, sfin_ref, st_ref = rest[-3:]
    c = CHUNK
    dh = HGRN_HEAD_DIM
    step = pl.program_id(1)

    @pl.when(step == 0)
    def _():
        for h in range(HGRN_HEADS):
            st_ref[h] = s0_ref[0, h].T

    lb_all = lb_ref[...]
    e = jnp.exp(lb_all - jnp.max(lb_all, axis=0, keepdims=True))
    p_lb = e / jnp.sum(e, axis=0, keepdims=True)
    lower = jnp.sum(p_lb[:layer + 1], axis=0, keepdims=True) - p_lb[0:1]

    def chunk_operands(ci):
        rows = slice(ci * c, (ci + 1) * c)
        f = lower + (1.0 - lower) * jax.nn.sigmoid(f_ref[0, rows])
        kk = 1.0 - f
        lf = jnp.log2(f)
        lf_hi = lf.astype(BF16)
        r1 = lf - lf_hi.astype(F32)
        lf_mid = r1.astype(BF16)
        lf_lo = (r1 - lf_mid.astype(F32)).astype(BF16)
        lf3 = jnp.concatenate([lf_hi, lf_mid, lf_lo], axis=0)
        ex = jnp.exp2(jnp.dot(tab_ref[...], lf3, preferred_element_type=F32))
        qq = q_ref[0, rows]
        vv = i_ref[0, rows]
        ops = dict(q16=qq.astype(BF16), k16=kk.astype(BF16), v16=vv.astype(BF16), vv=vv)
        ops["q_lv"] = [(qq * ex[(2 + li) * c:(3 + li) * c]).astype(BF16) for li in range(len(_LEVELS))]
        ops["k_lv"] = [(kk * ex[(2 + li) * c:(3 + li) * c]).astype(BF16) for li in range(len(_LEVELS))]
        ops["q_in"] = (qq * ex[0:c]).astype(BF16)
        ops["k_out"] = (kk * ex[c:2 * c]).astype(BF16)
        ops["decay"] = ex[c - 1:c]
        ops["gate"] = jax.nn.sigmoid(gt_ref[0, rows])
        return ops

    def chunk_outputs(ci, ops):
        rows = slice(ci * c, (ci + 1) * c)
        scores = []
        for h in range(HGRN_HEADS):
            sl = slice(h * dh, (h + 1) * dh)
            s_h = msk_ref[len(_LEVELS)] * lax.dot_general(
                ops["q16"][:, sl], ops["k16"][:, sl], _NT, preferred_element_type=F32)
            for li in range(len(_LEVELS)):
                s_h = s_h + msk_ref[li] * lax.dot_general(
                    ops["q_lv"][li][:, sl], ops["k_lv"][li][:, sl], _NT, preferred_element_type=F32)
            scores.append(s_h.astype(BF16))
        for h in range(HGRN_HEADS):
            sl = slice(h * dh, (h + 1) * dh)
            st = st_ref[h]
            o_h = jnp.dot(scores[h], ops["v16"][:, sl], preferred_element_type=F32)
            o_h = o_h + lax.dot_general(ops["q_in"][:, sl], st.astype(BF16), _NT, preferred_element_type=F32)
            st_ref[h] = st * ops["decay"][:, sl] + jnp.dot(
                ops["vv"][:, sl].T.astype(BF16), ops["k_out"][:, sl], preferred_element_type=F32)
            o_ref[0, rows, sl] = (_rms(o_h, gn_ref[:, sl]) * ops["gate"][:, sl]).astype(o_ref.dtype)

    n_sub = q_ref.shape[1] // c
    ops = chunk_operands(0)
    for ci in range(n_sub):
        ops_next = chunk_operands(ci + 1) if ci + 1 < n_sub else None
        chunk_outputs(ci, ops)
        ops = ops_next

    @pl.when(step == pl.num_programs(1) - 1)
    def _():
        for h in range(HGRN_HEADS):
            sfin_ref[0, h] = st_ref[h].T


def hgrn_scan(p, lb, gnorm, s0, *, layer, n_rows=None, out_rows=None, into=None, states=None):
    b = p.shape[0]
    l = p.shape[1] if n_rows is None else n_rows
    d = D_MODEL
    c = max(r for r in (4 * CHUNK, 2 * CHUNK, CHUNK) if l % r == 0)
    tab, msk = _hgrn_tables()
    kern = functools.partial(_hgrn_kernel, layer=layer)
    col = lambda j: pl.BlockSpec((1, c, d), lambda bi, t, j=j: (bi, t, j))
    state_spec = pl.BlockSpec((1, HGRN_HEADS, HGRN_HEAD_DIM, HGRN_HEAD_DIM), lambda bi, t: (bi, 0, 0, 0))
    out_spec, out_shape, buf_specs, bufs = _rows_out_spec(d, l // c, (b, c, out_rows or l), into, BF16)
    aliases = {9: 0} if bufs else {}
    states_spec = pl.BlockSpec((None,) + state_spec.block_shape, lambda bi, t: (layer, bi, 0, 0, 0))
    if states is not None:
        aliases[9 + len(bufs)] = 1
        buf_specs = buf_specs + [pl.BlockSpec(memory_space=pl.ANY)]
        bufs = bufs + [states]
    return pl.pallas_call(
        kern,
        grid=(b, l // c),
        in_specs=[
            col(0), col(1), col(2), col(3),
            pl.BlockSpec(lb.shape, lambda bi, t: (0, 0)),
            pl.BlockSpec((1, d), lambda bi, t: (0, 0)),
            pl.BlockSpec(tab.shape, lambda bi, t: (0, 0)),
            pl.BlockSpec(msk.shape, lambda bi, t: (0, 0, 0)),
            state_spec,
        ] + buf_specs,
        out_specs=[out_spec, states_spec],
        out_shape=[out_shape, jax.ShapeDtypeStruct((lb.shape[0],) + s0.shape, F32)],
        input_output_aliases=aliases,
        scratch_shapes=[pltpu.VMEM((HGRN_HEADS, HGRN_HEAD_DIM, HGRN_HEAD_DIM), F32)],
        compiler_params=_params("parallel", "arbitrary"),
    )(p, p, p, p, lb, gnorm.reshape(1, d), jnp.asarray(tab, BF16), jnp.asarray(msk), s0, *bufs)


def _attn_layer(x, g, w_in, w_out, layer, new_kv, cache_k, cache_v, cache_kidx, n_p, b_s, t_s):
    q, qi, kx, new_kv, k16, vt, ki16 = attn_in_proj(x, g, w_in, layer=layer, n_prompt=n_p, new_kv=new_kv)
    k_new, v_new, ki_new = new_kv[3][layer], new_kv[4][layer], new_kv[5][layer]
    past = cache_k.shape[1]
    tq = LANES

    o_p = dsa_attention(
        q[None], qi[None], kx[None], k16[None], vt[None], ki16[None], v_is_transposed=True,
        n_q=n_p, tq=tq, tk=vt.shape[2], q_pos0=0, k_top=min(TOPK_MAX, n_p // 4),
        out_rows=n_p + b_s * t_s)

    s_all = past + t_s
    n_adm_pad = ((past + tq - 1) // CHUNK + 1) * CHUNK
    s_pad = -(-max(s_all, n_adm_pad) // LANES) * LANES
    tk_s = s_pad

    def with_cache(cache, new, width):
        full = jnp.zeros((b_s, s_pad, width), BF16)
        full = lax.dynamic_update_slice(full, cache.reshape(b_s, past, width).astype(BF16), (0, 0, 0))
        return lax.dynamic_update_slice(full, new.astype(BF16).reshape(b_s, t_s, width), (0, past, 0))

    def sample_heads(a, heads, dim):
        a = a[:, n_p:].reshape(heads, b_s, t_s, dim).transpose(1, 0, 2, 3)
        return jnp.pad(a, ((0, 0), (0, 0), (0, tq - t_s), (0, 0)))

    kx_s = jnp.pad(kx[n_p:].reshape(b_s, t_s, LANES), ((0, 0), (0, tq - t_s), (0, 0)))
    o_s = dsa_attention(
        sample_heads(q, ATTN_HEADS, ATTN_HEAD_DIM), sample_heads(qi, IDX_HEADS, IDX_HEAD_DIM), kx_s,
        with_cache(cache_k, k_new, ATTN_KV_DIM), with_cache(cache_v, v_new, ATTN_KV_DIM),
        with_cache(cache_kidx, ki_new, IDX_HEAD_DIM),
        n_q=tq, tq=tq, tk=tk_s, q_pos0=past, k_top=min(TOPK_MAX, s_all // 4),
        into=(o_p, n_p, t_s))

    x = matmul_residual(o_s[0], w_out, x, layer=layer)
    return x, new_kv


def _rec_layer(x, g, w_in, w_out, gnorm, rec_lb, state, layer, new_states, n_p, b_s, t_s):
    p = norm_matmul(x, g, w_in, layer=layer)
    s0_p = jnp.zeros((1,) + state.shape[1:], F32)
    o_p, s_p = hgrn_scan(p[None], rec_lb, gnorm, s0_p, layer=layer, n_rows=n_p, out_rows=x.shape[0],
                         states=new_states[0])
    o_s, s_s = hgrn_scan(p[n_p:].reshape(b_s, t_s, -1), rec_lb, gnorm, state, layer=layer,
                         into=(o_p, n_p, t_s),
                         states=new_states[1])
    x = matmul_residual(o_s[0], w_out, x, layer=layer)
    return x, (s_p, s_s)


def kernel(x_prompt, x_sample, cache_k, cache_v, cache_kidx, state_s, norm_mix, norm_mlp, norm_final,
           attn_w_in, attn_w_out, rec_w_in, rec_w_out, rec_gnorm, rec_lb, mlp_w_up, mlp_w_down):
    b_p, l_p, d = x_prompt.shape
    b_s, t_s, _ = x_sample.shape
    assert b_p == 1
    n_p = b_p * l_p
    x = (x_prompt.reshape(n_p, d), x_sample.reshape(b_s * t_s, d))

    new_states = (None, None)
    new_kv = None
    for layer in range(DEPTH):
        j = layer // N_MIXERS
        if layer % N_MIXERS == 0:
            x, new_kv = _attn_layer(
                x, norm_mix[layer], attn_w_in, attn_w_out, j, new_kv,
                cache_k[j], cache_v[j], cache_kidx[j], n_p, b_s, t_s)
        else:
            x, new_states = _rec_layer(
                x, norm_mix[layer], rec_w_in, rec_w_out, rec_gnorm[j], rec_lb, state_s[j],
                j, new_states, n_p, b_s, t_s)
        last = layer == DEPTH - 1
        x = mlp_residual(x, norm_mlp[layer], mlp_w_up, mlp_w_down, layer=layer,
                         g_final=norm_final if last else None, split=n_p if last else None)

    y_p, y_s = x
    kp, vp, kip, ks, vs, kis = new_kv
    n_l = kp.shape[0]
    heads = (ATTN_KV_HEADS, ATTN_HEAD_DIM)
    return (y_p.reshape(b_p, l_p, d), y_s.reshape(b_s, t_s, d),
            kp.reshape(n_l, b_p, l_p, *heads), vp.reshape(n_l, b_p, l_p, *heads),
            kip.reshape(n_l, b_p, l_p, IDX_HEAD_DIM), new_states[0],
            ks.reshape(n_l, b_s, t_s, *heads), vs.reshape(n_l, b_s, t_s, *heads),
            kis.reshape(n_l, b_s, t_s, IDX_HEAD_DIM), new_states[1])
```
